```python
import math
import jax, jax.numpy as jnp
from jax import lax
import numpy as np

D_MODEL = 4096
BATCH = 4
SEQ = 4096
DEPTH = 4
DEC_BATCH = 16
DEC_SEQ = 64
PAST_LEN = 1024

CHUNK = 64
N_MIXERS = 3
Q_BLOCK = 128
EPS = 1e-6
ROPE_THETA = 500000.0
NEG = -1e30

N_A = len(range(0, DEPTH, N_MIXERS))
N_B = len(range(1, DEPTH, N_MIXERS))
N_C = len(range(2, DEPTH, N_MIXERS))

A_HEADS = 16
A_KV_HEADS = 4
A_GROUP = A_HEADS // A_KV_HEADS
A_DH = D_MODEL // (2 * A_HEADS)
A_Q = A_HEADS * 2 * A_DH
A_KV = A_KV_HEADS * 2 * A_DH

B_HEADS = 32
B_KV_HEADS = 4
B_GROUP = B_HEADS // B_KV_HEADS
B_DH = D_MODEL // B_HEADS
B_IDX_HEADS = 32
B_IDX_DIM = 128
B_TOPK_MAX = 256
B_IN = B_HEADS * B_DH + 2 * B_KV_HEADS * B_DH + B_IDX_HEADS * B_IDX_DIM + B_IDX_DIM + B_IDX_HEADS

C_HEADS = 32
C_DH = D_MODEL // C_HEADS
C_LEFT_CHUNKS = 8
C_WINDOW = C_LEFT_CHUNKS * CHUNK
C_REL_CLIP = 256

D_FF = -(-(8 * D_MODEL) // (3 * 256)) * 256

kernel_name = 'hybrid_streaming_encoder_step'


def rms_norm(x, g):
    xf = x.astype(jnp.float32)
    y = xf * lax.rsqrt(jnp.mean(xf * xf, axis=-1, keepdims=True) + EPS)
    return (y * g.astype(jnp.float32)).astype(x.dtype)


def rope_partial(x, pos):
    d = x.shape[-1]
    r = d // 4
    half = r // 2
    inv = ROPE_THETA ** (-2.0 * jnp.arange(half, dtype=jnp.float32) / r)
    ang = pos.astype(jnp.float32)[:, None] * inv[None, :]
    cos = jnp.cos(ang)[:, None, :]
    sin = jnp.sin(ang)[:, None, :]
    xf = x.astype(jnp.float32)
    x1, x2, rest = xf[..., :half], xf[..., half:r], xf[..., r:]
    out = jnp.concatenate([x1 * cos - x2 * sin, x2 * cos + x1 * sin, rest], axis=-1)
    return out.astype(x.dtype)


def chunk_mask(qpos, kpos):
    return (kpos[None, :] // CHUNK) <= (qpos[:, None] // CHUNK)


def over_query_blocks(fn, qs, qpos):
    T = qpos.shape[0]
    if T <= Q_BLOCK:
        return fn(qs, qpos)
    nb = T // Q_BLOCK
    split = lambda a: jnp.moveaxis(a.reshape((a.shape[0], nb, Q_BLOCK) + a.shape[2:]), 1, 0)
    out = lax.map(lambda args: fn(args[0], args[1]),
                  (jax.tree_util.tree_map(split, qs), qpos.reshape(nb, Q_BLOCK)))
    out = jnp.moveaxis(out, 0, 1)
    return out.reshape((out.shape[0], T) + out.shape[3:])


def diff_mixer(h, pos, past_k, past_v, past_pos, w_in, w_out, lq1, lk1, lq2, lk2, subln_g, lam_init):
    B, T, _ = h.shape
    proj = h @ w_in
    q = rope_partial(proj[..., :A_Q].reshape(B, T, 2 * A_HEADS, A_DH), pos)
    q = q.reshape(B, T, A_KV_HEADS, A_GROUP, 2, A_DH)
    k = rope_partial(proj[..., A_Q:A_Q + A_KV].reshape(B, T, 2 * A_KV_HEADS, A_DH), pos)
    k = k.reshape(B, T, A_KV_HEADS, 2 * A_DH)
    v = proj[..., A_Q + A_KV:].reshape(B, T, A_KV_HEADS, 2 * A_DH)
    kk = jnp.concatenate([past_k.astype(k.dtype), k], axis=1)
    vv = jnp.concatenate([past_v.astype(v.dtype), v], axis=1)
    kpos = jnp.concatenate([past_pos, pos])
    S = kk.shape[1]
    k5 = kk.reshape(B, S, A_KV_HEADS, 2, A_DH)
    vf = vv.astype(jnp.float32)
    f32 = jnp.float32
    lam = (jnp.exp(jnp.sum(lq1.astype(f32) * lk1.astype(f32)))
           - jnp.exp(jnp.sum(lq2.astype(f32) * lk2.astype(f32))) + lam_init)

    def core(qb, qp):
        s = jnp.einsum('bthgcd,bshcd->bhgcts', qb, k5, preferred_element_type=f32) * (A_DH ** -0.5)
        s = jnp.where(chunk_mask(qp, kpos), s, NEG)
        p = jax.nn.softmax(s, axis=-1)
        a = p[:, :, :, 0] - lam * p[:, :, :, 1]
        return jnp.einsum('bhgts,bshe->bthge', a, vf)

    o = over_query_blocks(core, q, pos)
    o = rms_norm(o, subln_g) * (1.0 - lam_init)
    o = o.astype(h.dtype).reshape(B, T, A_Q)
    return o @ w_out, k, v


def dsa_mixer(h, pos, past_k, past_v, past_ik, past_pos, w_in, w_out):
    B, T, _ = h.shape
    proj = h @ w_in
    o1 = B_HEADS * B_DH
    o2 = o1 + B_KV_HEADS * B_DH
    o3 = o2 + B_KV_HEADS * B_DH
    o4 = o3 + B_IDX_HEADS * B_IDX_DIM
    o5 = o4 + B_IDX_DIM
    q = rope_partial(proj[..., :o1].reshape(B, T, B_HEADS, B_DH), pos).reshape(B, T, B_KV_HEADS, B_GROUP, B_DH)
    k = rope_partial(proj[..., o1:o2].reshape(B, T, B_KV_HEADS, B_DH), pos)
    v = proj[..., o2:o3].reshape(B, T, B_KV_HEADS, B_DH)
    iq = rope_partial(proj[..., o3:o4].reshape(B, T, B_IDX_HEADS, B_IDX_DIM), pos)
    ik = rope_partial(proj[..., o4:o5].reshape(B, T, 1, B_IDX_DIM), pos)[:, :, 0]
    iw = proj[..., o5:]
    kk = jnp.concatenate([past_k.astype(k.dtype), k], axis=1)
    vv = jnp.concatenate([past_v.astype(v.dtype), v], axis=1)
    ikk = jnp.concatenate([past_ik.astype(ik.dtype), ik], axis=1)
    kpos = jnp.concatenate([past_pos, pos])
    L = kk.shape[1]
    n_sel = min(B_TOPK_MAX, L // 4)
    f32 = jnp.float32
    gather = jax.vmap(lambda rows, idx: rows[idx])

    def core(qs, qp):
        qb, iqb, iwb = qs
        logits = jax.nn.relu(jnp.einsum('btnd,bsd->btns', iqb, ikk, preferred_element_type=f32) * (B_IDX_DIM ** -0.5))
        score = jnp.einsum('btn,btns->bts', iwb.astype(f32) * (B_IDX_HEADS ** -0.5), logits)
        score = jnp.where(chunk_mask(qp, kpos)[None], score, NEG)
        _, sel = lax.top_k(score, n_sel)
        valid = (kpos[sel] // CHUNK) <= (qp[None, :, None] // CHUNK)
        ks = gather(kk, sel)
        vs = gather(vv, sel).astype(f32)
        s = jnp.einsum('bthgd,btnhd->bthgn', qb, ks, preferred_element_type=f32) * (B_DH ** -0.5)
        s = jnp.where(valid[:, :, None, None, :], s, NEG)
        p = jax.nn.softmax(s, axis=-1)
        return jnp.einsum('bthgn,btnhd->bthgd', p, vs)

    o = over_query_blocks(core, (q, iq, iw), pos)
    o = o.astype(h.dtype).reshape(B, T, B_HEADS * B_DH)
    return o @ w_out, k, v, ik


def band_mixer(h, pos, past_k, past_v, past_pos, w_in, w_out, rel_bias):
    B, T, _ = h.shape
    q, k, v = jnp.split(h @ w_in, 3, axis=-1)
    q = q.reshape(B, T, C_HEADS, C_DH)
    k = k.reshape(B, T, C_HEADS, C_DH)
    v = v.reshape(B, T, C_HEADS, C_DH)
    kk = jnp.concatenate([past_k.astype(k.dtype), k], axis=1)
    vv = jnp.concatenate([past_v.astype(v.dtype), v], axis=1)
    kpos = jnp.concatenate([past_pos, pos])
    W = past_k.shape[1]
    qc = min(CHUNK, T)
    nq = T // qc
    band = W + qc
    f32 = jnp.float32

    def chunk_step(c):
        start = c * qc
        q_c = lax.dynamic_slice_in_dim(q, start, qc, axis=1)
        qp = lax.dynamic_slice_in_dim(pos, start, qc)
        k_c = lax.dynamic_slice_in_dim(kk, start, band, axis=1)
        v_c = lax.dynamic_slice_in_dim(vv, start, band, axis=1).astype(f32)
        kp = lax.dynamic_slice_in_dim(kpos, start, band)
        rel = jnp.clip(qp[:, None] - kp[None, :], -C_REL_CLIP, C_REL_CLIP) + C_REL_CLIP
        bias = jnp.moveaxis(rel_bias[rel], -1, 0).astype(f32)
        qch = qp[:, None] // CHUNK
        kch = kp[None, :] // CHUNK
        ok = (kp[None, :] >= 0) & (kch <= qch) & (qch - kch <= C_LEFT_CHUNKS)
        s = jnp.einsum('bthd,bshd->bhts', q_c, k_c, preferred_element_type=f32) * (C_DH ** -0.5) + bias
        s = jnp.where(ok, s, NEG)
        p = jax.nn.softmax(s, axis=-1)
        return jnp.einsum('bhts,bshd->bthd', p, v_c).astype(h.dtype)

    o = lax.map(chunk_step, jnp.arange(nq, dtype=jnp.int32))
    o = jnp.moveaxis(o, 0, 1).reshape(B, T, C_HEADS * C_DH)
    return o @ w_out, kk, vv


def swiglu(u, w_gu, w_down):
    g, up = jnp.split(u @ w_gu, 2, axis=-1)
    return (jax.nn.silu(g) * up) @ w_down


def trunk(x, pos, hist, P):
    h = x
    a_k, a_v, b_k, b_v, b_ik, c_k, c_v = [], [], [], [], [], [], []
    for i in range(DEPTH):
        j = i // N_MIXERS
        u = rms_norm(h, P['norm_mix_g'][i])
        kind = i % N_MIXERS
        if kind == 0:
            lam_init = 0.8 - 0.6 * math.exp(-0.3 * i)
            o, k, v = diff_mixer(u, pos, hist['a_k'][j], hist['a_v'][j], hist['pos'],
                                 P['a_w_in'][j], P['a_w_out'][j], P['a_lam_q1'][j], P['a_lam_k1'][j],
                                 P['a_lam_q2'][j], P['a_lam_k2'][j], P['a_subln_g'][j], lam_init)
            a_k.append(k)
            a_v.append(v)
        elif kind == 1:
            o, k, v, ik = dsa_mixer(u, pos, hist['b_k'][j], hist['b_v'][j], hist['b_ik'][j], hist['pos'],
                                    P['b_w_in'][j], P['b_w_out'][j])
            b_k.append(k)
            b_v.append(v)
            b_ik.append(ik)
        else:
            o, kk, vv = band_mixer(u, pos, hist['c_k'][j], hist['c_v'][j], hist['c_pos'],
                                   P['c_w_in'][j], P['c_w_out'][j], P['c_rel_bias'][j])
            keep = hist['c_keep']
            c_k.append(kk[:, kk.shape[1] - keep:])
            c_v.append(vv[:, vv.shape[1] - keep:])
        h = h + o
        h = h + swiglu(rms_norm(h, P['norm_ffn_g'][i]), P['ffn_w_gu'][i], P['ffn_w_down'][i])
    y = rms_norm(h, P['norm_out_g'])
    st = lambda xs: jnp.stack(xs, axis=0)
    return y, (st(a_k), st(a_v), st(b_k), st(b_v), st(b_ik), st(c_k), st(c_v))


def setup_inputs(seed: int = 0) -> dict:
    keys = list(jax.random.split(jax.random.key(seed), 32))

    def nrm(shape, scale=1.0):
        return jax.random.normal(keys.pop(), shape, jnp.float32) * scale

    def gain(shape):
        return 1.0 + 0.02 * jax.random.normal(keys.pop(), shape, jnp.float32)

    w_c = min(C_WINDOW, PAST_LEN)
    return {
        'x_prompt': nrm((BATCH, SEQ, D_MODEL)),
        'x_sample': nrm((DEC_BATCH, DEC_SEQ, D_MODEL)),
        'cache_a_k': nrm((N_A, DEC_BATCH, PAST_LEN, A_KV_HEADS, 2 * A_DH)),
        'cache_a_v': nrm((N_A, DEC_BATCH, PAST_LEN, A_KV_HEADS, 2 * A_DH)),
        'cache_b_k': nrm((N_B, DEC_BATCH, PAST_LEN, B_KV_HEADS, B_DH)),
        'cache_b_v': nrm((N_B, DEC_BATCH, PAST_LEN, B_KV_HEADS, B_DH)),
        'cache_b_idx_k': nrm((N_B, DEC_BATCH, PAST_LEN, B_IDX_DIM)),
        'cache_c_k': nrm((N_C, DEC_BATCH, w_c, C_HEADS, C_DH)),
        'cache_c_v': nrm((N_C, DEC_BATCH, w_c, C_HEADS, C_DH)),
        'norm_mix_g': gain((DEPTH, D_MODEL)),
        'norm_ffn_g': gain((DEPTH, D_MODEL)),
        'norm_out_g': gain((D_MODEL,)),
        'a_w_in': nrm((N_A, D_MODEL, A_Q + 2 * A_KV), D_MODEL ** -0.5),
        'a_w_out': nrm((N_A, A_Q, D_MODEL), A_Q ** -0.5),
        'a_lam_q1': nrm((N_A, A_DH), 0.1),
        'a_lam_k1': nrm((N_A, A_DH), 0.1),
        'a_lam_q2': nrm((N_A, A_DH), 0.1),
        'a_lam_k2': nrm((N_A, A_DH), 0.1),
        'a_subln_g': gain((N_A, 2 * A_DH)),
        'b_w_in': nrm((N_B, D_MODEL, B_IN), D_MODEL ** -0.5),
        'b_w_out': nrm((N_B, B_HEADS * B_DH, D_MODEL), (B_HEADS * B_DH) ** -0.5),
        'c_w_in': nrm((N_C, D_MODEL, 3 * C_HEADS * C_DH), D_MODEL ** -0.5),
        'c_w_out': nrm((N_C, C_HEADS * C_DH, D_MODEL), (C_HEADS * C_DH) ** -0.5),
        'c_rel_bias': nrm((N_C, 2 * C_REL_CLIP + 1, C_HEADS), 0.5),
        'ffn_w_gu': nrm((DEPTH, D_MODEL, 2 * D_FF), D_MODEL ** -0.5),
        'ffn_w_down': nrm((DEPTH, D_FF, D_MODEL), D_FF ** -0.5),
    }


def reference(x_prompt, x_sample, cache_a_k, cache_a_v, cache_b_k, cache_b_v, cache_b_idx_k, cache_c_k, cache_c_v,
              norm_mix_g, norm_ffn_g, norm_out_g, a_w_in, a_w_out, a_lam_q1, a_lam_k1, a_lam_q2, a_lam_k2,
              a_subln_g, b_w_in, b_w_out, c_w_in, c_w_out, c_rel_bias, ffn_w_gu, ffn_w_down):
    P = {'norm_mix_g': norm_mix_g, 'norm_ffn_g': norm_ffn_g, 'norm_out_g': norm_out_g,
         'a_w_in': a_w_in, 'a_w_out': a_w_out, 'a_lam_q1': a_lam_q1, 'a_lam_k1': a_lam_k1,
         'a_lam_q2': a_lam_q2, 'a_lam_k2': a_lam_k2, 'a_subln_g': a_subln_g,
         'b_w_in': b_w_in, 'b_w_out': b_w_out, 'c_w_in': c_w_in, 'c_w_out': c_w_out,
         'c_rel_bias': c_rel_bias, 'ffn_w_gu': ffn_w_gu, 'ffn_w_down': ffn_w_down}
    bp, tp = x_prompt.shape[0], x_prompt.shape[1]
    bs, ts = x_sample.shape[0], x_sample.shape[1]
    past = cache_a_k.shape[2]
    w_c = cache_c_k.shape[2]
    dt = x_prompt.dtype

    hist_p = {
        'pos': jnp.zeros((0,), jnp.int32),
        'a_k': jnp.zeros((N_A, bp, 0, A_KV_HEADS, 2 * A_DH), dt),
        'a_v': jnp.zeros((N_A, bp, 0, A_KV_HEADS, 2 * A_DH), dt),
        'b_k': jnp.zeros((N_B, bp, 0, B_KV_HEADS, B_DH), dt),
        'b_v': jnp.zeros((N_B, bp, 0, B_KV_HEADS, B_DH), dt),
        'b_ik': jnp.zeros((N_B, bp, 0, B_IDX_DIM), dt),
        'c_k': jnp.zeros((N_C, bp, C_WINDOW, C_HEADS, C_DH), dt),
        'c_v': jnp.zeros((N_C, bp, C_WINDOW, C_HEADS, C_DH), dt),
        'c_pos': jnp.full((C_WINDOW,), -1, jnp.int32),
        'c_keep': min(C_WINDOW, tp),
    }
    pos_p = jnp.arange(tp, dtype=jnp.int32)
    y_prompt, st_p = trunk(x_prompt, pos_p, hist_p, P)

    hist_s = {
        'pos': jnp.arange(past, dtype=jnp.int32),
        'a_k': cache_a_k, 'a_v': cache_a_v,
        'b_k': cache_b_k, 'b_v': cache_b_v, 'b_ik': cache_b_idx_k,
        'c_k': cache_c_k, 'c_v': cache_c_v,
        'c_pos': jnp.arange(w_c, dtype=jnp.int32) + (past - w_c),
        'c_keep': w_c,
    }
    pos_s = past + jnp.arange(ts, dtype=jnp.int32)
    y_sample, st_s = trunk(x_sample, pos_s, hist_s, P)

    a_k_p, a_v_p, b_k_p, b_v_p, b_ik_p, c_k_p, c_v_p = st_p
    a_k_s, a_v_s, b_k_s, b_v_s, b_ik_s, c_k_s, c_v_s = st_s
    return (y_prompt, y_sample, a_k_p, a_v_p, b_k_p, b_v_p, b_ik_p, c_k_p, c_v_p,
            a_k_s, a_v_s, b_k_s, b_v_s, b_ik_s, c_k_s, c_v_s)
```

```python
import functools
import math

import jax
import jax.numpy as jnp
from jax import lax
from jax.experimental import pallas as pl
from jax.experimental.pallas import tpu as pltpu

F32 = jnp.float32
BF16 = jnp.bfloat16

CHUNK = 64
N_MIXERS = 3
EPS = 1e-6
ROPE_THETA = 500000.0
NEG = -1e30
A_HEADS = 16
A_KV_HEADS = 4
B_HEADS = 32
B_KV_HEADS = 4
B_IDX_HEADS = 32
B_IDX_DIM = 128
B_TOPK_MAX = 256
C_HEADS = 32
C_LEFT_CHUNKS = 8
C_REL_CLIP = 256

LANES = 128
VMEM_LIMIT = 56 * 1024 * 1024
INT_MIN = -2 ** 31


def _cparams(n_grid):
    return pltpu.CompilerParams(dimension_semantics=("arbitrary",) * n_grid,
                                vmem_limit_bytes=VMEM_LIMIT)


def _pick(n, target, mult=16):
    for t in range(min(n, target), 0, -1):
        if n % t == 0 and t % mult == 0:
            return t
    return n


def _rmsnorm_kernel(x_ref, g_ref, o_ref):
    x = x_ref[...]
    y = x * lax.rsqrt(jnp.mean(x * x, axis=-1, keepdims=True) + EPS)
    o_ref[...] = (y * g_ref[...]).astype(o_ref.dtype)


def _rmsnorm(x, g, out_dtype):
    m, d = x.shape
    tm = _pick(m, 512)
    return pl.pallas_call(
        _rmsnorm_kernel,
        grid=(m // tm,),
        in_specs=[pl.BlockSpec((tm, d), lambda i: (i, 0)),
                  pl.BlockSpec((1, d), lambda i: (0, 0))],
        out_specs=pl.BlockSpec((tm, d), lambda i: (i, 0)),
        out_shape=jax.ShapeDtypeStruct((m, d), out_dtype),
        compiler_params=_cparams(1),
        name="rmsnorm",
    )(x, g.reshape(1, d).astype(F32))


def _rope_tables(pos, dh=LANES):
    r = dh // 4
    half = r // 2
    inv = ROPE_THETA ** (-2.0 * jnp.arange(half, dtype=F32) / r)
    ang = pos.astype(F32)[:, None] * inv[None, :]
    cos, sin = jnp.cos(ang), jnp.sin(ang)
    rows = pos.shape[0]
    zeros = jnp.zeros((rows, dh - r), F32)
    zh = jnp.zeros((rows, half), F32)
    c = jnp.concatenate([cos, cos, jnp.ones((rows, dh - r), F32)], axis=1)
    sa = jnp.concatenate([zh, sin, zeros], axis=1)
    sb = jnp.concatenate([-sin, zh, zeros], axis=1)
    return c, sa, sb


def _rope_slab(x, c, sa, sb):
    return x * c + pltpu.roll(x, 16, 1) * sa + pltpu.roll(x, LANES - 16, 1) * sb


def _proj_kernel(*refs, rope, residual, n_out):
    x_ref, w_ref = refs[0], refs[1]
    pos = 2
    if rope:
        c_ref, sa_ref, sb_ref = refs[pos:pos + 3]
        pos += 3
    if residual:
        r_ref = refs[pos]
        pos += 1
    outs = refs[pos:pos + n_out]
    acc = jnp.dot(x_ref[...], w_ref[...], preferred_element_type=F32)
    if rope:
        c, sa, sb = c_ref[...], sa_ref[...], sb_ref[...]
        tn = acc.shape[1]
        acc = jnp.concatenate(
            [_rope_slab(acc[:, s * LANES:(s + 1) * LANES], c, sa, sb) for s in range(tn // LANES)], axis=1)
    if residual:
        acc = acc + r_ref[...]
    for o in outs:
        o[...] = acc.astype(o.dtype)


def _proj(x, w, out_dtypes, rope=None, residual=None, tm_target=1024, tn_target=512):
    m, k = x.shape
    n = w.shape[1]
    tm = _pick(m, tm_target)
    tn = _pick(n, tn_target, LANES)
    in_specs = [pl.BlockSpec((tm, k), lambda i, j: (i, 0)),
                pl.BlockSpec((k, tn), lambda i, j: (0, j))]
    args = [x, w]
    if rope is not None:
        in_specs += [pl.BlockSpec((tm, LANES), lambda i, j: (i, 0))] * 3
        args += list(rope)
    if residual is not None:
        in_specs.append(pl.BlockSpec((tm, tn), lambda i, j: (i, j)))
        args.append(residual)
    outs = pl.pallas_call(
        functools.partial(_proj_kernel, rope=rope is not None, residual=residual is not None,
                          n_out=len(out_dtypes)),
        grid=(m // tm, n // tn),
        in_specs=in_specs,
        out_specs=[pl.BlockSpec((tm, tn), lambda i, j: (i, j)) for _ in out_dtypes],
        out_shape=[jax.ShapeDtypeStruct((m, n), dt) for dt in out_dtypes],
        compiler_params=_cparams(2),
        name="proj",
    )(*args)
    return outs


def _ffn_kernel(h_ref, g_ref, wg_ref, wu_ref, wd_ref, o_ref, xn_ref):
    j = pl.program_id(1)

    @pl.when(j == 0)
    def _():
        x = h_ref[...]
        y = x * lax.rsqrt(jnp.mean(x * x, axis=-1, keepdims=True) + EPS)
        xn_ref[...] = (y * g_ref[...]).astype(BF16)
        o_ref[...] = x

    xn = xn_ref[...]
    gate = jnp.dot(xn, wg_ref[...], preferred_element_type=F32)
    up = jnp.dot(xn, wu_ref[...], preferred_element_type=F32)
    act = (gate * (1.0 / (1.0 + jnp.exp(-gate))) * up).astype(BF16)
    o_ref[...] += jnp.dot(act, wd_ref[...], preferred_element_type=F32)


def _ffn(h, g, w_gu, w_down, tm_target=512, tf=256):
    m, d = h.shape
    f = w_down.shape[0]
    tm = _pick(m, tm_target)
    nf = f // tf
    return pl.pallas_call(
        _ffn_kernel,
        grid=(m // tm, nf),
        in_specs=[pl.BlockSpec((tm, d), lambda i, j: (i, 0), pipeline_mode=pl.Buffered(1)),
                  pl.BlockSpec((1, d), lambda i, j: (0, 0)),
                  pl.BlockSpec((d, tf), lambda i, j: (0, j)),
                  pl.BlockSpec((d, tf), lambda i, j: (0, j + nf)),
                  pl.BlockSpec((tf, d), lambda i, j: (j, 0))],
        out_specs=pl.BlockSpec((tm, d), lambda i, j: (i, 0)),
        out_shape=jax.ShapeDtypeStruct((m, d), F32),
        scratch_shapes=[pltpu.VMEM((tm, d), BF16)],
        compiler_params=_cparams(2),
        name="ffn",
    )(h, g.reshape(1, d).astype(F32), w_gu, w_gu, w_down)


_DN_T = (((1,), (1,)), ((), ()))


def _band_prompt_kernel(q_ref, kp_ref, kc_ref, vp_ref, vc_ref, bias_ref, o_ref, *, scale, tq):
    t = pl.program_id(2)
    k = jnp.concatenate([kp_ref[...], kc_ref[...]], axis=0)
    v = jnp.concatenate([vp_ref[...], vc_ref[...]], axis=0)
    s = lax.dot_general(q_ref[...], k, _DN_T, preferred_element_type=F32) * scale + bias_ref[0]
    kpos = (t - 1) * tq + lax.broadcasted_iota(jnp.int32, s.shape, 1)
    s = jnp.where(kpos >= 0, s, NEG)
    m = jnp.max(s, axis=1, keepdims=True)
    p = jnp.exp(s - m)
    l = jnp.sum(p, axis=1, keepdims=True)
    o = jnp.dot(p.astype(BF16), v, preferred_element_type=F32) / l
    o_ref[...] = o.astype(o_ref.dtype)


def _band_prompt(q, kv, bias_tile, *, batch, seq, heads, dh, row0):
    tq = bias_tile.shape[1]
    nt = seq // tq
    rb0 = row0 // tq
    scale = dh ** -0.5
    cur = lambda h, b, t: (rb0 + b * nt + t, h)
    prev = lambda h, b, t: (rb0 + b * nt + jnp.maximum(t - 1, 0), h)
    cur_v = lambda h, b, t: (rb0 + b * nt + t, heads + h)
    prev_v = lambda h, b, t: (rb0 + b * nt + jnp.maximum(t - 1, 0), heads + h)
    return pl.pallas_call(
        functools.partial(_band_prompt_kernel, scale=scale, tq=tq),
        grid=(heads, batch, nt),
        in_specs=[pl.BlockSpec((tq, dh), cur),
                  pl.BlockSpec((tq, dh), prev), pl.BlockSpec((tq, dh), cur),
                  pl.BlockSpec((tq, dh), prev_v), pl.BlockSpec((tq, dh), cur_v),
                  pl.BlockSpec((1, tq, 2 * tq), lambda h, b, t: (h, 0, 0))],
        out_specs=pl.BlockSpec((tq, dh), lambda h, b, t: (b * nt + t, h)),
        out_shape=jax.ShapeDtypeStruct((batch * seq, heads * dh), BF16),
        compiler_params=_cparams(3),
        name="band_prompt",
    )(q, kv, kv, kv, kv, bias_tile)


def _band_sample_kernel(q_ref, kc_ref, vc_ref, kn_ref, vn_ref, bias_ref, o_ref, *, scale, w):
    q = q_ref[...]
    kc = kc_ref[0].astype(BF16)
    vc = vc_ref[0].astype(BF16)
    bias = bias_ref[0]
    sc = lax.dot_general(q, kc, _DN_T, preferred_element_type=F32) * scale + bias[:, :w]
    sn = lax.dot_general(q, kn_ref[...], _DN_T, preferred_element_type=F32) * scale + bias[:, w:]
    m = jnp.maximum(jnp.max(sc, axis=1, keepdims=True), jnp.max(sn, axis=1, keepdims=True))
    pc = jnp.exp(sc - m)
    pn = jnp.exp(sn - m)
    l = jnp.sum(pc, axis=1, keepdims=True) + jnp.sum(pn, axis=1, keepdims=True)
    o = (jnp.dot(pc.astype(BF16), vc, preferred_element_type=F32)
         + jnp.dot(pn.astype(BF16), vn_ref[...], preferred_element_type=F32)) / l
    o_ref[...] = o.astype(o_ref.dtype)


def _band_sample(q, kv, cache_k, cache_v, bias, *, batch, ts, heads, dh, row0):
    w = cache_k.shape[1]
    rb0 = row0 // ts
    scale = dh ** -0.5
    return pl.pallas_call(
        functools.partial(_band_sample_kernel, scale=scale, w=w),
        grid=(heads, batch),
        in_specs=[pl.BlockSpec((ts, dh), lambda h, b: (rb0 + b, h)),
                  pl.BlockSpec((1, w, dh), lambda h, b: (b, 0, h)),
                  pl.BlockSpec((1, w, dh), lambda h, b: (b, 0, h)),
                  pl.BlockSpec((ts, dh), lambda h, b: (rb0 + b, h)),
                  pl.BlockSpec((ts, dh), lambda h, b: (rb0 + b, heads + h)),
                  pl.BlockSpec((1, ts, w + ts), lambda h, b: (h, 0, 0))],
        out_specs=pl.BlockSpec((ts, dh), lambda h, b: (b, h)),
        out_shape=jax.ShapeDtypeStruct((batch * ts, heads * dh), BF16),
        compiler_params=_cparams(2),
        name="band_sample",
    )(q, cache_k, cache_v, kv, kv, bias)


def _band_bias(rel_bias, qpos, kpos):
    rel = jnp.clip(qpos[:, None] - kpos[None, :], -C_REL_CLIP, C_REL_CLIP) + C_REL_CLIP
    bias = jnp.moveaxis(rel_bias[rel], -1, 0).astype(F32)
    qch = qpos[:, None] // CHUNK
    kch = kpos[None, :] // CHUNK
    ok = (kpos[None, :] >= 0) & (kch <= qch) & (qch - kch <= C_LEFT_CHUNKS)
    return jnp.where(ok[None], bias, NEG)


def _diff_lambda(lq1_ref, lk1_ref, lq2_ref, lk2_ref, lam_init):
    s1 = jnp.sum(lq1_ref[...] * lk1_ref[...], axis=1, keepdims=True)
    s2 = jnp.sum(lq2_ref[...] * lk2_ref[...], axis=1, keepdims=True)
    return jnp.exp(s1) - jnp.exp(s2) + lam_init


def _stack_groups(q_ref, c, group, dh):
    return jnp.concatenate([q_ref[:, (g * 2 + c) * dh:(g * 2 + c + 1) * dh] for g in range(group)], axis=0)


def _diff_finish(o, g_ref, o_ref, *, group, tq, out_scale):
    y = o * lax.rsqrt(jnp.mean(o * o, axis=-1, keepdims=True) + EPS) * g_ref[...] * out_scale
    y = y.astype(o_ref.dtype)
    e = y.shape[1]
    for g in range(group):
        o_ref[:, g * e:(g + 1) * e] = y[g * tq:(g + 1) * tq]


def _diff_prompt_kernel(q_ref, k_ref, v_ref, lq1_ref, lk1_ref, lq2_ref, lk2_ref, g_ref, o_ref,
                        m_scr, l_scr, acc_scr, *, tq, kb, group, dh, scale, lam_init):
    t = pl.program_id(2)
    rows = group * tq
    lam = _diff_lambda(lq1_ref, lk1_ref, lq2_ref, lk2_ref, lam_init)
    m_scr[...] = jnp.full(m_scr.shape, NEG, F32)
    l_scr[...] = jnp.zeros(l_scr.shape, F32)
    acc_scr[...] = jnp.zeros(acc_scr.shape, F32)
    qi = lax.broadcasted_iota(jnp.int32, (rows, 1), 0) % tq
    qend = ((t * tq + qi) // CHUNK + 1) * CHUNK
    n_blocks = ((t + 1) * tq + kb - 1) // kb

    def body(i, carry):
        ks = pl.multiple_of(i * kb, kb)
        kblk = k_ref[pl.ds(ks, kb), :]
        vblk = v_ref[pl.ds(ks, kb), :]
        mask = (ks + lax.broadcasted_iota(jnp.int32, (rows, kb), 1)) < qend
        for c in range(2):
            qc = _stack_groups(q_ref, c, group, dh)
            s = lax.dot_general(qc, kblk[:, c * dh:(c + 1) * dh], _DN_T, preferred_element_type=F32) * scale
            s = jnp.where(mask, s, NEG)
            m_old = m_scr[c]
            m_new = jnp.maximum(m_old, jnp.max(s, axis=1, keepdims=True))
            alpha = jnp.exp(m_old - m_new)
            p = jnp.exp(s - m_new[:, :1])
            l_scr[c] = alpha * l_scr[c] + jnp.sum(p, axis=1, keepdims=True)
            acc_scr[c] = alpha[:, :1] * acc_scr[c] + jnp.dot(p.astype(BF16), vblk, preferred_element_type=F32)
            m_scr[c] = m_new
        return carry

    lax.fori_loop(0, n_blocks, body, 0)
    o = acc_scr[0] / l_scr[0][:, :1] - lam * (acc_scr[1] / l_scr[1][:, :1])
    _diff_finish(o, g_ref, o_ref, group=group, tq=tq, out_scale=1.0 - lam_init)


def _lam_specs(n_grid):
    zero = (lambda *_: (0, 0))
    return [pl.BlockSpec((1, LANES), zero)] * 4


def _diff_prompt(q, k, v, lams, subln_g, *, batch, seq, kvh, group, dh, lam_init, tq=128, kb=512):
    tq = _pick(seq, tq)
    kb = _pick(seq, kb)
    nt = seq // tq
    rows = group * tq
    e = 2 * dh
    return pl.pallas_call(
        functools.partial(_diff_prompt_kernel, tq=tq, kb=kb, group=group, dh=dh, scale=dh ** -0.5,
                          lam_init=lam_init),
        grid=(batch, kvh, nt),
        in_specs=[pl.BlockSpec((tq, group * e), lambda b, h, t: (b * nt + t, h)),
                  pl.BlockSpec((seq, e), lambda b, h, t: (b, h)),
                  pl.BlockSpec((seq, e), lambda b, h, t: (b, h))]
                 + _lam_specs(3) + [pl.BlockSpec((1, e), lambda b, h, t: (0, 0))],
        out_specs=pl.BlockSpec((tq, group * e), lambda b, h, t: (b * nt + t, h)),
        out_shape=jax.ShapeDtypeStruct((batch * seq, kvh * group * e), BF16),
        scratch_shapes=[pltpu.VMEM((2, rows, LANES), F32), pltpu.VMEM((2, rows, LANES), F32),
                        pltpu.VMEM((2, rows, e), F32)],
        compiler_params=_cparams(3),
        name="diff_prompt",
    )(q, k, v, *lams, subln_g)


def _diff_sample_kernel(q_ref, kc_ref, vc_ref, kn_ref, vn_ref, lq1_ref, lk1_ref, lq2_ref, lk2_ref, g_ref, o_ref,
                        *, ts, group, dh, scale, lam_init):
    lam = _diff_lambda(lq1_ref, lk1_ref, lq2_ref, lk2_ref, lam_init)
    kc = kc_ref[0].astype(BF16)
    vc = vc_ref[0].astype(BF16)
    kn = kn_ref[...]
    a_c, a_n = [], []
    for c in range(2):
        qc = _stack_groups(q_ref, c, group, dh)
        sc = lax.dot_general(qc, kc[:, c * dh:(c + 1) * dh], _DN_T, preferred_element_type=F32) * scale
        sn = lax.dot_general(qc, kn[:, c * dh:(c + 1) * dh], _DN_T, preferred_element_type=F32) * scale
        m = jnp.maximum(jnp.max(sc, axis=1, keepdims=True), jnp.max(sn, axis=1, keepdims=True))
        pc = jnp.exp(sc - m)
        pn = jnp.exp(sn - m)
        l = jnp.sum(pc, axis=1, keepdims=True) + jnp.sum(pn, axis=1, keepdims=True)
        a_c.append(pc / l)
        a_n.append(pn / l)
    ac = (a_c[0] - lam * a_c[1]).astype(BF16)
    an = (a_n[0] - lam * a_n[1]).astype(BF16)
    o = (jnp.dot(ac, vc, preferred_element_type=F32) + jnp.dot(an, vn_ref[...], preferred_element_type=F32))
    _diff_finish(o, g_ref, o_ref, group=group, tq=ts, out_scale=1.0 - lam_init)


def _diff_sample(q, k, v, cache_k, cache_v, lams, subln_g, *, batch, ts, kvh, group, dh, lam_init, row0):
    past = cache_k.shape[1]
    rb0 = row0 // ts
    e = 2 * dh
    return pl.pallas_call(
        functools.partial(_diff_sample_kernel, ts=ts, group=group, dh=dh, scale=dh ** -0.5, lam_init=lam_init),
        grid=(batch, kvh),
        in_specs=[pl.BlockSpec((ts, group * e), lambda b, h: (rb0 + b, h)),
                  pl.BlockSpec((1, past, e), lambda b, h: (b, 0, h)),
                  pl.BlockSpec((1, past, e), lambda b, h: (b, 0, h)),
                  pl.BlockSpec((ts, e), lambda b, h: (rb0 + b, h)),
                  pl.BlockSpec((ts, e), lambda b, h: (rb0 + b, h))]
                 + _lam_specs(2) + [pl.BlockSpec((1, e), lambda b, h: (0, 0))],
        out_specs=pl.BlockSpec((ts, group * e), lambda b, h: (b, h)),
        out_shape=jax.ShapeDtypeStruct((batch * ts, kvh * group * e), BF16),
        compiler_params=_cparams(2),
        name="diff_sample",
    )(q, cache_k, cache_v, k, v, *lams, subln_g)


def _dsa_kernel(iq_ref, ik_ref, iwt_ref, q_ref, k_ref, vt_ref, o_ref, key_scr,
                *, tq, kb, causal, s_real, n_sel, n_idx, kvh, group, dh, iscale, scale):
    t = pl.program_id(1)
    s_pad = ik_ref.shape[0]
    lane = lax.broadcasted_iota(jnp.int32, (1, tq), 1)
    if causal:
        qend = ((t * tq + lane) // CHUNK + 1) * CHUNK
        n_blocks = ((t + 1) * tq + kb - 1) // kb
    else:
        qend = jnp.full((1, tq), s_real, jnp.int32)
        n_blocks = s_pad // kb

    w = iwt_ref[...] * iscale

    def score_body(i, carry):
        ks = pl.multiple_of(i * kb, kb)
        ikb = ik_ref[pl.ds(ks, kb), :]
        sc = jnp.zeros((kb, tq), F32)
        for n in range(n_idx):
            logit = lax.dot_general(ikb, iq_ref[:, n * dh:(n + 1) * dh], _DN_T, preferred_element_type=F32)
            sc = sc + w[n:n + 1, :] * jnp.maximum(logit, 0.0)
        bits = lax.bitcast_convert_type(sc, jnp.int32)
        key = jnp.where(bits < 0, bits ^ 0x7FFFFFFF, bits)
        kpos = ks + lax.broadcasted_iota(jnp.int32, (kb, tq), 0)
        key_scr[pl.ds(ks, kb), :] = jnp.where(kpos < qend, key, INT_MIN)
        return carry

    lax.fori_loop(0, n_blocks, score_body, 0)

    def count_ge(cand):
        def body(i, acc):
            ks = pl.multiple_of(i * kb, kb)
            ge = key_scr[pl.ds(ks, kb), :] >= cand
            return acc + jnp.sum(ge.astype(jnp.int32), axis=0, keepdims=True)
        return lax.fori_loop(0, n_blocks, body, jnp.zeros((1, tq), jnp.int32))

    zero = jnp.zeros((1, tq), jnp.int32)
    thr = jnp.where(count_ge(zero) >= n_sel, zero, jnp.full((1, tq), INT_MIN, jnp.int32))

    def bit_body(i, thr):
        cand = thr | lax.shift_left(jnp.int32(1), 30 - i)
        return jnp.where(count_ge(cand) >= n_sel, cand, thr)

    thr = lax.fori_loop(0, 31, bit_body, thr)
    thr = jnp.maximum(thr, INT_MIN + 1)

    for h in range(kvh):
        def head_body(g, carry, h=h):
            c0 = pl.multiple_of((h * group + g) * dh, dh)
            qh = q_ref[:, pl.ds(c0, dh)]

            def kv_body(i, st):
                m, l, acc = st
                ks = pl.multiple_of(i * kb, kb)
                kblk = k_ref[pl.ds(ks, kb), h * dh:(h + 1) * dh]
                s = lax.dot_general(kblk, qh, _DN_T, preferred_element_type=F32) * scale
                sel = key_scr[pl.ds(ks, kb), :] >= thr
                s = jnp.where(sel, s, NEG)
                m_new = jnp.maximum(m, jnp.max(s, axis=0, keepdims=True))
                alpha = jnp.exp(m - m_new)
                p = jnp.where(sel, jnp.exp(s - m_new), 0.0)
                l = alpha * l + jnp.sum(p, axis=0, keepdims=True)
                vtb = vt_ref[0, h * dh:(h + 1) * dh, pl.ds(ks, kb)]
                acc = alpha * acc + jnp.dot(vtb, p.astype(BF16), preferred_element_type=F32)
                return m_new, l, acc

            st0 = (jnp.full((1, tq), NEG, F32), jnp.zeros((1, tq), F32), jnp.zeros((dh, tq), F32))
            _, l, acc = lax.fori_loop(0, n_blocks, kv_body, st0)
            o_ref[pl.ds(c0, dh), :] = (acc / l).astype(o_ref.dtype)
            return carry

        lax.fori_loop(0, group, head_body, 0)


def _dsa(iq, ik, iwt, q, k, vt, *, batch, tq_total, tq, kb, causal, s_real, n_sel):
    dh = ik.shape[1]
    s_pad = vt.shape[2]
    kvh = k.shape[1] // dh
    heads = q.shape[1] // dh
    n_idx = iq.shape[1] // dh
    nt = tq_total // tq
    return pl.pallas_call(
        functools.partial(_dsa_kernel, tq=tq, kb=kb, causal=causal, s_real=s_real, n_sel=n_sel, n_idx=n_idx,
                          kvh=kvh, group=heads // kvh, dh=dh, iscale=(n_idx * dh) ** -0.5, scale=dh ** -0.5),
        grid=(batch, nt),
        in_specs=[pl.BlockSpec((tq, n_idx * dh), lambda b, t: (b * nt + t, 0)),
                  pl.BlockSpec((s_pad, dh), lambda b, t: (b, 0)),
                  pl.BlockSpec((n_idx, tq), lambda b, t: (0, b * nt + t)),
                  pl.BlockSpec((tq, heads * dh), lambda b, t: (b * nt + t, 0)),
                  pl.BlockSpec((s_pad, kvh * dh), lambda b, t: (b, 0)),
                  pl.BlockSpec((1, kvh * dh, s_pad), lambda b, t: (b, 0, 0))],
        out_specs=pl.BlockSpec((heads * dh, tq), lambda b, t: (0, b * nt + t)),
        out_shape=jax.ShapeDtypeStruct((heads * dh, batch * tq_total), BF16),
        scratch_shapes=[pltpu.VMEM((s_pad, tq), jnp.int32)],
        compiler_params=_cparams(2),
        name="dsa",
    )(iq, ik, iwt, q, k, vt)


def _mixer_a(xn, tabs, w_in, lams, subln_g, cache_k, cache_v, lam_init, dims):
    bp, tp, bs, ts, d, past = dims
    mp = bp * tp
    dh = d // (2 * A_HEADS)
    a_q = A_HEADS * 2 * dh
    a_kv = A_KV_HEADS * 2 * dh
    group = A_HEADS // A_KV_HEADS
    (q,) = _proj(xn, w_in[:, :a_q].astype(BF16), (BF16,), rope=tabs)
    k32, k16 = _proj(xn, w_in[:, a_q:a_q + a_kv].astype(BF16), (F32, BF16), rope=tabs)
    v32, v16 = _proj(xn, w_in[:, a_q + a_kv:].astype(BF16), (F32, BF16))
    lams = tuple(x.reshape(1, dh).astype(F32) for x in lams)
    g = subln_g.reshape(1, 2 * dh).astype(F32)
    past = cache_k.shape[1]
    o_p = _diff_prompt(q, k16, v16, lams, g, batch=bp, seq=tp, kvh=A_KV_HEADS, group=group, dh=dh,
                       lam_init=lam_init)
    o_s = _diff_sample(q, k16, v16, cache_k.reshape(bs, past, a_kv), cache_v.reshape(bs, past, a_kv), lams, g,
                       batch=bs, ts=ts, kvh=A_KV_HEADS, group=group, dh=dh, lam_init=lam_init, row0=mp)
    return jnp.concatenate([o_p, o_s], axis=0), k32, v32


def _mixer_b(xn, tabs, w_in, cache_k, cache_v, cache_ik, dims):
    bp, tp, bs, ts, d, past = dims
    mp = bp * tp
    dh = d // B_HEADS
    o1 = B_HEADS * dh
    o2 = o1 + B_KV_HEADS * dh
    o3 = o2 + B_KV_HEADS * dh
    o4 = o3 + B_IDX_HEADS * B_IDX_DIM
    o5 = o4 + B_IDX_DIM
    kvw = B_KV_HEADS * dh
    (q,) = _proj(xn, w_in[:, :o1].astype(BF16), (BF16,), rope=tabs)
    k32, k16 = _proj(xn, w_in[:, o1:o2].astype(BF16), (F32, BF16), rope=tabs)
    v32, v16 = _proj(xn, w_in[:, o2:o3].astype(BF16), (F32, BF16))
    (iq,) = _proj(xn, w_in[:, o3:o4].astype(BF16), (BF16,), rope=tabs)
    ik32, ik16 = _proj(xn, w_in[:, o4:o5].astype(BF16), (F32, BF16), rope=tabs)
    w_iw = jnp.pad(w_in[:, o5:], ((0, 0), (0, LANES - B_IDX_HEADS))).astype(BF16)
    (iw,) = _proj(xn, w_iw, (F32,))
    iwt = iw[:, :B_IDX_HEADS].T

    vt_p = jnp.swapaxes(v16[:mp].reshape(bp, tp, kvw), 1, 2)
    tq_p = _pick(tp, 256, LANES)
    o_p = _dsa(iq, ik16, iwt, q, k16, vt_p, batch=bp, tq_total=tp, tq=tq_p, kb=_pick(tp, 512, LANES),
               causal=True, s_real=tp, n_sel=min(B_TOPK_MAX, tp // 4)).T

    past = cache_k.shape[1]
    s_real = past + ts
    s_pad = -(-s_real // LANES) * LANES
    tq_s = -(-ts // LANES) * LANES

    def keys(cache, new):
        new = new[mp:].reshape(bs, ts, -1)
        kk = jnp.concatenate([cache.reshape(bs, past, -1).astype(BF16), new,
                              jnp.zeros((bs, s_pad - s_real, new.shape[-1]), BF16)], axis=1)
        return kk

    def queries(a):
        a = jnp.pad(a[mp:].reshape(bs, ts, -1), ((0, 0), (0, tq_s - ts), (0, 0)))
        return a.reshape(bs * tq_s, -1)

    kk = keys(cache_k, k16).reshape(bs * s_pad, kvw)
    ikk = keys(cache_ik, ik16).reshape(bs * s_pad, B_IDX_DIM)
    vt_s = jnp.swapaxes(keys(cache_v, v16), 1, 2)
    o_s = _dsa(queries(iq), ikk, queries(iw[:, :B_IDX_HEADS]).T, queries(q), kk, vt_s, batch=bs, tq_total=tq_s,
               tq=tq_s, kb=_pick(s_pad, 512, LANES), causal=False, s_real=s_real,
               n_sel=min(B_TOPK_MAX, s_real // 4))
    o_s = o_s.T.reshape(bs, tq_s, o1)[:, :ts].reshape(bs * ts, o1)
    return jnp.concatenate([o_p, o_s], axis=0), k32, v32, ik32


def _mixer_c(xn, w_in, rel_bias, cache_k, cache_v, dims):
    bp, tp, bs, ts, d, past = dims
    mp = bp * tp
    dh = d // C_HEADS
    hw = C_HEADS * dh
    win = C_LEFT_CHUNKS * CHUNK
    (q,) = _proj(xn, w_in[:, :hw].astype(BF16), (BF16,))
    kv32, kv16 = _proj(xn, w_in[:, hw:].astype(BF16), (F32, BF16))

    i32 = jnp.int32
    chunk_bias = _band_bias(rel_bias, win + jnp.arange(CHUNK, dtype=i32), jnp.arange(win + CHUNK, dtype=i32))
    n_cc = win // CHUNK
    rows = []
    for cc in range(n_cc):
        left = jnp.full((C_HEADS, CHUNK, cc * CHUNK), NEG, F32)
        right = jnp.full((C_HEADS, CHUNK, 2 * win - (win + CHUNK) - cc * CHUNK), NEG, F32)
        rows.append(jnp.concatenate([left, chunk_bias, right], axis=2))
    bias_tile = jnp.concatenate(rows, axis=1)
    o_p = _band_prompt(q, kv16, bias_tile, batch=bp, seq=tp, heads=C_HEADS, dh=dh, row0=0)

    w_c = cache_k.shape[1]
    qpos = past + jnp.arange(ts, dtype=i32)
    kpos = jnp.concatenate([jnp.arange(w_c, dtype=i32) + (past - w_c), qpos])
    o_s = _band_sample(q, kv16, cache_k.reshape(bs, w_c, hw), cache_v.reshape(bs, w_c, hw),
                       _band_bias(rel_bias, qpos, kpos), batch=bs, ts=ts, heads=C_HEADS, dh=dh, row0=mp)
    return jnp.concatenate([o_p, o_s], axis=0), kv32[:, :hw], kv32[:, hw:]


def kernel(x_prompt, x_sample, cache_a_k, cache_a_v, cache_b_k, cache_b_v, cache_b_idx_k, cache_c_k, cache_c_v,
           norm_mix_g, norm_ffn_g, norm_out_g, a_w_in, a_w_out, a_lam_q1, a_lam_k1, a_lam_q2, a_lam_k2,
           a_subln_g, b_w_in, b_w_out, c_w_in, c_w_out, c_rel_bias, ffn_w_gu, ffn_w_down):
    bp, tp, d = x_prompt.shape
    bs, ts, _ = x_sample.shape
    past = cache_a_k.shape[2]
    depth = norm_mix_g.shape[0]
    assert ts <= CHUNK and past % CHUNK == 0 and tp % CHUNK == 0
    mp, ms = bp * tp, bs * ts
    dims = (bp, tp, bs, ts, d, past)
    i32 = jnp.int32

    h = jnp.concatenate([x_prompt.reshape(mp, d), x_sample.reshape(ms, d)], axis=0)
    pos = jnp.concatenate([jnp.tile(jnp.arange(tp, dtype=i32), bp), jnp.tile(past + jnp.arange(ts, dtype=i32), bs)])
    tabs = _rope_tables(pos)

    def split(a, tail):
        return a[:mp].reshape((bp, tp) + tail), a[mp:].reshape((bs, ts) + tail)

    st = {name: [] for name in ("a_k", "a_v", "b_k", "b_v", "b_ik", "c_k", "c_v")}
    for i in range(depth):
        j = i // N_MIXERS
        kind = i % N_MIXERS
        xn = _rmsnorm(h, norm_mix_g[i], BF16)
        if kind == 0:
            lam_init = 0.8 - 0.6 * math.exp(-0.3 * i)
            o, k, v = _mixer_a(xn, tabs, a_w_in[j], (a_lam_q1[j], a_lam_k1[j], a_lam_q2[j], a_lam_k2[j]),
                               a_subln_g[j], cache_a_k[j], cache_a_v[j], lam_init, dims)
            tail = cache_a_k.shape[3:]
            st["a_k"].append(split(k, tail))
            st["a_v"].append(split(v, tail))
            w_out = a_w_out[j]
        elif kind == 1:
            o, k, v, ik = _mixer_b(xn, tabs, b_w_in[j], cache_b_k[j], cache_b_v[j], cache_b_idx_k[j], dims)
            tail = cache_b_k.shape[3:]
            st["b_k"].append(split(k, tail))
            st["b_v"].append(split(v, tail))
            st["b_ik"].append(split(ik, cache_b_idx_k.shape[3:]))
            w_out = b_w_out[j]
        else:
            o, k, v = _mixer_c(xn, c_w_in[j], c_rel_bias[j], cache_c_k[j], cache_c_v[j], dims)
            tail = cache_c_k.shape[3:]
            new = []
            for a, cache in ((k, cache_c_k[j]), (v, cache_c_v[j])):
                a_p, a_s = split(a, tail)
                win = C_LEFT_CHUNKS * CHUNK
                kk_p = jnp.concatenate([jnp.zeros((bp, win) + tail, a.dtype), a_p], axis=1)
                kk_s = jnp.concatenate([cache, a_s], axis=1)
                keep_p, keep_s = min(win, tp), cache.shape[1]
                new.append((kk_p[:, kk_p.shape[1] - keep_p:], kk_s[:, kk_s.shape[1] - keep_s:]))
            st["c_k"].append(new[0])
            st["c_v"].append(new[1])
            w_out = c_w_out[j]
        (h,) = _proj(o, w_out.astype(BF16), (F32,), residual=h)
        h = _ffn(h, norm_ffn_g[i], ffn_w_gu[i].astype(BF16), ffn_w_down[i].astype(BF16))

    y = _rmsnorm(h, norm_out_g, F32)
    y_p, y_s = split(y, (d,))
    stack = lambda name, g: jnp.stack([pair[g] for pair in st[name]], axis=0)
    names = ("a_k", "a_v", "b_k", "b_v", "b_ik", "c_k", "c_v")
    return (y_p, y_s) + tuple(stack(n, 0) for n in names) + tuple(stack(n, 1) for n in names)
```

```python
import functools
import math

import jax
import jax.numpy as jnp
import numpy as np
from jax import lax
from jax.experimental import pallas as pl
from jax.experimental.pallas import tpu as pltpu

F32 = jnp.float32
BF16 = jnp.bfloat16

CHUNK = 64
N_MIXERS = 3
EPS = 1e-6
ROPE_THETA = 500000.0
NEG = -1e30
A_HEADS = 16
A_KV_HEADS = 4
B_HEADS = 32
B_KV_HEADS = 4
B_IDX_HEADS = 32
B_IDX_DIM = 128
B_TOPK_MAX = 256
C_HEADS = 32
C_LEFT_CHUNKS = 8
C_REL_CLIP = 256

LANES = 128
VMEM_LIMIT = 56 * 1024 * 1024
INT_MIN = -2 ** 31
LOG2E = math.log2(math.e)


def _cparams(n_grid):
    return pltpu.CompilerParams(dimension_semantics=("arbitrary",) * n_grid,
                                vmem_limit_bytes=VMEM_LIMIT)


def _pick(n, target, mult=16):
    for t in range(min(n, target), 0, -1):
        if n % t == 0 and t % mult == 0:
            return t
    return n


def _rmsnorm_kernel(x_ref, g_ref, o_ref):
    x = x_ref[...]
    y = x * lax.rsqrt(jnp.mean(x * x, axis=-1, keepdims=True) + EPS)
    o_ref[...] = (y * g_ref[...]).astype(o_ref.dtype)


def _rmsnorm(x, g, out_dtype):
    m, d = x.shape
    tm = _pick(m, 512)
    return pl.pallas_call(
        _rmsnorm_kernel,
        grid=(m // tm,),
        in_specs=[pl.BlockSpec((tm, d), lambda i: (i, 0)),
                  pl.BlockSpec((1, d), lambda i: (0, 0))],
        out_specs=pl.BlockSpec((tm, d), lambda i: (i, 0)),
        out_shape=jax.ShapeDtypeStruct((m, d), out_dtype),
        compiler_params=_cparams(1),
        name="rmsnorm",
    )(x, g.reshape(1, d).astype(F32))


def _rope_tables(pos, dh=LANES):
    r = dh // 4
    half = r // 2
    inv = ROPE_THETA ** (-2.0 * jnp.arange(half, dtype=F32) / r)
    ang = pos.astype(F32)[:, None] * inv[None, :]
    cos, sin = jnp.cos(ang), jnp.sin(ang)
    rows = pos.shape[0]
    zeros = jnp.zeros((rows, dh - r), F32)
    zh = jnp.zeros((rows, half), F32)
    c = jnp.concatenate([cos, cos, jnp.ones((rows, dh - r), F32)], axis=1)
    sa = jnp.concatenate([zh, sin, zeros], axis=1)
    sb = jnp.concatenate([-sin, zh, zeros], axis=1)
    return c, sa, sb


def _rope_slab(x, c, sa, sb):
    return x * c + pltpu.roll(x, 16, 1) * sa + pltpu.roll(x, LANES - 16, 1) * sb


def _proj_kernel(*refs, rope, residual, n_out, out_scale):
    x_ref, w_ref = refs[0], refs[1]
    pos = 2
    if rope:
        c_ref, sa_ref, sb_ref = refs[pos:pos + 3]
        pos += 3
    if residual:
        r_ref = refs[pos]
        pos += 1
    outs = refs[pos:pos + n_out]
    acc = jnp.dot(x_ref[...], w_ref[...], preferred_element_type=F32)
    if rope:
        c, sa, sb = c_ref[...], sa_ref[...], sb_ref[...]
        tn = acc.shape[1]
        acc = jnp.concatenate(
            [_rope_slab(acc[:, s * LANES:(s + 1) * LANES], c, sa, sb) for s in range(tn // LANES)], axis=1)
    if residual:
        acc = acc + r_ref[...]
    if out_scale != 1.0:
        acc = acc * out_scale
    for o in outs:
        o[...] = acc.astype(o.dtype)


def _proj(x, w, out_dtypes, rope=None, residual=None, out_scale=1.0, tm_target=1024, tn_target=512):
    m, k = x.shape
    n = w.shape[1]
    tm = _pick(m, tm_target)
    tn = _pick(n, tn_target, LANES)
    in_specs = [pl.BlockSpec((tm, k), lambda i, j: (i, 0)),
                pl.BlockSpec((k, tn), lambda i, j: (0, j))]
    args = [x, w]
    if rope is not None:
        in_specs += [pl.BlockSpec((tm, LANES), lambda i, j: (i, 0))] * 3
        args += list(rope)
    if residual is not None:
        in_specs.append(pl.BlockSpec((tm, tn), lambda i, j: (i, j)))
        args.append(residual)
    outs = pl.pallas_call(
        functools.partial(_proj_kernel, rope=rope is not None, residual=residual is not None,
                          n_out=len(out_dtypes), out_scale=out_scale),
        grid=(m // tm, n // tn),
        in_specs=in_specs,
        out_specs=[pl.BlockSpec((tm, tn), lambda i, j: (i, j)) for _ in out_dtypes],
        out_shape=[jax.ShapeDtypeStruct((m, n), dt) for dt in out_dtypes],
        compiler_params=_cparams(2),
        name="proj",
    )(*args)
    return outs


def _ffn_kernel(h_ref, g_ref, wg_ref, wu_ref, wd_ref, o_ref, xn_ref):
    j = pl.program_id(1)

    @pl.when(j == 0)
    def _():
        x = h_ref[...]
        y = x * lax.rsqrt(jnp.mean(x * x, axis=-1, keepdims=True) + EPS)
        xn_ref[...] = (y * g_ref[...]).astype(BF16)
        o_ref[...] = x

    xn = xn_ref[...]
    gate = jnp.dot(xn, wg_ref[...], preferred_element_type=F32)
    up = jnp.dot(xn, wu_ref[...], preferred_element_type=F32)
    act = (gate * (1.0 / (1.0 + jnp.exp(-gate))) * up).astype(BF16)
    o_ref[...] += jnp.dot(act, wd_ref[...], preferred_element_type=F32)


def _ffn(h, g, w_gu, w_down, tm_target=512, tf=256):
    m, d = h.shape
    f = w_down.shape[0]
    tm = _pick(m, tm_target)
    nf = f // tf
    return pl.pallas_call(
        _ffn_kernel,
        grid=(m // tm, nf),
        in_specs=[pl.BlockSpec((tm, d), lambda i, j: (i, 0), pipeline_mode=pl.Buffered(1)),
                  pl.BlockSpec((1, d), lambda i, j: (0, 0)),
                  pl.BlockSpec((d, tf), lambda i, j: (0, j)),
                  pl.BlockSpec((d, tf), lambda i, j: (0, j + nf)),
                  pl.BlockSpec((tf, d), lambda i, j: (j, 0))],
        out_specs=pl.BlockSpec((tm, d), lambda i, j: (i, 0)),
        out_shape=jax.ShapeDtypeStruct((m, d), F32),
        scratch_shapes=[pltpu.VMEM((tm, d), BF16)],
        compiler_params=_cparams(2),
        name="ffn",
    )(h, g.reshape(1, d).astype(F32), w_gu, w_gu, w_down)


_DN_T = (((1,), (1,)), ((), ()))


def _band_prompt_kernel(q_ref, kp_ref, kc_ref, vp_ref, vc_ref, bias_ref, o_ref, *, scale, tq):
    t = pl.program_id(2)
    k = jnp.concatenate([kp_ref[...], kc_ref[...]], axis=0)
    v = jnp.concatenate([vp_ref[...], vc_ref[...]], axis=0)
    s = lax.dot_general(q_ref[...], k, _DN_T, preferred_element_type=F32) * scale + bias_ref[0]
    kpos = (t - 1) * tq + lax.broadcasted_iota(jnp.int32, s.shape, 1)
    s = jnp.where(kpos >= 0, s, NEG)
    m = jnp.max(s, axis=1, keepdims=True)
    p = jnp.exp(s - m)
    l = jnp.sum(p, axis=1, keepdims=True)
    o = jnp.dot(p.astype(BF16), v, preferred_element_type=F32) / l
    o_ref[...] = o.astype(o_ref.dtype)


def _band_prompt(q, kv, bias_tile, *, batch, seq, heads, dh, row0):
    tq = bias_tile.shape[1]
    nt = seq // tq
    rb0 = row0 // tq
    scale = dh ** -0.5
    cur = lambda h, b, t: (rb0 + b * nt + t, h)
    prev = lambda h, b, t: (rb0 + b * nt + jnp.maximum(t - 1, 0), h)
    cur_v = lambda h, b, t: (rb0 + b * nt + t, heads + h)
    prev_v = lambda h, b, t: (rb0 + b * nt + jnp.maximum(t - 1, 0), heads + h)
    return pl.pallas_call(
        functools.partial(_band_prompt_kernel, scale=scale, tq=tq),
        grid=(heads, batch, nt),
        in_specs=[pl.BlockSpec((tq, dh), cur),
                  pl.BlockSpec((tq, dh), prev), pl.BlockSpec((tq, dh), cur),
                  pl.BlockSpec((tq, dh), prev_v), pl.BlockSpec((tq, dh), cur_v),
                  pl.BlockSpec((1, tq, 2 * tq), lambda h, b, t: (h, 0, 0))],
        out_specs=pl.BlockSpec((tq, dh), lambda h, b, t: (b * nt + t, h)),
        out_shape=jax.ShapeDtypeStruct((batch * seq, heads * dh), BF16),
        compiler_params=_cparams(3),
        name="band_prompt",
    )(q, kv, kv, kv, kv, bias_tile)


def _band_sample_kernel(q_ref, kc_ref, vc_ref, kn_ref, vn_ref, bias_ref, o_ref, *, scale, w):
    q = q_ref[...]
    kc = kc_ref[0].astype(BF16)
    vc = vc_ref[0].astype(BF16)
    bias = bias_ref[0]
    sc = lax.dot_general(q, kc, _DN_T, preferred_element_type=F32) * scale + bias[:, :w]
    sn = lax.dot_general(q, kn_ref[...], _DN_T, preferred_element_type=F32) * scale + bias[:, w:]
    m = jnp.maximum(jnp.max(sc, axis=1, keepdims=True), jnp.max(sn, axis=1, keepdims=True))
    pc = jnp.exp(sc - m)
    pn = jnp.exp(sn - m)
    l = jnp.sum(pc, axis=1, keepdims=True) + jnp.sum(pn, axis=1, keepdims=True)
    o = (jnp.dot(pc.astype(BF16), vc, preferred_element_type=F32)
         + jnp.dot(pn.astype(BF16), vn_ref[...], preferred_element_type=F32)) / l
    o_ref[...] = o.astype(o_ref.dtype)


def _band_sample(q, kv, cache_k, cache_v, bias, *, batch, ts, heads, dh, row0):
    w = cache_k.shape[1]
    rb0 = row0 // ts
    scale = dh ** -0.5
    return pl.pallas_call(
        functools.partial(_band_sample_kernel, scale=scale, w=w),
        grid=(heads, batch),
        in_specs=[pl.BlockSpec((ts, dh), lambda h, b: (rb0 + b, h)),
                  pl.BlockSpec((1, w, dh), lambda h, b: (b, 0, h)),
                  pl.BlockSpec((1, w, dh), lambda h, b: (b, 0, h)),
                  pl.BlockSpec((ts, dh), lambda h, b: (rb0 + b, h)),
                  pl.BlockSpec((ts, dh), lambda h, b: (rb0 + b, heads + h)),
                  pl.BlockSpec((1, ts, w + ts), lambda h, b: (h, 0, 0))],
        out_specs=pl.BlockSpec((ts, dh), lambda h, b: (b, h)),
        out_shape=jax.ShapeDtypeStruct((batch * ts, heads * dh), BF16),
        compiler_params=_cparams(2),
        name="band_sample",
    )(q, cache_k, cache_v, kv, kv, bias)


def _band_bias(rel_bias, q0, nq, k0, nk):
    diag = np.arange(nq + nk - 1) - (nq - 1)
    idx = np.clip(q0 - k0 - diag, -C_REL_CLIP, C_REL_CLIP) + C_REL_CLIP
    r = rel_bias[idx].T.astype(F32)
    bias = jnp.stack([r[:, nq - 1 - i:nq - 1 - i + nk] for i in range(nq)], axis=1)
    qpos = q0 + np.arange(nq)[:, None]
    kpos = k0 + np.arange(nk)[None, :]
    qch, kch = qpos // CHUNK, kpos // CHUNK
    ok = (kpos >= 0) & (kch <= qch) & (qch - kch <= C_LEFT_CHUNKS)
    return jnp.where(ok[None], bias, NEG)


def _diff_lambda(lq1_ref, lk1_ref, lq2_ref, lk2_ref, lam_init):
    s1 = jnp.sum(lq1_ref[...] * lk1_ref[...], axis=1, keepdims=True)
    s2 = jnp.sum(lq2_ref[...] * lk2_ref[...], axis=1, keepdims=True)
    return jnp.exp(s1) - jnp.exp(s2) + lam_init


def _stack_groups(q_ref, c, group, dh):
    return jnp.concatenate([q_ref[:, (g * 2 + c) * dh:(g * 2 + c + 1) * dh] for g in range(group)], axis=0)


def _diff_finish(o, g_ref, o_ref, *, group, tq, out_scale):
    y = o * lax.rsqrt(jnp.mean(o * o, axis=-1, keepdims=True) + EPS) * g_ref[...] * out_scale
    y = y.astype(o_ref.dtype)
    e = y.shape[1]
    for g in range(group):
        o_ref[:, g * e:(g + 1) * e] = y[g * tq:(g + 1) * tq]


def _diff_prompt_kernel(q_ref, k_ref, vt_ref, lq1_ref, lk1_ref, lq2_ref, lk2_ref, g_ref, o_ref, acc_scr,
                        *, tq, kb, group, dh, lam_init):
    t = pl.program_id(2)
    rows = group * tq
    e = 2 * dh
    lam = _diff_lambda(lq1_ref, lk1_ref, lq2_ref, lk2_ref, lam_init)
    acc_scr[...] = jnp.zeros(acc_scr.shape, F32)
    n_blocks = ((t + 1) * tq + kb - 1) // kb

    def block(i, st, masked):
        ks = pl.multiple_of(i * kb, kb)
        kblk = k_ref[pl.ds(ks, kb), :]
        vtb = vt_ref[0, :, pl.ds(ks, kb)]
        if masked:
            qi = lax.broadcasted_iota(jnp.int32, (kb, rows), 1) % tq
            qend = ((t * tq + qi) // CHUNK + 1) * CHUNK
            kpos = ks + lax.broadcasted_iota(jnp.int32, (kb, rows), 0)
            bias = jnp.where(kpos < qend, 0.0, NEG)
        out = []
        for c in range(2):
            m, l = st[2 * c], st[2 * c + 1]
            qc = _stack_groups(q_ref, c, group, dh)
            s = lax.dot_general(kblk[:, c * dh:(c + 1) * dh], qc, _DN_T, preferred_element_type=F32)
            if masked:
                s = s + bias
            m_new = jnp.maximum(m, jnp.max(s, axis=0, keepdims=True))
            alpha = jnp.exp2(m - m_new)
            p = jnp.exp2(s - m_new)
            l = alpha * l + jnp.sum(p, axis=0, keepdims=True)
            acc_scr[c] = alpha * acc_scr[c] + jnp.dot(vtb, p.astype(BF16), preferred_element_type=F32)
            out += [m_new, l]
        return tuple(out)

    row = lambda v: jnp.full((1, rows), v, F32)
    st = lax.fori_loop(0, n_blocks - 1, functools.partial(block, masked=False), (row(NEG), row(0.0)) * 2)
    _, l0, _, l1 = block(n_blocks - 1, st, True)
    o = acc_scr[0] / l0 - lam * (acc_scr[1] / l1)
    y = o * lax.rsqrt(jnp.mean(o * o, axis=0, keepdims=True) + EPS) * (1.0 - lam_init)
    gain = g_ref[...]
    for g in range(group):
        o_ref[g * e:(g + 1) * e, :] = (y[:, g * tq:(g + 1) * tq] * gain).astype(o_ref.dtype)


def _lam_specs(n_grid):
    zero = (lambda *_: (0, 0))
    return [pl.BlockSpec((1, LANES), zero)] * 4


def _diff_prompt(q, k, vt, lams, subln_g, *, batch, seq, kvh, group, dh, lam_init, tq=LANES, kb=512):
    kb = _pick(seq, kb, LANES)
    nt = seq // tq
    rows = group * tq
    e = 2 * dh
    gain = jnp.broadcast_to(subln_g.reshape(e, 1), (e, tq))
    return pl.pallas_call(
        functools.partial(_diff_prompt_kernel, tq=tq, kb=kb, group=group, dh=dh, lam_init=lam_init),
        grid=(batch, kvh, nt),
        in_specs=[pl.BlockSpec((tq, group * e), lambda b, h, t: (b * nt + t, h)),
                  pl.BlockSpec((seq, e), lambda b, h, t: (b, h)),
                  pl.BlockSpec((1, e, seq), lambda b, h, t: (b, h, 0))]
                 + _lam_specs(3) + [pl.BlockSpec((e, tq), lambda b, h, t: (0, 0))],
        out_specs=pl.BlockSpec((group * e, tq), lambda b, h, t: (h, b * nt + t)),
        out_shape=jax.ShapeDtypeStruct((kvh * group * e, batch * seq), BF16),
        scratch_shapes=[pltpu.VMEM((2, e, rows), F32)],
        compiler_params=_cparams(3),
        name="diff_prompt",
    )(q, k, vt, *lams, gain)


def _diff_sample_kernel(q_ref, kc_ref, vc_ref, kn_ref, vn_ref, lq1_ref, lk1_ref, lq2_ref, lk2_ref, g_ref, o_ref,
                        *, ts, group, dh, lam_init):
    lam = _diff_lambda(lq1_ref, lk1_ref, lq2_ref, lk2_ref, lam_init)
    kc = kc_ref[0].astype(BF16)
    vc = vc_ref[0].astype(BF16)
    kn = kn_ref[...]
    a_c, a_n = [], []
    for c in range(2):
        qc = _stack_groups(q_ref, c, group, dh)
        sc = lax.dot_general(qc, kc[:, c * dh:(c + 1) * dh], _DN_T, preferred_element_type=F32)
        sn = lax.dot_general(qc, kn[:, c * dh:(c + 1) * dh], _DN_T, preferred_element_type=F32)
        m = jnp.maximum(jnp.max(sc, axis=1, keepdims=True), jnp.max(sn, axis=1, keepdims=True))
        pc = jnp.exp2(sc - m)
        pn = jnp.exp2(sn - m)
        l = jnp.sum(pc, axis=1, keepdims=True) + jnp.sum(pn, axis=1, keepdims=True)
        a_c.append(pc / l)
        a_n.append(pn / l)
    ac = (a_c[0] - lam * a_c[1]).astype(BF16)
    an = (a_n[0] - lam * a_n[1]).astype(BF16)
    o = (jnp.dot(ac, vc, preferred_element_type=F32) + jnp.dot(an, vn_ref[...], preferred_element_type=F32))
    _diff_finish(o, g_ref, o_ref, group=group, tq=ts, out_scale=1.0 - lam_init)


def _diff_sample(q, k, v, cache_k, cache_v, lams, subln_g, *, batch, ts, kvh, group, dh, lam_init, row0):
    past = cache_k.shape[1]
    rb0 = row0 // ts
    e = 2 * dh
    return pl.pallas_call(
        functools.partial(_diff_sample_kernel, ts=ts, group=group, dh=dh, lam_init=lam_init),
        grid=(batch, kvh),
        in_specs=[pl.BlockSpec((ts, group * e), lambda b, h: (rb0 + b, h)),
                  pl.BlockSpec((1, past, e), lambda b, h: (b, 0, h)),
                  pl.BlockSpec((1, past, e), lambda b, h: (b, 0, h)),
                  pl.BlockSpec((ts, e), lambda b, h: (rb0 + b, h)),
                  pl.BlockSpec((ts, e), lambda b, h: (rb0 + b, h))]
                 + _lam_specs(2) + [pl.BlockSpec((1, e), lambda b, h: (0, 0))],
        out_specs=pl.BlockSpec((ts, group * e), lambda b, h: (b, h)),
        out_shape=jax.ShapeDtypeStruct((batch * ts, kvh * group * e), BF16),
        compiler_params=_cparams(2),
        name="diff_sample",
    )(q, cache_k, cache_v, k, v, *lams, subln_g)


def _dsa_kernel(iq_ref, ik_ref, iwt_ref, q_ref, k_ref, vt_ref, o_ref, key_scr, acc_scr,
                *, tq, kb, causal, s_real, n_sel, n_idx, kvh, group, dh, iscale):
    t = pl.program_id(1)
    s_pad = ik_ref.shape[0]
    lane = lax.broadcasted_iota(jnp.int32, (1, tq), 1)
    if causal:
        qend = ((t * tq + lane) // CHUNK + 1) * CHUNK
        n_blocks = ((t + 1) * tq + kb - 1) // kb
    else:
        qend = jnp.full((1, tq), s_real, jnp.int32)
        n_blocks = s_pad // kb

    w = iwt_ref[...] * iscale

    def score_body(i, carry):
        ks = pl.multiple_of(i * kb, kb)
        ikb = ik_ref[pl.ds(ks, kb), :]
        sc = jnp.zeros((kb, tq), F32)
        for n in range(n_idx):
            logit = lax.dot_general(ikb, iq_ref[:, n * dh:(n + 1) * dh], _DN_T, preferred_element_type=F32)
            sc = sc + w[n:n + 1, :] * jnp.maximum(logit, 0.0)
        bits = lax.bitcast_convert_type(sc, jnp.int32)
        key = jnp.where(bits < 0, bits ^ 0x7FFFFFFF, bits)
        kpos = ks + lax.broadcasted_iota(jnp.int32, (kb, tq), 0)
        key_scr[pl.ds(ks, kb), :] = jnp.where(kpos < qend, key, INT_MIN)
        return carry

    lax.fori_loop(0, n_blocks, score_body, 0)

    def count_ge(cand):
        def body(i, acc):
            ks = pl.multiple_of(i * kb, kb)
            ge = key_scr[pl.ds(ks, kb), :] >= cand
            return acc + jnp.sum(ge.astype(jnp.int32), axis=0, keepdims=True)
        return lax.fori_loop(0, n_blocks, body, jnp.zeros((1, tq), jnp.int32))

    zero = jnp.zeros((1, tq), jnp.int32)
    thr = jnp.where(count_ge(zero) >= n_sel, zero, jnp.full((1, tq), INT_MIN, jnp.int32))

    def bit_body(i, thr):
        cand = thr | lax.shift_left(jnp.int32(1), 30 - i)
        return jnp.where(count_ge(cand) >= n_sel, cand, thr)

    thr = lax.fori_loop(0, 31, bit_body, thr)
    thr = jnp.maximum(thr, INT_MIN + 1)

    def bias_body(i, carry):
        ks = pl.multiple_of(i * kb, kb)
        bias = jnp.where(key_scr[pl.ds(ks, kb), :] >= thr, 0.0, NEG)
        key_scr[pl.ds(ks, kb), :] = lax.bitcast_convert_type(bias, jnp.int32)
        return carry

    lax.fori_loop(0, n_blocks, bias_body, 0)

    cols = group * tq
    for h in range(kvh):
        acc_scr[...] = jnp.zeros(acc_scr.shape, F32)

        def kv_body(i, st, h=h):
            m, l = st
            ks = pl.multiple_of(i * kb, kb)
            kblk = k_ref[pl.ds(ks, kb), h * dh:(h + 1) * dh]
            qh = jnp.concatenate([q_ref[:, (h * group + g) * dh:(h * group + g + 1) * dh] for g in range(group)],
                                 axis=0)
            bias = lax.bitcast_convert_type(key_scr[pl.ds(ks, kb), :], F32)
            s = lax.dot_general(kblk, qh, _DN_T, preferred_element_type=F32)
            s = s + jnp.concatenate([bias] * group, axis=1)
            m_new = jnp.maximum(m, jnp.max(s, axis=0, keepdims=True))
            alpha = jnp.exp2(m - m_new)
            p = jnp.exp2(s - m_new)
            l = alpha * l + jnp.sum(p, axis=0, keepdims=True)
            vtb = vt_ref[0, h * dh:(h + 1) * dh, pl.ds(ks, kb)]
            acc_scr[...] = alpha * acc_scr[...] + jnp.dot(vtb, p.astype(BF16), preferred_element_type=F32)
            return m_new, l

        _, l = lax.fori_loop(0, n_blocks, kv_body, (jnp.full((1, cols), NEG, F32), jnp.zeros((1, cols), F32)))
        out = (acc_scr[...] / l).astype(o_ref.dtype)
        for g in range(group):
            o_ref[(h * group + g) * dh:(h * group + g + 1) * dh, :] = out[:, g * tq:(g + 1) * tq]


def _dsa(iq, ik, iwt, q, k, vt, *, batch, tq_total, tq, kb, causal, s_real, n_sel):
    dh = ik.shape[1]
    s_pad = vt.shape[2]
    kvh = k.shape[1] // dh
    heads = q.shape[1] // dh
    n_idx = iq.shape[1] // dh
    nt = tq_total // tq
    return pl.pallas_call(
        functools.partial(_dsa_kernel, tq=tq, kb=kb, causal=causal, s_real=s_real, n_sel=n_sel, n_idx=n_idx,
                          kvh=kvh, group=heads // kvh, dh=dh, iscale=(n_idx * dh) ** -0.5),
        grid=(batch, nt),
        in_specs=[pl.BlockSpec((tq, n_idx * dh), lambda b, t: (b * nt + t, 0)),
                  pl.BlockSpec((s_pad, dh), lambda b, t: (b, 0)),
                  pl.BlockSpec((n_idx, tq), lambda b, t: (0, b * nt + t)),
                  pl.BlockSpec((tq, heads * dh), lambda b, t: (b * nt + t, 0)),
                  pl.BlockSpec((s_pad, kvh * dh), lambda b, t: (b, 0)),
                  pl.BlockSpec((1, kvh * dh, s_pad), lambda b, t: (b, 0, 0))],
        out_specs=pl.BlockSpec((heads * dh, tq), lambda b, t: (0, b * nt + t)),
        out_shape=jax.ShapeDtypeStruct((heads * dh, batch * tq_total), BF16),
        scratch_shapes=[pltpu.VMEM((s_pad, tq), jnp.int32), pltpu.VMEM((dh, heads // kvh * tq), F32)],
        compiler_params=_cparams(2),
        name="dsa",
    )(iq, ik, iwt, q, k, vt)


def _mixer_a(xn, tabs, w_in, lams, subln_g, cache_k, cache_v, lam_init, dims):
    bp, tp, bs, ts, d, past = dims
    mp = bp * tp
    dh = d // (2 * A_HEADS)
    a_q = A_HEADS * 2 * dh
    a_kv = A_KV_HEADS * 2 * dh
    group = A_HEADS // A_KV_HEADS
    (q,) = _proj(xn, w_in[:, :a_q].astype(BF16), (BF16,), rope=tabs, out_scale=dh ** -0.5 * LOG2E)
    k32, k16 = _proj(xn, w_in[:, a_q:a_q + a_kv].astype(BF16), (F32, BF16), rope=tabs)
    v32, v16 = _proj(xn, w_in[:, a_q + a_kv:].astype(BF16), (F32, BF16))
    lams = tuple(x.reshape(1, dh).astype(F32) for x in lams)
    subln_g = subln_g.astype(F32)
    vt_p = jnp.swapaxes(v16[:mp].reshape(bp, tp, a_kv), 1, 2)
    o_p = _diff_prompt(q, k16, vt_p, lams, subln_g, batch=bp, seq=tp, kvh=A_KV_HEADS, group=group, dh=dh,
                       lam_init=lam_init).T
    o_s = _diff_sample(q, k16, v16, cache_k.reshape(bs, past, a_kv), cache_v.reshape(bs, past, a_kv), lams,
                       subln_g.reshape(1, 2 * dh), batch=bs, ts=ts, kvh=A_KV_HEADS, group=group, dh=dh,
                       lam_init=lam_init, row0=mp)
    return jnp.concatenate([o_p, o_s], axis=0), k32, v32


def _mixer_b(xn, tabs, w_in, cache_k, cache_v, cache_ik, dims):
    bp, tp, bs, ts, d, past = dims
    mp = bp * tp
    dh = d // B_HEADS
    o1 = B_HEADS * dh
    o2 = o1 + B_KV_HEADS * dh
    o3 = o2 + B_KV_HEADS * dh
    o4 = o3 + B_IDX_HEADS * B_IDX_DIM
    o5 = o4 + B_IDX_DIM
    kvw = B_KV_HEADS * dh
    (q,) = _proj(xn, w_in[:, :o1].astype(BF16), (BF16,), rope=tabs, out_scale=dh ** -0.5 * LOG2E)
    k32, k16 = _proj(xn, w_in[:, o1:o2].astype(BF16), (F32, BF16), rope=tabs)
    v32, v16 = _proj(xn, w_in[:, o2:o3].astype(BF16), (F32, BF16))
    (iq,) = _proj(xn, w_in[:, o3:o4].astype(BF16), (BF16,), rope=tabs)
    ik32, ik16 = _proj(xn, w_in[:, o4:o5].astype(BF16), (F32, BF16), rope=tabs)
    w_iw = jnp.pad(w_in[:, o5:], ((0, 0), (0, LANES - B_IDX_HEADS))).astype(BF16)
    (iw,) = _proj(xn, w_iw, (F32,))
    iwt = iw[:, :B_IDX_HEADS].T

    vt_p = jnp.swapaxes(v16[:mp].reshape(bp, tp, kvw), 1, 2)
    tq_p = _pick(tp, 256, LANES)
    o_p = _dsa(iq, ik16, iwt, q, k16, vt_p, batch=bp, tq_total=tp, tq=tq_p, kb=_pick(tp, 512, LANES),
               causal=True, s_real=tp, n_sel=min(B_TOPK_MAX, tp // 4)).T

    past = cache_k.shape[1]
    s_real = past + ts
    s_pad = -(-s_real // LANES) * LANES
    tq_s = -(-ts // LANES) * LANES

    def keys(cache, new):
        new = new[mp:].reshape(bs, ts, -1)
        kk = jnp.concatenate([cache.reshape(bs, past, -1).astype(BF16), new,
                              jnp.zeros((bs, s_pad - s_real, new.shape[-1]), BF16)], axis=1)
        return kk

    def queries(a):
        a = jnp.pad(a[mp:].reshape(bs, ts, -1), ((0, 0), (0, tq_s - ts), (0, 0)))
        return a.reshape(bs * tq_s, -1)

    kk = keys(cache_k, k16).reshape(bs * s_pad, kvw)
    ikk = keys(cache_ik, ik16).reshape(bs * s_pad, B_IDX_DIM)
    vt_s = jnp.swapaxes(keys(cache_v, v16), 1, 2)
    o_s = _dsa(queries(iq), ikk, queries(iw[:, :B_IDX_HEADS]).T, queries(q), kk, vt_s, batch=bs, tq_total=tq_s,
               tq=tq_s, kb=_pick(s_pad, 512, LANES), causal=False, s_real=s_real,
               n_sel=min(B_TOPK_MAX, s_real // 4))
    o_s = o_s.T.reshape(bs, tq_s, o1)[:, :ts].reshape(bs * ts, o1)
    return jnp.concatenate([o_p, o_s], axis=0), k32, v32, ik32


def _mixer_c(xn, w_in, rel_bias, cache_k, cache_v, dims):
    bp, tp, bs, ts, d, past = dims
    mp = bp * tp
    dh = d // C_HEADS
    hw = C_HEADS * dh
    win = C_LEFT_CHUNKS * CHUNK
    (q,) = _proj(xn, w_in[:, :hw].astype(BF16), (BF16,))
    kv32, kv16 = _proj(xn, w_in[:, hw:].astype(BF16), (F32, BF16))

    chunk_bias = _band_bias(rel_bias, win, CHUNK, 0, win + CHUNK)
    bias_tile = jnp.full((C_HEADS, win, 2 * win), NEG, F32)
    for cc in range(win // CHUNK):
        bias_tile = lax.dynamic_update_slice(bias_tile, chunk_bias, (0, cc * CHUNK, cc * CHUNK))
    o_p = _band_prompt(q, kv16, bias_tile, batch=bp, seq=tp, heads=C_HEADS, dh=dh, row0=0)

    w_c = cache_k.shape[1]
    o_s = _band_sample(q, kv16, cache_k.reshape(bs, w_c, hw), cache_v.reshape(bs, w_c, hw),
                       _band_bias(rel_bias, past, ts, past - w_c, w_c + ts),
                       batch=bs, ts=ts, heads=C_HEADS, dh=dh, row0=mp)
    return jnp.concatenate([o_p, o_s], axis=0), kv32[:, :hw], kv32[:, hw:]


def kernel(x_prompt, x_sample, cache_a_k, cache_a_v, cache_b_k, cache_b_v, cache_b_idx_k, cache_c_k, cache_c_v,
           norm_mix_g, norm_ffn_g, norm_out_g, a_w_in, a_w_out, a_lam_q1, a_lam_k1, a_lam_q2, a_lam_k2,
           a_subln_g, b_w_in, b_w_out, c_w_in, c_w_out, c_rel_bias, ffn_w_gu, ffn_w_down):
    bp, tp, d = x_prompt.shape
    bs, ts, _ = x_sample.shape
    past = cache_a_k.shape[2]
    depth = norm_mix_g.shape[0]
    assert ts <= CHUNK and past % CHUNK == 0 and tp % CHUNK == 0
    mp, ms = bp * tp, bs * ts
    dims = (bp, tp, bs, ts, d, past)
    i32 = jnp.int32

    h = jnp.concatenate([x_prompt.reshape(mp, d), x_sample.reshape(ms, d)], axis=0)
    pos = jnp.concatenate([jnp.tile(jnp.arange(tp, dtype=i32), bp), jnp.tile(past + jnp.arange(ts, dtype=i32), bs)])
    tabs = _rope_tables(pos)

    def split(a, tail):
        return a[:mp].reshape((bp, tp) + tail), a[mp:].reshape((bs, ts) + tail)

    st = {name: [] for name in ("a_k", "a_v", "b_k", "b_v", "b_ik", "c_k", "c_v")}
    for i in range(depth):
        j = i // N_MIXERS
        kind = i % N_MIXERS
        xn = _rmsnorm(h, norm_mix_g[i], BF16)
        if kind == 0:
            lam_init = 0.8 - 0.6 * math.exp(-0.3 * i)
            o, k, v = _mixer_a(xn, tabs, a_w_in[j], (a_lam_q1[j], a_lam_k1[j], a_lam_q2[j], a_lam_k2[j]),
                               a_subln_g[j], cache_a_k[j], cache_a_v[j], lam_init, dims)
            tail = cache_a_k.shape[3:]
            st["a_k"].append(split(k, tail))
            st["a_v"].append(split(v, tail))
            w_out = a_w_out[j]
        elif kind == 1:
            o, k, v, ik = _mixer_b(xn, tabs, b_w_in[j], cache_b_k[j], cache_b_v[j], cache_b_idx_k[j], dims)
            tail = cache_b_k.shape[3:]
            st["b_k"].append(split(k, tail))
            st["b_v"].append(split(v, tail))
            st["b_ik"].append(split(ik, cache_b_idx_k.shape[3:]))
            w_out = b_w_out[j]
        else:
            o, k, v = _mixer_c(xn, c_w_in[j], c_rel_bias[j], cache_c_k[j], cache_c_v[j], dims)
            tail = cache_c_k.shape[3:]
            new = []
            for a, cache in ((k, cache_c_k[j]), (v, cache_c_v[j])):
                a_p, a_s = split(a, tail)
                win = C_LEFT_CHUNKS * CHUNK
                kk_p = jnp.concatenate([jnp.zeros((bp, win) + tail, a.dtype), a_p], axis=1)
                kk_s = jnp.concatenate([cache, a_s], axis=1)
                keep_p, keep_s = min(win, tp), cache.shape[1]
                new.append((kk_p[:, kk_p.shape[1] - keep_p:], kk_s[:, kk_s.shape[1] - keep_s:]))
            st["c_k"].append(new[0])
            st["c_v"].append(new[1])
            w_out = c_w_out[j]
        (h,) = _proj(o, w_out.astype(BF16), (F32,), residual=h)
        h = _ffn(h, norm_ffn_g[i], ffn_w_gu[i].astype(BF16), ffn_w_down[i].astype(BF16))

    y = _rmsnorm(h, norm_out_g, F32)
    y_p, y_s = split(y, (d,))
    stack = lambda name, g: jnp.stack([pair[g] for pair in st[name]], axis=0)
    names = ("a_k", "a_v", "b_k", "b_v", "b_ik", "c_k", "c_v")
    return (y_p, y_s) + tuple(stack(n, 0) for n in names) + tuple(stack(n, 1) for n in names)
```

```python
import functools
import math

import jax
import jax.numpy as jnp
import numpy as np
from jax import lax
from jax.experimental import pallas as pl
from jax.experimental.pallas import tpu as pltpu

F32 = jnp.float32
BF16 = jnp.bfloat16

CHUNK = 64
N_MIXERS = 3
EPS = 1e-6
ROPE_THETA = 500000.0
NEG = -1e30
A_HEADS = 16
A_KV_HEADS = 4
B_HEADS = 32
B_KV_HEADS = 4
B_IDX_HEADS = 32
B_IDX_DIM = 128
B_TOPK_MAX = 256
C_HEADS = 32
C_LEFT_CHUNKS = 8
C_REL_CLIP = 256

LANES = 128
VMEM_LIMIT = 56 * 1024 * 1024
INT_MIN = -2 ** 31
LOG2E = math.log2(math.e)


def _cparams(n_grid):
    return pltpu.CompilerParams(dimension_semantics=("arbitrary",) * n_grid,
                                vmem_limit_bytes=VMEM_LIMIT)


def _pick(n, target, mult=16):
    for t in range(min(n, target), 0, -1):
        if n % t == 0 and t % mult == 0:
            return t
    return n


def _rmsnorm_kernel(x_ref, g_ref, o_ref):
    x = x_ref[...]
    y = x * lax.rsqrt(jnp.mean(x * x, axis=-1, keepdims=True) + EPS)
    o_ref[...] = (y * g_ref[...]).astype(o_ref.dtype)


def _rmsnorm(x, g, out_dtype):
    m, d = x.shape
    tm = _pick(m, 512)
    return pl.pallas_call(
        _rmsnorm_kernel,
        grid=(m // tm,),
        in_specs=[pl.BlockSpec((tm, d), lambda i: (i, 0)),
                  pl.BlockSpec((1, d), lambda i: (0, 0))],
        out_specs=pl.BlockSpec((tm, d), lambda i: (i, 0)),
        out_shape=jax.ShapeDtypeStruct((m, d), out_dtype),
        compiler_params=_cparams(1),
        name="rmsnorm",
    )(x, g.reshape(1, d).astype(F32))


def _rope_tables(pos, dh=LANES):
    r = dh // 4
    half = r // 2
    inv = ROPE_THETA ** (-2.0 * jnp.arange(half, dtype=F32) / r)
    ang = pos.astype(F32)[:, None] * inv[None, :]
    cos, sin = jnp.cos(ang), jnp.sin(ang)
    rows = pos.shape[0]
    zeros = jnp.zeros((rows, dh - r), F32)
    zh = jnp.zeros((rows, half), F32)
    c = jnp.concatenate([cos, cos, jnp.ones((rows, dh - r), F32)], axis=1)
    sa = jnp.concatenate([zh, sin, zeros], axis=1)
    sb = jnp.concatenate([-sin, zh, zeros], axis=1)
    return c, sa, sb


def _rope_slab(x, c, sa, sb):
    return x * c + pltpu.roll(x, 16, 1) * sa + pltpu.roll(x, LANES - 16, 1) * sb


def _proj_kernel(*refs, rope, residual, n_out, out_scale):
    x_ref, w_ref = refs[0], refs[1]
    pos = 2
    if rope:
        c_ref, sa_ref, sb_ref = refs[pos:pos + 3]
        pos += 3
    if residual:
        r_ref = refs[pos]
        pos += 1
    outs = refs[pos:pos + n_out]
    acc = jnp.dot(x_ref[...], w_ref[...], preferred_element_type=F32)
    if rope:
        c, sa, sb = c_ref[...], sa_ref[...], sb_ref[...]
        tn = acc.shape[1]
        acc = jnp.concatenate(
            [_rope_slab(acc[:, s * LANES:(s + 1) * LANES], c, sa, sb) for s in range(tn // LANES)], axis=1)
    if residual:
        acc = acc + r_ref[...]
    if out_scale != 1.0:
        acc = acc * out_scale
    for o in outs:
        o[...] = acc.astype(o.dtype)


def _proj(x, w, out_dtypes, rope=None, residual=None, out_scale=1.0, tm_target=1024, tn_target=512):
    m, k = x.shape
    n = w.shape[1]
    tm = _pick(m, tm_target)
    tn = _pick(n, tn_target, LANES)
    in_specs = [pl.BlockSpec((tm, k), lambda i, j: (i, 0)),
                pl.BlockSpec((k, tn), lambda i, j: (0, j))]
    args = [x, w]
    if rope is not None:
        in_specs += [pl.BlockSpec((tm, LANES), lambda i, j: (i, 0))] * 3
        args += list(rope)
    if residual is not None:
        in_specs.append(pl.BlockSpec((tm, tn), lambda i, j: (i, j)))
        args.append(residual)
    outs = pl.pallas_call(
        functools.partial(_proj_kernel, rope=rope is not None, residual=residual is not None,
                          n_out=len(out_dtypes), out_scale=out_scale),
        grid=(m // tm, n // tn),
        in_specs=in_specs,
        out_specs=[pl.BlockSpec((tm, tn), lambda i, j: (i, j)) for _ in out_dtypes],
        out_shape=[jax.ShapeDtypeStruct((m, n), dt) for dt in out_dtypes],
        compiler_params=_cparams(2),
        name="proj",
    )(*args)
    return outs


def _ffn_kernel(h_ref, g_ref, wg_ref, wu_ref, wd_ref, o_ref, xn_ref):
    j = pl.program_id(1)

    @pl.when(j == 0)
    def _():
        x = h_ref[...]
        y = x * lax.rsqrt(jnp.mean(x * x, axis=-1, keepdims=True) + EPS)
        xn_ref[...] = (y * g_ref[...]).astype(BF16)
        o_ref[...] = x

    xn = xn_ref[...]
    gate = jnp.dot(xn, wg_ref[...], preferred_element_type=F32)
    up = jnp.dot(xn, wu_ref[...], preferred_element_type=F32)
    act = (gate * (1.0 / (1.0 + jnp.exp(-gate))) * up).astype(BF16)
    o_ref[...] += jnp.dot(act, wd_ref[...], preferred_element_type=F32)


def _ffn(h, g, w_gu, w_down, layer, tm_target=512, tf=256):
    m, d = h.shape
    f = w_down.shape[1]
    tm = _pick(m, tm_target)
    nf = f // tf
    return pl.pallas_call(
        _ffn_kernel,
        grid=(m // tm, nf),
        in_specs=[pl.BlockSpec((tm, d), lambda i, j: (i, 0), pipeline_mode=pl.Buffered(1)),
                  pl.BlockSpec((1, d), lambda i, j: (0, 0)),
                  pl.BlockSpec((None, d, tf), lambda i, j: (layer, 0, j)),
                  pl.BlockSpec((None, d, tf), lambda i, j: (layer, 0, j + nf)),
                  pl.BlockSpec((None, tf, d), lambda i, j: (layer, j, 0))],
        out_specs=pl.BlockSpec((tm, d), lambda i, j: (i, 0)),
        out_shape=jax.ShapeDtypeStruct((m, d), F32),
        scratch_shapes=[pltpu.VMEM((tm, d), BF16)],
        compiler_params=_cparams(2),
        name="ffn",
    )(h, g.reshape(1, d).astype(F32), w_gu, w_gu, w_down)


_DN_T = (((1,), (1,)), ((), ()))


def _band_prompt_kernel(q_ref, kp_ref, kc_ref, vp_ref, vc_ref, bias_ref, o_ref, *, scale, tq):
    t = pl.program_id(2)
    k = jnp.concatenate([kp_ref[...], kc_ref[...]], axis=0)
    v = jnp.concatenate([vp_ref[...], vc_ref[...]], axis=0)
    s = lax.dot_general(q_ref[...], k, _DN_T, preferred_element_type=F32) * scale + bias_ref[0]
    kpos = (t - 1) * tq + lax.broadcasted_iota(jnp.int32, s.shape, 1)
    s = jnp.where(kpos >= 0, s, NEG)
    m = jnp.max(s, axis=1, keepdims=True)
    p = jnp.exp(s - m)
    l = jnp.sum(p, axis=1, keepdims=True)
    o = jnp.dot(p.astype(BF16), v, preferred_element_type=F32) / l
    o_ref[...] = o.astype(o_ref.dtype)


def _band_prompt(q, kv, bias_tile, *, batch, seq, heads, dh, row0):
    tq = bias_tile.shape[1]
    nt = seq // tq
    rb0 = row0 // tq
    scale = dh ** -0.5
    cur = lambda h, b, t: (rb0 + b * nt + t, h)
    prev = lambda h, b, t: (rb0 + b * nt + jnp.maximum(t - 1, 0), h)
    cur_v = lambda h, b, t: (rb0 + b * nt + t, heads + h)
    prev_v = lambda h, b, t: (rb0 + b * nt + jnp.maximum(t - 1, 0), heads + h)
    return pl.pallas_call(
        functools.partial(_band_prompt_kernel, scale=scale, tq=tq),
        grid=(heads, batch, nt),
        in_specs=[pl.BlockSpec((tq, dh), cur),
                  pl.BlockSpec((tq, dh), prev), pl.BlockSpec((tq, dh), cur),
                  pl.BlockSpec((tq, dh), prev_v), pl.BlockSpec((tq, dh), cur_v),
                  pl.BlockSpec((1, tq, 2 * tq), lambda h, b, t: (h, 0, 0))],
        out_specs=pl.BlockSpec((tq, dh), lambda h, b, t: (b * nt + t, h)),
        out_shape=jax.ShapeDtypeStruct((batch * seq, heads * dh), BF16),
        compiler_params=_cparams(3),
        name="band_prompt",
    )(q, kv, kv, kv, kv, bias_tile)


def _band_sample_kernel(q_ref, kc_ref, vc_ref, kn_ref, vn_ref, bias_ref, o_ref, *, scale, w):
    q = q_ref[...]
    kc = kc_ref[0].astype(BF16)
    vc = vc_ref[0].astype(BF16)
    bias = bias_ref[0]
    sc = lax.dot_general(q, kc, _DN_T, preferred_element_type=F32) * scale + bias[:, :w]
    sn = lax.dot_general(q, kn_ref[...], _DN_T, preferred_element_type=F32) * scale + bias[:, w:]
    m = jnp.maximum(jnp.max(sc, axis=1, keepdims=True), jnp.max(sn, axis=1, keepdims=True))
    pc = jnp.exp(sc - m)
    pn = jnp.exp(sn - m)
    l = jnp.sum(pc, axis=1, keepdims=True) + jnp.sum(pn, axis=1, keepdims=True)
    o = (jnp.dot(pc.astype(BF16), vc, preferred_element_type=F32)
         + jnp.dot(pn.astype(BF16), vn_ref[...], preferred_element_type=F32)) / l
    o_ref[...] = o.astype(o_ref.dtype)


def _band_sample(q, kv, cache_k, cache_v, bias, *, batch, ts, heads, dh, row0):
    w = cache_k.shape[1]
    rb0 = row0 // ts
    scale = dh ** -0.5
    return pl.pallas_call(
        functools.partial(_band_sample_kernel, scale=scale, w=w),
        grid=(heads, batch),
        in_specs=[pl.BlockSpec((ts, dh), lambda h, b: (rb0 + b, h)),
                  pl.BlockSpec((1, w, dh), lambda h, b: (b, 0, h)),
                  pl.BlockSpec((1, w, dh), lambda h, b: (b, 0, h)),
                  pl.BlockSpec((ts, dh), lambda h, b: (rb0 + b, h)),
                  pl.BlockSpec((ts, dh), lambda h, b: (rb0 + b, heads + h)),
                  pl.BlockSpec((1, ts, w + ts), lambda h, b: (h, 0, 0))],
        out_specs=pl.BlockSpec((ts, dh), lambda h, b: (b, h)),
        out_shape=jax.ShapeDtypeStruct((batch * ts, heads * dh), BF16),
        compiler_params=_cparams(2),
        name="band_sample",
    )(q, cache_k, cache_v, kv, kv, bias)


def _band_bias(rel_bias, q0, nq, k0, nk):
    diag = np.arange(nq + nk - 1) - (nq - 1)
    idx = np.clip(q0 - k0 - diag, -C_REL_CLIP, C_REL_CLIP) + C_REL_CLIP
    r = rel_bias[idx].T.astype(F32)
    bias = jnp.stack([r[:, nq - 1 - i:nq - 1 - i + nk] for i in range(nq)], axis=1)
    qpos = q0 + np.arange(nq)[:, None]
    kpos = k0 + np.arange(nk)[None, :]
    qch, kch = qpos // CHUNK, kpos // CHUNK
    ok = (kpos >= 0) & (kch <= qch) & (qch - kch <= C_LEFT_CHUNKS)
    return jnp.where(ok[None], bias, NEG)


def _diff_lambda(lq1_ref, lk1_ref, lq2_ref, lk2_ref, lam_init):
    s1 = jnp.sum(lq1_ref[...] * lk1_ref[...], axis=1, keepdims=True)
    s2 = jnp.sum(lq2_ref[...] * lk2_ref[...], axis=1, keepdims=True)
    return jnp.exp(s1) - jnp.exp(s2) + lam_init


def _stack_groups(q_ref, c, group, dh):
    return jnp.concatenate([q_ref[:, (g * 2 + c) * dh:(g * 2 + c + 1) * dh] for g in range(group)], axis=0)


def _diff_finish(o, g_ref, o_ref, *, group, tq, out_scale):
    y = o * lax.rsqrt(jnp.mean(o * o, axis=-1, keepdims=True) + EPS) * g_ref[...] * out_scale
    y = y.astype(o_ref.dtype)
    e = y.shape[1]
    for g in range(group):
        o_ref[:, g * e:(g + 1) * e] = y[g * tq:(g + 1) * tq]


def _diff_prompt_kernel(q_ref, k_ref, vt_ref, lq1_ref, lk1_ref, lq2_ref, lk2_ref, g_ref, o_ref,
                        acc_scr, s_scr, m_scr, l_scr, *, tq, kb, group, dh, lam_init):
    t = pl.program_id(2)
    rows = group * tq
    e = 2 * dh
    lam = _diff_lambda(lq1_ref, lk1_ref, lq2_ref, lk2_ref, lam_init)
    acc_scr[...] = jnp.zeros(acc_scr.shape, F32)
    m_scr[...] = jnp.full(m_scr.shape, NEG, F32)
    l_scr[...] = jnp.zeros(l_scr.shape, F32)
    n_blocks = ((t + 1) * tq + kb - 1) // kb
    n_free = n_blocks - 1

    def scores(i, slot):
        ks = pl.multiple_of(i * kb, kb)
        kblk = k_ref[pl.ds(ks, kb), :]
        for c in range(2):
            qc = _stack_groups(q_ref, c, group, dh)
            s_scr[slot, c] = lax.dot_general(kblk[:, c * dh:(c + 1) * dh], qc, _DN_T,
                                             preferred_element_type=F32)

    def accumulate(i, slot, masked):
        ks = pl.multiple_of(i * kb, kb)
        vtb = vt_ref[0, :, pl.ds(ks, kb)]
        if masked:
            qi = lax.broadcasted_iota(jnp.int32, (kb, rows), 1) % tq
            qend = ((t * tq + qi) // CHUNK + 1) * CHUNK
            kpos = ks + lax.broadcasted_iota(jnp.int32, (kb, rows), 0)
            bias = jnp.where(kpos < qend, 0.0, NEG)
        for c in range(2):
            s = s_scr[slot, c]
            if masked:
                s = s + bias
            m = m_scr[c]
            m_new = jnp.maximum(m, jnp.max(s, axis=0, keepdims=True))
            alpha = jnp.exp2(m - m_new)
            p = jnp.exp2(s - m_new)
            l_scr[c] = alpha * l_scr[c] + jnp.sum(p, axis=0, keepdims=True)
            m_scr[c] = m_new
            acc_scr[c] = alpha * acc_scr[c] + jnp.dot(vtb, p.astype(BF16), preferred_element_type=F32)

    scores(0, 0)

    def pair(j, carry):
        i = 2 * j
        scores(i + 1, 1)
        accumulate(i, 0, False)
        scores(i + 2, 0)
        accumulate(i + 1, 1, False)
        return carry

    lax.fori_loop(0, n_free // 2, pair, 0)
    i0 = (n_free // 2) * 2

    @pl.when(n_free % 2 == 1)
    def _():
        scores(i0 + 1, 1)
        accumulate(i0, 0, False)
        accumulate(i0 + 1, 1, True)

    @pl.when(n_free % 2 == 0)
    def _():
        accumulate(i0, 0, True)

    o = acc_scr[0] / l_scr[0] - lam * (acc_scr[1] / l_scr[1])
    y = o * lax.rsqrt(jnp.mean(o * o, axis=0, keepdims=True) + EPS) * (1.0 - lam_init)
    gain = g_ref[...]
    for g in range(group):
        o_ref[g * e:(g + 1) * e, :] = (y[:, g * tq:(g + 1) * tq] * gain).astype(o_ref.dtype)


def _lam_specs(n_grid):
    zero = (lambda *_: (0, 0))
    return [pl.BlockSpec((1, LANES), zero)] * 4


def _diff_prompt(q, k, vt, lams, subln_g, *, batch, seq, kvh, group, dh, lam_init, tq=LANES, kb=512):
    kb = _pick(seq, kb, LANES)
    nt = seq // tq
    rows = group * tq
    e = 2 * dh
    gain = jnp.broadcast_to(subln_g.reshape(e, 1), (e, tq))
    return pl.pallas_call(
        functools.partial(_diff_prompt_kernel, tq=tq, kb=kb, group=group, dh=dh, lam_init=lam_init),
        grid=(batch, kvh, nt),
        in_specs=[pl.BlockSpec((tq, group * e), lambda b, h, t: (b * nt + t, h)),
                  pl.BlockSpec((seq, e), lambda b, h, t: (b, h)),
                  pl.BlockSpec((1, e, seq), lambda b, h, t: (b, h, 0))]
                 + _lam_specs(3) + [pl.BlockSpec((e, tq), lambda b, h, t: (0, 0))],
        out_specs=pl.BlockSpec((group * e, tq), lambda b, h, t: (h, b * nt + t)),
        out_shape=jax.ShapeDtypeStruct((kvh * group * e, batch * seq), BF16),
        scratch_shapes=[pltpu.VMEM((2, e, rows), F32), pltpu.VMEM((2, 2, kb, rows), F32),
                        pltpu.VMEM((2, 1, rows), F32), pltpu.VMEM((2, 1, rows), F32)],
        compiler_params=_cparams(3),
        name="diff_prompt",
    )(q, k, vt, *lams, gain)


def _diff_sample_kernel(q_ref, kc_ref, vc_ref, kn_ref, vn_ref, lq1_ref, lk1_ref, lq2_ref, lk2_ref, g_ref, o_ref,
                        *, ts, group, dh, lam_init):
    lam = _diff_lambda(lq1_ref, lk1_ref, lq2_ref, lk2_ref, lam_init)
    kc = kc_ref[0].astype(BF16)
    vc = vc_ref[0].astype(BF16)
    kn = kn_ref[...]
    a_c, a_n = [], []
    for c in range(2):
        qc = _stack_groups(q_ref, c, group, dh)
        sc = lax.dot_general(qc, kc[:, c * dh:(c + 1) * dh], _DN_T, preferred_element_type=F32)
        sn = lax.dot_general(qc, kn[:, c * dh:(c + 1) * dh], _DN_T, preferred_element_type=F32)
        m = jnp.maximum(jnp.max(sc, axis=1, keepdims=True), jnp.max(sn, axis=1, keepdims=True))
        pc = jnp.exp2(sc - m)
        pn = jnp.exp2(sn - m)
        l = jnp.sum(pc, axis=1, keepdims=True) + jnp.sum(pn, axis=1, keepdims=True)
        a_c.append(pc / l)
        a_n.append(pn / l)
    ac = (a_c[0] - lam * a_c[1]).astype(BF16)
    an = (a_n[0] - lam * a_n[1]).astype(BF16)
    o = (jnp.dot(ac, vc, preferred_element_type=F32) + jnp.dot(an, vn_ref[...], preferred_element_type=F32))
    _diff_finish(o, g_ref, o_ref, group=group, tq=ts, out_scale=1.0 - lam_init)


def _diff_sample(q, k, v, cache_k, cache_v, lams, subln_g, *, batch, ts, kvh, group, dh, lam_init, row0):
    past = cache_k.shape[1]
    rb0 = row0 // ts
    e = 2 * dh
    return pl.pallas_call(
        functools.partial(_diff_sample_kernel, ts=ts, group=group, dh=dh, lam_init=lam_init),
        grid=(batch, kvh),
        in_specs=[pl.BlockSpec((ts, group * e), lambda b, h: (rb0 + b, h)),
                  pl.BlockSpec((1, past, e), lambda b, h: (b, 0, h)),
                  pl.BlockSpec((1, past, e), lambda b, h: (b, 0, h)),
                  pl.BlockSpec((ts, e), lambda b, h: (rb0 + b, h)),
                  pl.BlockSpec((ts, e), lambda b, h: (rb0 + b, h))]
                 + _lam_specs(2) + [pl.BlockSpec((1, e), lambda b, h: (0, 0))],
        out_specs=pl.BlockSpec((ts, group * e), lambda b, h: (b, h)),
        out_shape=jax.ShapeDtypeStruct((batch * ts, kvh * group * e), BF16),
        compiler_params=_cparams(2),
        name="diff_sample",
    )(q, cache_k, cache_v, k, v, *lams, subln_g)


def _dsa_kernel(iq_ref, ik_ref, iwt_ref, q_ref, k_ref, vt_ref, o_ref, key_scr, acc_scr,
                *, tq, kb, causal, s_real, n_sel, n_idx, kvh, group, dh, iscale):
    t = pl.program_id(1)
    s_pad = ik_ref.shape[0]
    lane = lax.broadcasted_iota(jnp.int32, (1, tq), 1)
    if causal:
        qend = ((t * tq + lane) // CHUNK + 1) * CHUNK
        n_blocks = ((t + 1) * tq + kb - 1) // kb
    else:
        qend = jnp.full((1, tq), s_real, jnp.int32)
        n_blocks = s_pad // kb

    w = iwt_ref[...] * iscale

    def score_body(i, carry):
        ks = pl.multiple_of(i * kb, kb)
        ikb = ik_ref[pl.ds(ks, kb), :]
        sc = jnp.zeros((kb, tq), F32)
        for n in range(n_idx):
            logit = lax.dot_general(ikb, iq_ref[:, n * dh:(n + 1) * dh], _DN_T, preferred_element_type=F32)
            sc = sc + w[n:n + 1, :] * jnp.maximum(logit, 0.0)
        bits = lax.bitcast_convert_type(sc, jnp.int32)
        key = jnp.where(bits < 0, bits ^ 0x7FFFFFFF, bits)
        kpos = ks + lax.broadcasted_iota(jnp.int32, (kb, tq), 0)
        key_scr[pl.ds(ks, kb), :] = jnp.where(kpos < qend, key, INT_MIN)
        return carry

    lax.fori_loop(0, n_blocks, score_body, 0)

    def count_ge(cand):
        def body(i, acc):
            ks = pl.multiple_of(i * kb, kb)
            ge = key_scr[pl.ds(ks, kb), :] >= cand
            return acc + jnp.sum(ge.astype(jnp.int32), axis=0, keepdims=True)
        return lax.fori_loop(0, n_blocks, body, jnp.zeros((1, tq), jnp.int32))

    zero = jnp.zeros((1, tq), jnp.int32)
    thr = jnp.where(count_ge(zero) >= n_sel, zero, jnp.full((1, tq), INT_MIN, jnp.int32))

    def bit_body(i, thr):
        cand = thr | lax.shift_left(jnp.int32(1), 30 - i)
        return jnp.where(count_ge(cand) >= n_sel, cand, thr)

    thr = lax.fori_loop(0, 31, bit_body, thr)
    thr = jnp.maximum(thr, INT_MIN + 1)

    def bias_body(i, carry):
        ks = pl.multiple_of(i * kb, kb)
        bias = jnp.where(key_scr[pl.ds(ks, kb), :] >= thr, 0.0, NEG)
        key_scr[pl.ds(ks, kb), :] = lax.bitcast_convert_type(bias, jnp.int32)
        return carry

    lax.fori_loop(0, n_blocks, bias_body, 0)

    cols = group * tq
    for h in range(kvh):
        acc_scr[...] = jnp.zeros(acc_scr.shape, F32)

        def kv_body(i, st, h=h):
            m, l = st
            ks = pl.multiple_of(i * kb, kb)
            kblk = k_ref[pl.ds(ks, kb), h * dh:(h + 1) * dh]
            qh = jnp.concatenate([q_ref[:, (h * group + g) * dh:(h * group + g + 1) * dh] for g in range(group)],
                                 axis=0)
            bias = lax.bitcast_convert_type(key_scr[pl.ds(ks, kb), :], F32)
            s = lax.dot_general(kblk, qh, _DN_T, preferred_element_type=F32)
            s = s + jnp.concatenate([bias] * group, axis=1)
            m_new = jnp.maximum(m, jnp.max(s, axis=0, keepdims=True))
            alpha = jnp.exp2(m - m_new)
            p = jnp.exp2(s - m_new)
            l = alpha * l + jnp.sum(p, axis=0, keepdims=True)
            vtb = vt_ref[0, h * dh:(h + 1) * dh, pl.ds(ks, kb)]
            acc_scr[...] = alpha * acc_scr[...] + jnp.dot(vtb, p.astype(BF16), preferred_element_type=F32)
            return m_new, l

        _, l = lax.fori_loop(0, n_blocks, kv_body, (jnp.full((1, cols), NEG, F32), jnp.zeros((1, cols), F32)))
        out = (acc_scr[...] / l).astype(o_ref.dtype)
        for g in range(group):
            o_ref[(h * group + g) * dh:(h * group + g + 1) * dh, :] = out[:, g * tq:(g + 1) * tq]


def _dsa(iq, ik, iwt, q, k, vt, *, batch, tq_total, tq, kb, causal, s_real, n_sel):
    dh = ik.shape[1]
    s_pad = vt.shape[2]
    kvh = k.shape[1] // dh
    heads = q.shape[1] // dh
    n_idx = iq.shape[1] // dh
    nt = tq_total // tq
    return pl.pallas_call(
        functools.partial(_dsa_kernel, tq=tq, kb=kb, causal=causal, s_real=s_real, n_sel=n_sel, n_idx=n_idx,
                          kvh=kvh, group=heads // kvh, dh=dh, iscale=(n_idx * dh) ** -0.5),
        grid=(batch, nt),
        in_specs=[pl.BlockSpec((tq, n_idx * dh), lambda b, t: (b * nt + t, 0)),
                  pl.BlockSpec((s_pad, dh), lambda b, t: (b, 0)),
                  pl.BlockSpec((n_idx, tq), lambda b, t: (0, b * nt + t)),
                  pl.BlockSpec((tq, heads * dh), lambda b, t: (b * nt + t, 0)),
                  pl.BlockSpec((s_pad, kvh * dh), lambda b, t: (b, 0)),
                  pl.BlockSpec((1, kvh * dh, s_pad), lambda b, t: (b, 0, 0))],
        out_specs=pl.BlockSpec((heads * dh, tq), lambda b, t: (0, b * nt + t)),
        out_shape=jax.ShapeDtypeStruct((heads * dh, batch * tq_total), BF16),
        scratch_shapes=[pltpu.VMEM((s_pad, tq), jnp.int32), pltpu.VMEM((dh, heads // kvh * tq), F32)],
        compiler_params=_cparams(2),
        name="dsa",
    )(iq, ik, iwt, q, k, vt)


def _mixer_a(xn, tabs, w_in, lams, subln_g, cache_k, cache_v, lam_init, dims):
    bp, tp, bs, ts, d, past = dims
    mp = bp * tp
    dh = d // (2 * A_HEADS)
    a_q = A_HEADS * 2 * dh
    a_kv = A_KV_HEADS * 2 * dh
    group = A_HEADS // A_KV_HEADS
    (q,) = _proj(xn, w_in[:, :a_q].astype(BF16), (BF16,), rope=tabs, out_scale=dh ** -0.5 * LOG2E)
    k32, k16 = _proj(xn, w_in[:, a_q:a_q + a_kv].astype(BF16), (F32, BF16), rope=tabs)
    v32, v16 = _proj(xn, w_in[:, a_q + a_kv:].astype(BF16), (F32, BF16))
    lams = tuple(x.reshape(1, dh).astype(F32) for x in lams)
    subln_g = subln_g.astype(F32)
    vt_p = jnp.swapaxes(v16[:mp].reshape(bp, tp, a_kv), 1, 2)
    o_p = _diff_prompt(q, k16, vt_p, lams, subln_g, batch=bp, seq=tp, kvh=A_KV_HEADS, group=group, dh=dh,
                       lam_init=lam_init).T
    o_s = _diff_sample(q, k16, v16, cache_k.reshape(bs, past, a_kv), cache_v.reshape(bs, past, a_kv), lams,
                       subln_g.reshape(1, 2 * dh), batch=bs, ts=ts, kvh=A_KV_HEADS, group=group, dh=dh,
                       lam_init=lam_init, row0=mp)
    return jnp.concatenate([o_p, o_s], axis=0), k32, v32


def _mixer_b(xn, tabs, w_in, cache_k, cache_v, cache_ik, dims):
    bp, tp, bs, ts, d, past = dims
    mp = bp * tp
    dh = d // B_HEADS
    o1 = B_HEADS * dh
    o2 = o1 + B_KV_HEADS * dh
    o3 = o2 + B_KV_HEADS * dh
    o4 = o3 + B_IDX_HEADS * B_IDX_DIM
    o5 = o4 + B_IDX_DIM
    kvw = B_KV_HEADS * dh
    (q,) = _proj(xn, w_in[:, :o1].astype(BF16), (BF16,), rope=tabs, out_scale=dh ** -0.5 * LOG2E)
    k32, k16 = _proj(xn, w_in[:, o1:o2].astype(BF16), (F32, BF16), rope=tabs)
    v32, v16 = _proj(xn, w_in[:, o2:o3].astype(BF16), (F32, BF16))
    (iq,) = _proj(xn, w_in[:, o3:o4].astype(BF16), (BF16,), rope=tabs)
    ik32, ik16 = _proj(xn, w_in[:, o4:o5].astype(BF16), (F32, BF16), rope=tabs)
    w_iw = jnp.pad(w_in[:, o5:], ((0, 0), (0, LANES - B_IDX_HEADS))).astype(BF16)
    (iw,) = _proj(xn, w_iw, (F32,))
    iwt = iw[:, :B_IDX_HEADS].T

    vt_p = jnp.swapaxes(v16[:mp].reshape(bp, tp, kvw), 1, 2)
    tq_p = _pick(tp, 256, LANES)
    o_p = _dsa(iq, ik16, iwt, q, k16, vt_p, batch=bp, tq_total=tp, tq=tq_p, kb=_pick(tp, 512, LANES),
               causal=True, s_real=tp, n_sel=min(B_TOPK_MAX, tp // 4)).T

    past = cache_k.shape[1]
    s_real = past + ts
    s_pad = -(-s_real // LANES) * LANES
    tq_s = -(-ts // LANES) * LANES

    def keys(cache, new):
        new = new[mp:].reshape(bs, ts, -1)
        kk = jnp.concatenate([cache.reshape(bs, past, -1).astype(BF16), new,
                              jnp.zeros((bs, s_pad - s_real, new.shape[-1]), BF16)], axis=1)
        return kk

    def queries(a):
        a = jnp.pad(a[mp:].reshape(bs, ts, -1), ((0, 0), (0, tq_s - ts), (0, 0)))
        return a.reshape(bs * tq_s, -1)

    kk = keys(cache_k, k16).reshape(bs * s_pad, kvw)
    ikk = keys(cache_ik, ik16).reshape(bs * s_pad, B_IDX_DIM)
    vt_s = jnp.swapaxes(keys(cache_v, v16), 1, 2)
    o_s = _dsa(queries(iq), ikk, queries(iw[:, :B_IDX_HEADS]).T, queries(q), kk, vt_s, batch=bs, tq_total=tq_s,
               tq=tq_s, kb=_pick(s_pad, 512, LANES), causal=False, s_real=s_real,
               n_sel=min(B_TOPK_MAX, s_real // 4))
    o_s = o_s.T.reshape(bs, tq_s, o1)[:, :ts].reshape(bs * ts, o1)
    return jnp.concatenate([o_p, o_s], axis=0), k32, v32, ik32


def _mixer_c(xn, w_in, rel_bias, cache_k, cache_v, dims):
    bp, tp, bs, ts, d, past = dims
    mp = bp * tp
    dh = d // C_HEADS
    hw = C_HEADS * dh
    win = C_LEFT_CHUNKS * CHUNK
    (q,) = _proj(xn, w_in[:, :hw].astype(BF16), (BF16,))
    kv32, kv16 = _proj(xn, w_in[:, hw:].astype(BF16), (F32, BF16))

    chunk_bias = _band_bias(rel_bias, win, CHUNK, 0, win + CHUNK)
    bias_tile = jnp.full((C_HEADS, win, 2 * win), NEG, F32)
    for cc in range(win // CHUNK):
        bias_tile = lax.dynamic_update_slice(bias_tile, chunk_bias, (0, cc * CHUNK, cc * CHUNK))
    o_p = _band_prompt(q, kv16, bias_tile, batch=bp, seq=tp, heads=C_HEADS, dh=dh, row0=0)

    w_c = cache_k.shape[1]
    o_s = _band_sample(q, kv16, cache_k.reshape(bs, w_c, hw), cache_v.reshape(bs, w_c, hw),
                       _band_bias(rel_bias, past, ts, past - w_c, w_c + ts),
                       batch=bs, ts=ts, heads=C_HEADS, dh=dh, row0=mp)
    return jnp.concatenate([o_p, o_s], axis=0), kv32[:, :hw], kv32[:, hw:]


def kernel(x_prompt, x_sample, cache_a_k, cache_a_v, cache_b_k, cache_b_v, cache_b_idx_k, cache_c_k, cache_c_v,
           norm_mix_g, norm_ffn_g, norm_out_g, a_w_in, a_w_out, a_lam_q1, a_lam_k1, a_lam_q2, a_lam_k2,
           a_subln_g, b_w_in, b_w_out, c_w_in, c_w_out, c_rel_bias, ffn_w_gu, ffn_w_down):
    bp, tp, d = x_prompt.shape
    bs, ts, _ = x_sample.shape
    past = cache_a_k.shape[2]
    depth = norm_mix_g.shape[0]
    assert ts <= CHUNK and past % CHUNK == 0 and tp % CHUNK == 0
    mp, ms = bp * tp, bs * ts
    dims = (bp, tp, bs, ts, d, past)
    i32 = jnp.int32

    h = jnp.concatenate([x_prompt.reshape(mp, d), x_sample.reshape(ms, d)], axis=0)
    pos = jnp.concatenate([jnp.tile(jnp.arange(tp, dtype=i32), bp), jnp.tile(past + jnp.arange(ts, dtype=i32), bs)])
    tabs = _rope_tables(pos)
    w_gu16 = ffn_w_gu.astype(BF16)
    w_down16 = ffn_w_down.astype(BF16)

    def split(a, tail):
        return a[:mp].reshape((bp, tp) + tail), a[mp:].reshape((bs, ts) + tail)

    st = {name: [] for name in ("a_k", "a_v", "b_k", "b_v", "b_ik", "c_k", "c_v")}
    for i in range(depth):
        j = i // N_MIXERS
        kind = i % N_MIXERS
        xn = _rmsnorm(h, norm_mix_g[i], BF16)
        if kind == 0:
            lam_init = 0.8 - 0.6 * math.exp(-0.3 * i)
            o, k, v = _mixer_a(xn, tabs, a_w_in[j], (a_lam_q1[j], a_lam_k1[j], a_lam_q2[j], a_lam_k2[j]),
                               a_subln_g[j], cache_a_k[j], cache_a_v[j], lam_init, dims)
            tail = cache_a_k.shape[3:]
            st["a_k"].append(split(k, tail))
            st["a_v"].append(split(v, tail))
            w_out = a_w_out[j]
        elif kind == 1:
            o, k, v, ik = _mixer_b(xn, tabs, b_w_in[j], cache_b_k[j], cache_b_v[j], cache_b_idx_k[j], dims)
            tail = cache_b_k.shape[3:]
            st["b_k"].append(split(k, tail))
            st["b_v"].append(split(v, tail))
            st["b_ik"].append(split(ik, cache_b_idx_k.shape[3:]))
            w_out = b_w_out[j]
        else:
            o, k, v = _mixer_c(xn, c_w_in[j], c_rel_bias[j], cache_c_k[j], cache_c_v[j], dims)
            tail = cache_c_k.shape[3:]
            new = []
            for a, cache in ((k, cache_c_k[j]), (v, cache_c_v[j])):
                a_p, a_s = split(a, tail)
                win = C_LEFT_CHUNKS * CHUNK
                kk_p = jnp.concatenate([jnp.zeros((bp, win) + tail, a.dtype), a_p], axis=1)
                kk_s = jnp.concatenate([cache, a_s], axis=1)
                keep_p, keep_s = min(win, tp), cache.shape[1]
                new.append((kk_p[:, kk_p.shape[1] - keep_p:], kk_s[:, kk_s.shape[1] - keep_s:]))
            st["c_k"].append(new[0])
            st["c_v"].append(new[1])
            w_out = c_w_out[j]
        (h,) = _proj(o, w_out.astype(BF16), (F32,), residual=h)
        h = _ffn(h, norm_ffn_g[i], w_gu16, w_down16, i)

    y = _rmsnorm(h, norm_out_g, F32)
    y_p, y_s = split(y, (d,))
    stack = lambda name, g: jnp.stack([pair[g] for pair in st[name]], axis=0)
    names = ("a_k", "a_v", "b_k", "b_v", "b_ik", "c_k", "c_v")
    return (y_p, y_s) + tuple(stack(n, 0) for n in names) + tuple(stack(n, 1) for n in names)
```

```python
import functools
import math

import jax
import jax.numpy as jnp
import numpy as np
from jax import lax
from jax.experimental import pallas as pl
from jax.experimental.pallas import tpu as pltpu

F32 = jnp.float32
BF16 = jnp.bfloat16

CHUNK = 64
N_MIXERS = 3
EPS = 1e-6
ROPE_THETA = 500000.0
NEG = -1e30
A_HEADS = 16
A_KV_HEADS = 4
B_HEADS = 32
B_KV_HEADS = 4
B_IDX_HEADS = 32
B_IDX_DIM = 128
B_TOPK_MAX = 256
C_HEADS = 32
C_LEFT_CHUNKS = 8
C_REL_CLIP = 256

LANES = 128
VMEM_LIMIT = 56 * 1024 * 1024
INT_MIN = -2 ** 31
LOG2E = math.log2(math.e)


def _cparams(n_grid):
    return pltpu.CompilerParams(dimension_semantics=("arbitrary",) * n_grid,
                                vmem_limit_bytes=VMEM_LIMIT)


def _pick(n, target, mult=16):
    for t in range(min(n, target), 0, -1):
        if n % t == 0 and t % mult == 0:
            return t
    return n


def _rmsnorm_kernel(x_ref, g_ref, o_ref):
    x = x_ref[...]
    y = x * lax.rsqrt(jnp.mean(x * x, axis=-1, keepdims=True) + EPS)
    o_ref[...] = (y * g_ref[...]).astype(o_ref.dtype)


def _rmsnorm(x, g, out_dtype, row0=0, rows=None):
    d = x.shape[1]
    m = x.shape[0] if rows is None else rows
    tm = _pick(math.gcd(m, row0) if row0 else m, 512)
    b0 = row0 // tm
    return pl.pallas_call(
        _rmsnorm_kernel,
        grid=(m // tm,),
        in_specs=[pl.BlockSpec((tm, d), lambda i: (b0 + i, 0)),
                  pl.BlockSpec((1, d), lambda i: (0, 0))],
        out_specs=pl.BlockSpec((tm, d), lambda i: (i, 0)),
        out_shape=jax.ShapeDtypeStruct((m, d), out_dtype),
        compiler_params=_cparams(1),
        name="rmsnorm",
    )(x, g.reshape(1, d).astype(F32))


def _rope_tables(pos, dh=LANES):
    r = dh // 4
    half = r // 2
    inv = ROPE_THETA ** (-2.0 * jnp.arange(half, dtype=F32) / r)
    ang = pos.astype(F32)[:, None] * inv[None, :]
    cos, sin = jnp.cos(ang), jnp.sin(ang)
    rows = pos.shape[0]
    zeros = jnp.zeros((rows, dh - r), F32)
    zh = jnp.zeros((rows, half), F32)
    c = jnp.concatenate([cos, cos, jnp.ones((rows, dh - r), F32)], axis=1)
    sa = jnp.concatenate([zh, sin, zeros], axis=1)
    sb = jnp.concatenate([-sin, zh, zeros], axis=1)
    return c, sa, sb


def _rope_slab(x, c, sa, sb):
    return x * c + pltpu.roll(x, 16, 1) * sa + pltpu.roll(x, LANES - 16, 1) * sb


def _proj_kernel(*refs, rope, residual, n_out, out_scale):
    x_ref, w_ref = refs[0], refs[1]
    pos = 2
    if rope:
        c_ref, sa_ref, sb_ref = refs[pos:pos + 3]
        pos += 3
    if residual:
        r_ref = refs[pos]
        pos += 1
    outs = refs[pos:pos + n_out]
    acc = jnp.dot(x_ref[...], w_ref[...], preferred_element_type=F32)
    if rope:
        c, sa, sb = c_ref[...], sa_ref[...], sb_ref[...]
        tn = acc.shape[1]
        acc = jnp.concatenate(
            [_rope_slab(acc[:, s * LANES:(s + 1) * LANES], c, sa, sb) for s in range(tn // LANES)], axis=1)
    if residual:
        acc = acc + r_ref[...]
    if out_scale != 1.0:
        acc = acc * out_scale
    for o in outs:
        o[...] = acc.astype(o.dtype)


def _proj(x, w, out_dtypes, rope=None, residual=None, out_scale=1.0, tm_target=1024, tn_target=512):
    m, k = x.shape
    n = w.shape[1]
    tm = _pick(m, tm_target)
    tn = _pick(n, tn_target, LANES)
    in_specs = [pl.BlockSpec((tm, k), lambda i, j: (i, 0)),
                pl.BlockSpec((k, tn), lambda i, j: (0, j))]
    args = [x, w]
    if rope is not None:
        in_specs += [pl.BlockSpec((tm, LANES), lambda i, j: (i, 0))] * 3
        args += list(rope)
    if residual is not None:
        in_specs.append(pl.BlockSpec((tm, tn), lambda i, j: (i, j)))
        args.append(residual)
    outs = pl.pallas_call(
        functools.partial(_proj_kernel, rope=rope is not None, residual=residual is not None,
                          n_out=len(out_dtypes), out_scale=out_scale),
        grid=(m // tm, n // tn),
        in_specs=in_specs,
        out_specs=[pl.BlockSpec((tm, tn), lambda i, j: (i, j)) for _ in out_dtypes],
        out_shape=[jax.ShapeDtypeStruct((m, n), dt) for dt in out_dtypes],
        compiler_params=_cparams(2),
        name="proj",
    )(*args)
    return outs


def _out_proj_kernel(xp_ref, xs_ref, w_ref, r_ref, o_ref, *, n_p):
    i = pl.program_id(0)

    @pl.when(i < n_p)
    def _():
        o_ref[...] = r_ref[...] + jnp.dot(xp_ref[...], w_ref[...], preferred_element_type=F32)

    @pl.when(i >= n_p)
    def _():
        o_ref[...] = r_ref[...] + jnp.dot(xs_ref[...], w_ref[...], preferred_element_type=F32)


def _out_proj(x_p, x_s, w, residual, tm_target=1024, tn_target=512):
    (m_p, k), m_s = x_p.shape, x_s.shape[0]
    n = w.shape[1]
    tm = _pick(math.gcd(m_p, m_s), tm_target)
    tn = _pick(n, tn_target, LANES)
    n_p = m_p // tm
    return pl.pallas_call(
        functools.partial(_out_proj_kernel, n_p=n_p),
        grid=((m_p + m_s) // tm, n // tn),
        in_specs=[pl.BlockSpec((tm, k), lambda i, j: (jnp.minimum(i, n_p - 1), 0)),
                  pl.BlockSpec((tm, k), lambda i, j: (jnp.maximum(i - n_p, 0), 0)),
                  pl.BlockSpec((k, tn), lambda i, j: (0, j)),
                  pl.BlockSpec((tm, tn), lambda i, j: (i, j))],
        out_specs=pl.BlockSpec((tm, tn), lambda i, j: (i, j)),
        out_shape=jax.ShapeDtypeStruct((m_p + m_s, n), F32),
        compiler_params=_cparams(2),
        name="out_proj",
    )(x_p, x_s, w, residual)


def _ffn_kernel(h_ref, g_ref, wg_ref, wu_ref, wd_ref, o_ref, xn_ref):
    j = pl.program_id(1)

    @pl.when(j == 0)
    def _():
        x = h_ref[...]
        y = x * lax.rsqrt(jnp.mean(x * x, axis=-1, keepdims=True) + EPS)
        xn_ref[...] = (y * g_ref[...]).astype(BF16)
        o_ref[...] = x

    xn = xn_ref[...]
    gate = jnp.dot(xn, wg_ref[...], preferred_element_type=F32)
    up = jnp.dot(xn, wu_ref[...], preferred_element_type=F32)
    act = (gate * (1.0 / (1.0 + jnp.exp(-gate))) * up).astype(BF16)
    o_ref[...] += jnp.dot(act, wd_ref[...], preferred_element_type=F32)


def _ffn(h, g, w_gu, w_down, layer, tm_target=512, tf=256):
    m, d = h.shape
    f = w_down.shape[1]
    tm = _pick(m, tm_target)
    nf = f // tf
    return pl.pallas_call(
        _ffn_kernel,
        grid=(m // tm, nf),
        in_specs=[pl.BlockSpec((tm, d), lambda i, j: (i, 0), pipeline_mode=pl.Buffered(1)),
                  pl.BlockSpec((1, d), lambda i, j: (0, 0)),
                  pl.BlockSpec((None, d, tf), lambda i, j: (layer, 0, j)),
                  pl.BlockSpec((None, d, tf), lambda i, j: (layer, 0, j + nf)),
                  pl.BlockSpec((None, tf, d), lambda i, j: (layer, j, 0))],
        out_specs=pl.BlockSpec((tm, d), lambda i, j: (i, 0)),
        out_shape=jax.ShapeDtypeStruct((m, d), F32),
        scratch_shapes=[pltpu.VMEM((tm, d), BF16)],
        compiler_params=_cparams(2),
        name="ffn",
    )(h, g.reshape(1, d).astype(F32), w_gu, w_gu, w_down)


_DN_T = (((1,), (1,)), ((), ()))


BAND_HEADS_PER_STEP = 4


def _band_prompt_kernel(q_ref, kp_ref, kc_ref, vp_ref, vc_ref, bias_ref, o_ref, bias_scr, *, tq, dh):
    t = pl.program_id(2)
    hp = bias_ref.shape[0]

    @pl.when((pl.program_id(1) == 0) & (t == 0))
    def _():
        bias_scr[...] = jnp.full(bias_scr.shape, NEG, F32)
        width = bias_ref.shape[3]
        for u in range(hp):
            for cc in range(tq // CHUNK):
                c0 = (cc // 2) * LANES
                bias_scr[u, cc * CHUNK:(cc + 1) * CHUNK, c0:c0 + width] = bias_ref[u, cc % 2]

    col = lax.broadcasted_iota(jnp.int32, (1, 2 * tq), 1)
    before_start = jnp.where((col < tq) & (t == 0), NEG, 0.0)
    for u in range(hp):
        hs = slice(u * dh, (u + 1) * dh)
        k = jnp.concatenate([kp_ref[:, hs], kc_ref[:, hs]], axis=0)
        v = jnp.concatenate([vp_ref[:, hs], vc_ref[:, hs]], axis=0)
        s = lax.dot_general(q_ref[:, hs], k, _DN_T, preferred_element_type=F32) + bias_scr[u] + before_start
        m = jnp.max(s, axis=1, keepdims=True)
        p = jnp.exp2(s - m)
        l = jnp.sum(p, axis=1, keepdims=True)
        o = jnp.dot(p.astype(BF16), v, preferred_element_type=F32) / l
        o_ref[:, hs] = o.astype(o_ref.dtype)


def _band_prompt(q, kv, chunk_bias, *, batch, seq, heads, dh, row0):
    assert 2 * CHUNK == LANES
    hp = BAND_HEADS_PER_STEP
    tq = chunk_bias.shape[2] - CHUNK
    nt = seq // tq
    rb0 = row0 // tq
    nh = heads // hp
    pad = jnp.full((heads, CHUNK, CHUNK), NEG, F32)
    bias2 = jnp.stack([jnp.concatenate([chunk_bias, pad], axis=2), jnp.concatenate([pad, chunk_bias], axis=2)],
                      axis=1)
    cur = lambda h, b, t: (rb0 + b * nt + t, h)
    prev = lambda h, b, t: (rb0 + b * nt + jnp.maximum(t - 1, 0), h)
    cur_v = lambda h, b, t: (rb0 + b * nt + t, nh + h)
    prev_v = lambda h, b, t: (rb0 + b * nt + jnp.maximum(t - 1, 0), nh + h)
    blk = (tq, hp * dh)
    return pl.pallas_call(
        functools.partial(_band_prompt_kernel, tq=tq, dh=dh),
        grid=(nh, batch, nt),
        in_specs=[pl.BlockSpec(blk, cur),
                  pl.BlockSpec(blk, prev), pl.BlockSpec(blk, cur),
                  pl.BlockSpec(blk, prev_v), pl.BlockSpec(blk, cur_v),
                  pl.BlockSpec((hp, 2, CHUNK, tq + 2 * CHUNK), lambda h, b, t: (h, 0, 0, 0))],
        out_specs=pl.BlockSpec(blk, lambda h, b, t: (b * nt + t, h)),
        out_shape=jax.ShapeDtypeStruct((batch * seq, heads * dh), BF16),
        scratch_shapes=[pltpu.VMEM((hp, tq, 2 * tq), F32)],
        compiler_params=_cparams(3),
        name="band_prompt",
    )(q, kv, kv, kv, kv, bias2)


def _band_sample_kernel(q_ref, kc_ref, vc_ref, kn_ref, vn_ref, bias_ref, o_ref, *, w, dh):
    for u in range(bias_ref.shape[0]):
        hs = slice(u * dh, (u + 1) * dh)
        q = q_ref[:, hs]
        kc = kc_ref[0, :, hs].astype(BF16)
        vc = vc_ref[0, :, hs].astype(BF16)
        bias = bias_ref[u]
        sc = lax.dot_general(q, kc, _DN_T, preferred_element_type=F32) + bias[:, :w]
        sn = lax.dot_general(q, kn_ref[:, hs], _DN_T, preferred_element_type=F32) + bias[:, w:]
        m = jnp.maximum(jnp.max(sc, axis=1, keepdims=True), jnp.max(sn, axis=1, keepdims=True))
        pc = jnp.exp2(sc - m)
        pn = jnp.exp2(sn - m)
        l = jnp.sum(pc, axis=1, keepdims=True) + jnp.sum(pn, axis=1, keepdims=True)
        o = (jnp.dot(pc.astype(BF16), vc, preferred_element_type=F32)
             + jnp.dot(pn.astype(BF16), vn_ref[:, hs], preferred_element_type=F32)) / l
        o_ref[:, hs] = o.astype(o_ref.dtype)


def _band_sample(q, kv, cache_k, cache_v, bias, *, batch, ts, heads, dh, row0):
    hp = BAND_HEADS_PER_STEP
    w = cache_k.shape[1]
    rb0 = row0 // ts
    nh = heads // hp
    return pl.pallas_call(
        functools.partial(_band_sample_kernel, w=w, dh=dh),
        grid=(nh, batch),
        in_specs=[pl.BlockSpec((ts, hp * dh), lambda h, b: (rb0 + b, h)),
                  pl.BlockSpec((1, w, hp * dh), lambda h, b: (b, 0, h)),
                  pl.BlockSpec((1, w, hp * dh), lambda h, b: (b, 0, h)),
                  pl.BlockSpec((ts, hp * dh), lambda h, b: (rb0 + b, h)),
                  pl.BlockSpec((ts, hp * dh), lambda h, b: (rb0 + b, nh + h)),
                  pl.BlockSpec((hp, ts, w + ts), lambda h, b: (h, 0, 0))],
        out_specs=pl.BlockSpec((ts, hp * dh), lambda h, b: (b, h)),
        out_shape=jax.ShapeDtypeStruct((batch * ts, heads * dh), BF16),
        compiler_params=_cparams(2),
        name="band_sample",
    )(q, cache_k, cache_v, kv, kv, bias)


def _band_bias(rel_bias, q0, nq, k0, nk):
    diag = np.arange(nq + nk - 1) - (nq - 1)
    idx = np.clip(q0 - k0 - diag, -C_REL_CLIP, C_REL_CLIP) + C_REL_CLIP
    r = rel_bias[idx].T.astype(F32)
    bias = jnp.stack([r[:, nq - 1 - i:nq - 1 - i + nk] for i in range(nq)], axis=1)
    qpos = q0 + np.arange(nq)[:, None]
    kpos = k0 + np.arange(nk)[None, :]
    qch, kch = qpos // CHUNK, kpos // CHUNK
    ok = (kpos >= 0) & (kch <= qch) & (qch - kch <= C_LEFT_CHUNKS)
    return jnp.where(ok[None], bias, NEG)


def _diff_lambda(lq1_ref, lk1_ref, lq2_ref, lk2_ref, lam_init):
    s1 = jnp.sum(lq1_ref[...] * lk1_ref[...], axis=1, keepdims=True)
    s2 = jnp.sum(lq2_ref[...] * lk2_ref[...], axis=1, keepdims=True)
    return jnp.exp(s1) - jnp.exp(s2) + lam_init


def _stack_groups(q_ref, c, group, dh):
    return jnp.concatenate([q_ref[:, (g * 2 + c) * dh:(g * 2 + c + 1) * dh] for g in range(group)], axis=0)


def _diff_finish(o, g_ref, o_ref, *, group, tq, out_scale):
    y = o * lax.rsqrt(jnp.mean(o * o, axis=-1, keepdims=True) + EPS) * g_ref[...] * out_scale
    y = y.astype(o_ref.dtype)
    e = y.shape[1]
    for g in range(group):
        o_ref[:, g * e:(g + 1) * e] = y[g * tq:(g + 1) * tq]


def _diff_prompt_kernel(q_ref, k_ref, vt_ref, lq1_ref, lk1_ref, lq2_ref, lk2_ref, g_ref, o_ref,
                        acc_scr, s_scr, m_scr, l_scr, *, tq, kb, group, dh, lam_init):
    t = pl.program_id(2)
    rows = group * tq
    e = 2 * dh
    lam = _diff_lambda(lq1_ref, lk1_ref, lq2_ref, lk2_ref, lam_init)
    acc_scr[...] = jnp.zeros(acc_scr.shape, F32)
    m_scr[...] = jnp.full(m_scr.shape, NEG, F32)
    l_scr[...] = jnp.zeros(l_scr.shape, F32)
    n_blocks = ((t + 1) * tq + kb - 1) // kb
    n_free = n_blocks - 1

    def scores(i, slot):
        ks = pl.multiple_of(i * kb, kb)
        kblk = k_ref[pl.ds(ks, kb), :]
        for c in range(2):
            qc = _stack_groups(q_ref, c, group, dh)
            s_scr[slot, c] = lax.dot_general(kblk[:, c * dh:(c + 1) * dh], qc, _DN_T,
                                             preferred_element_type=F32)

    def accumulate(i, slot, masked):
        ks = pl.multiple_of(i * kb, kb)
        vtb = vt_ref[0, :, pl.ds(ks, kb)]
        if masked:
            qi = lax.broadcasted_iota(jnp.int32, (kb, rows), 1) % tq
            qend = ((t * tq + qi) // CHUNK + 1) * CHUNK
            kpos = ks + lax.broadcasted_iota(jnp.int32, (kb, rows), 0)
            bias = jnp.where(kpos < qend, 0.0, NEG)
        for c in range(2):
            s = s_scr[slot, c]
            if masked:
                s = s + bias
            m = m_scr[c]
            m_new = jnp.maximum(m, jnp.max(s, axis=0, keepdims=True))
            alpha = jnp.exp2(m - m_new)
            p = jnp.exp2(s - m_new)
            l_scr[c] = alpha * l_scr[c] + jnp.sum(p, axis=0, keepdims=True)
            m_scr[c] = m_new
            acc_scr[c] = alpha * acc_scr[c] + jnp.dot(vtb, p.astype(BF16), preferred_element_type=F32)

    scores(0, 0)

    def pair(j, carry):
        i = 2 * j
        scores(i + 1, 1)
        accumulate(i, 0, False)
        scores(i + 2, 0)
        accumulate(i + 1, 1, False)
        return carry

    lax.fori_loop(0, n_free // 2, pair, 0)
    i0 = (n_free // 2) * 2

    @pl.when(n_free % 2 == 1)
    def _():
        scores(i0 + 1, 1)
        accumulate(i0, 0, False)
        accumulate(i0 + 1, 1, True)

    @pl.when(n_free % 2 == 0)
    def _():
        accumulate(i0, 0, True)

    o = acc_scr[0] / l_scr[0] - lam * (acc_scr[1] / l_scr[1])
    y = o * lax.rsqrt(jnp.mean(o * o, axis=0, keepdims=True) + EPS) * (1.0 - lam_init)
    for g in range(group):
        o_ref[:, g * e:(g + 1) * e] = (y[:, g * tq:(g + 1) * tq].T * g_ref[...]).astype(o_ref.dtype)


def _lam_specs(n_grid):
    zero = (lambda *_: (0, 0))
    return [pl.BlockSpec((1, LANES), zero)] * 4


def _diff_prompt(q, k, vt, lams, subln_g, *, batch, seq, kvh, group, dh, lam_init, tq=LANES, kb=512):
    kb = _pick(seq, kb, LANES)
    nt = seq // tq
    rows = group * tq
    e = 2 * dh
    return pl.pallas_call(
        functools.partial(_diff_prompt_kernel, tq=tq, kb=kb, group=group, dh=dh, lam_init=lam_init),
        grid=(batch, kvh, nt),
        in_specs=[pl.BlockSpec((tq, group * e), lambda b, h, t: (b * nt + t, h)),
                  pl.BlockSpec((seq, e), lambda b, h, t: (b, h)),
                  pl.BlockSpec((1, e, seq), lambda b, h, t: (b, h, 0))]
                 + _lam_specs(3) + [pl.BlockSpec((1, e), lambda b, h, t: (0, 0))],
        out_specs=pl.BlockSpec((tq, group * e), lambda b, h, t: (b * nt + t, h)),
        out_shape=jax.ShapeDtypeStruct((batch * seq, kvh * group * e), BF16),
        scratch_shapes=[pltpu.VMEM((2, e, rows), F32), pltpu.VMEM((2, 2, kb, rows), F32),
                        pltpu.VMEM((2, 1, rows), F32), pltpu.VMEM((2, 1, rows), F32)],
        compiler_params=_cparams(3),
        name="diff_prompt",
    )(q, k, vt, *lams, subln_g.reshape(1, e))


def _diff_sample_kernel(q_ref, kc_ref, vc_ref, kn_ref, vn_ref, lq1_ref, lk1_ref, lq2_ref, lk2_ref, g_ref, o_ref,
                        *, ts, group, dh, lam_init):
    lam = _diff_lambda(lq1_ref, lk1_ref, lq2_ref, lk2_ref, lam_init)
    kc = kc_ref[0].astype(BF16)
    vc = vc_ref[0].astype(BF16)
    kn = kn_ref[...]
    a_c, a_n = [], []
    for c in range(2):
        qc = _stack_groups(q_ref, c, group, dh)
        sc = lax.dot_general(qc, kc[:, c * dh:(c + 1) * dh], _DN_T, preferred_element_type=F32)
        sn = lax.dot_general(qc, kn[:, c * dh:(c + 1) * dh], _DN_T, preferred_element_type=F32)
        m = jnp.maximum(jnp.max(sc, axis=1, keepdims=True), jnp.max(sn, axis=1, keepdims=True))
        pc = jnp.exp2(sc - m)
        pn = jnp.exp2(sn - m)
        l = jnp.sum(pc, axis=1, keepdims=True) + jnp.sum(pn, axis=1, keepdims=True)
        a_c.append(pc / l)
        a_n.append(pn / l)
    ac = (a_c[0] - lam * a_c[1]).astype(BF16)
    an = (a_n[0] - lam * a_n[1]).astype(BF16)
    o = (jnp.dot(ac, vc, preferred_element_type=F32) + jnp.dot(an, vn_ref[...], preferred_element_type=F32))
    _diff_finish(o, g_ref, o_ref, group=group, tq=ts, out_scale=1.0 - lam_init)


def _diff_sample(q, k, v, cache_k, cache_v, lams, subln_g, *, batch, ts, kvh, group, dh, lam_init, row0):
    past = cache_k.shape[1]
    rb0 = row0 // ts
    e = 2 * dh
    return pl.pallas_call(
        functools.partial(_diff_sample_kernel, ts=ts, group=group, dh=dh, lam_init=lam_init),
        grid=(batch, kvh),
        in_specs=[pl.BlockSpec((ts, group * e), lambda b, h: (rb0 + b, h)),
                  pl.BlockSpec((1, past, e), lambda b, h: (b, 0, h)),
                  pl.BlockSpec((1, past, e), lambda b, h: (b, 0, h)),
                  pl.BlockSpec((ts, e), lambda b, h: (rb0 + b, h)),
                  pl.BlockSpec((ts, e), lambda b, h: (rb0 + b, h))]
                 + _lam_specs(2) + [pl.BlockSpec((1, e), lambda b, h: (0, 0))],
        out_specs=pl.BlockSpec((ts, group * e), lambda b, h: (b, h)),
        out_shape=jax.ShapeDtypeStruct((batch * ts, kvh * group * e), BF16),
        compiler_params=_cparams(2),
        name="diff_sample",
    )(q, cache_k, cache_v, k, v, *lams, subln_g)


def _dsa_kernel(iq_ref, ik_ref, iwt_ref, q_ref, k_ref, vt_ref, o_ref, key_scr, acc_scr,
                *, tq, kb, causal, s_real, n_sel, n_idx, kvh, group, dh, iscale):
    t = pl.program_id(1)
    s_pad = ik_ref.shape[0]
    lane = lax.broadcasted_iota(jnp.int32, (1, tq), 1)
    if causal:
        qend = ((t * tq + lane) // CHUNK + 1) * CHUNK
        n_blocks = ((t + 1) * tq + kb - 1) // kb
    else:
        qend = jnp.full((1, tq), s_real, jnp.int32)
        n_blocks = s_pad // kb

    w = iwt_ref[...] * iscale

    def score_body(i, carry):
        ks = pl.multiple_of(i * kb, kb)
        ikb = ik_ref[pl.ds(ks, kb), :]
        sc = jnp.zeros((kb, tq), F32)
        for n in range(n_idx):
            logit = lax.dot_general(ikb, iq_ref[:, n * dh:(n + 1) * dh], _DN_T, preferred_element_type=F32)
            sc = sc + w[n:n + 1, :] * jnp.maximum(logit, 0.0)
        bits = lax.bitcast_convert_type(sc, jnp.int32)
        key = jnp.where(bits < 0, bits ^ 0x7FFFFFFF, bits)
        kpos = ks + lax.broadcasted_iota(jnp.int32, (kb, tq), 0)
        key_scr[pl.ds(ks, kb), :] = jnp.where(kpos < qend, key, INT_MIN)
        return carry

    lax.fori_loop(0, n_blocks, score_body, 0)

    def count_ge(cand):
        def body(i, acc):
            ks = pl.multiple_of(i * kb, kb)
            ge = key_scr[pl.ds(ks, kb), :] >= cand
            return acc + jnp.sum(ge.astype(jnp.int32), axis=0, keepdims=True)
        return lax.fori_loop(0, n_blocks, body, jnp.zeros((1, tq), jnp.int32))

    zero = jnp.zeros((1, tq), jnp.int32)
    thr = jnp.where(count_ge(zero) >= n_sel, zero, jnp.full((1, tq), INT_MIN, jnp.int32))

    def bit_body(i, thr):
        cand = thr | lax.shift_left(jnp.int32(1), 30 - i)
        return jnp.where(count_ge(cand) >= n_sel, cand, thr)

    thr = lax.fori_loop(0, 31, bit_body, thr)
    thr = jnp.maximum(thr, INT_MIN + 1)

    def bias_body(i, carry):
        ks = pl.multiple_of(i * kb, kb)
        bias = jnp.where(key_scr[pl.ds(ks, kb), :] >= thr, 0.0, NEG)
        key_scr[pl.ds(ks, kb), :] = lax.bitcast_convert_type(bias, jnp.int32)
        return carry

    lax.fori_loop(0, n_blocks, bias_body, 0)

    cols = group * tq
    for h in range(kvh):
        acc_scr[...] = jnp.zeros(acc_scr.shape, F32)

        def kv_body(i, st, h=h):
            m, l = st
            ks = pl.multiple_of(i * kb, kb)
            kblk = k_ref[pl.ds(ks, kb), h * dh:(h + 1) * dh]
            qh = jnp.concatenate([q_ref[:, (h * group + g) * dh:(h * group + g + 1) * dh] for g in range(group)],
                                 axis=0)
            bias = lax.bitcast_convert_type(key_scr[pl.ds(ks, kb), :], F32)
            s = lax.dot_general(kblk, qh, _DN_T, preferred_element_type=F32)
            s = s + jnp.concatenate([bias] * group, axis=1)
            m_new = jnp.maximum(m, jnp.max(s, axis=0, keepdims=True))
            alpha = jnp.exp2(m - m_new)
            p = jnp.exp2(s - m_new)
            l = alpha * l + jnp.sum(p, axis=0, keepdims=True)
            vtb = vt_ref[0, h * dh:(h + 1) * dh, pl.ds(ks, kb)]
            acc_scr[...] = alpha * acc_scr[...] + jnp.dot(vtb, p.astype(BF16), preferred_element_type=F32)
            return m_new, l

        _, l = lax.fori_loop(0, n_blocks, kv_body, (jnp.full((1, cols), NEG, F32), jnp.zeros((1, cols), F32)))
        out = acc_scr[...] / l
        for g in range(group):
            c0 = (h * group + g) * dh
            o_ref[:, c0:c0 + dh] = out[:, g * tq:(g + 1) * tq].T.astype(o_ref.dtype)


def _dsa(iq, ik, iwt, q, k, vt, *, batch, tq_total, tq, kb, causal, s_real, n_sel):
    dh = ik.shape[1]
    s_pad = vt.shape[2]
    kvh = k.shape[1] // dh
    heads = q.shape[1] // dh
    n_idx = iq.shape[1] // dh
    nt = tq_total // tq
    cols = heads // kvh * tq
    return pl.pallas_call(
        functools.partial(_dsa_kernel, tq=tq, kb=kb, causal=causal, s_real=s_real, n_sel=n_sel, n_idx=n_idx,
                          kvh=kvh, group=heads // kvh, dh=dh, iscale=(n_idx * dh) ** -0.5),
        grid=(batch, nt),
        in_specs=[pl.BlockSpec((tq, n_idx * dh), lambda b, t: (b * nt + t, 0)),
                  pl.BlockSpec((s_pad, dh), lambda b, t: (b, 0)),
                  pl.BlockSpec((n_idx, tq), lambda b, t: (0, b * nt + t)),
                  pl.BlockSpec((tq, heads * dh), lambda b, t: (b * nt + t, 0)),
                  pl.BlockSpec((s_pad, kvh * dh), lambda b, t: (b, 0)),
                  pl.BlockSpec((1, kvh * dh, s_pad), lambda b, t: (b, 0, 0))],
        out_specs=pl.BlockSpec((tq, heads * dh), lambda b, t: (b * nt + t, 0)),
        out_shape=jax.ShapeDtypeStruct((batch * tq_total, heads * dh), BF16),
        scratch_shapes=[pltpu.VMEM((s_pad, tq), jnp.int32), pltpu.VMEM((dh, cols), F32)],
        compiler_params=_cparams(2),
        name="dsa",
    )(iq, ik, iwt, q, k, vt)


def _mixer_a(xn, tabs, w_in, lams, subln_g, cache_k, cache_v, lam_init, dims):
    bp, tp, bs, ts, d, past = dims
    mp = bp * tp
    dh = d // (2 * A_HEADS)
    a_q = A_HEADS * 2 * dh
    a_kv = A_KV_HEADS * 2 * dh
    group = A_HEADS // A_KV_HEADS
    (q,) = _proj(xn, w_in[:, :a_q].astype(BF16), (BF16,), rope=tabs, out_scale=dh ** -0.5 * LOG2E)
    k32, k16 = _proj(xn, w_in[:, a_q:a_q + a_kv].astype(BF16), (F32, BF16), rope=tabs)
    v32, v16 = _proj(xn, w_in[:, a_q + a_kv:].astype(BF16), (F32, BF16))
    lams = tuple(x.reshape(1, dh).astype(F32) for x in lams)
    subln_g = subln_g.astype(F32)
    vt_p = jnp.swapaxes(v16[:mp].reshape(bp, tp, a_kv), 1, 2)
    o_p = _diff_prompt(q, k16, vt_p, lams, subln_g, batch=bp, seq=tp, kvh=A_KV_HEADS, group=group, dh=dh,
                       lam_init=lam_init)
    o_s = _diff_sample(q, k16, v16, cache_k.reshape(bs, past, a_kv), cache_v.reshape(bs, past, a_kv), lams,
                       subln_g.reshape(1, 2 * dh), batch=bs, ts=ts, kvh=A_KV_HEADS, group=group, dh=dh,
                       lam_init=lam_init, row0=mp)
    return (o_p, o_s), k32, v32


def _mixer_b(xn, tabs, w_in, cache_k, cache_v, cache_ik, dims):
    bp, tp, bs, ts, d, past = dims
    mp = bp * tp
    dh = d // B_HEADS
    o1 = B_HEADS * dh
    o2 = o1 + B_KV_HEADS * dh
    o3 = o2 + B_KV_HEADS * dh
    o4 = o3 + B_IDX_HEADS * B_IDX_DIM
    o5 = o4 + B_IDX_DIM
    kvw = B_KV_HEADS * dh
    (q,) = _proj(xn, w_in[:, :o1].astype(BF16), (BF16,), rope=tabs, out_scale=dh ** -0.5 * LOG2E)
    k32, k16 = _proj(xn, w_in[:, o1:o2].astype(BF16), (F32, BF16), rope=tabs)
    v32, v16 = _proj(xn, w_in[:, o2:o3].astype(BF16), (F32, BF16))
    (iq,) = _proj(xn, w_in[:, o3:o4].astype(BF16), (BF16,), rope=tabs)
    ik32, ik16 = _proj(xn, w_in[:, o4:o5].astype(BF16), (F32, BF16), rope=tabs)
    w_iw = jnp.pad(w_in[:, o5:], ((0, 0), (0, LANES - B_IDX_HEADS))).astype(BF16)
    (iw,) = _proj(xn, w_iw, (F32,))
    iwt = iw[:, :B_IDX_HEADS].T

    vt_p = jnp.swapaxes(v16[:mp].reshape(bp, tp, kvw), 1, 2)
    tq_p = _pick(tp, 256, LANES)
    o_p = _dsa(iq, ik16, iwt, q, k16, vt_p, batch=bp, tq_total=tp, tq=tq_p, kb=_pick(tp, 512, LANES),
               causal=True, s_real=tp, n_sel=min(B_TOPK_MAX, tp // 4))

    past = cache_k.shape[1]
    s_real = past + ts
    s_pad = -(-s_real // LANES) * LANES
    tq_s = -(-ts // LANES) * LANES

    def keys(cache, new):
        new = new[mp:].reshape(bs, ts, -1)
        kk = jnp.concatenate([cache.reshape(bs, past, -1).astype(BF16), new,
                              jnp.zeros((bs, s_pad - s_real, new.shape[-1]), BF16)], axis=1)
        return kk

    def queries(a):
        a = jnp.pad(a[mp:].reshape(bs, ts, -1), ((0, 0), (0, tq_s - ts), (0, 0)))
        return a.reshape(bs * tq_s, -1)

    kk = keys(cache_k, k16).reshape(bs * s_pad, kvw)
    ikk = keys(cache_ik, ik16).reshape(bs * s_pad, B_IDX_DIM)
    vt_s = jnp.swapaxes(keys(cache_v, v16), 1, 2)
    o_s = _dsa(queries(iq), ikk, queries(iw[:, :B_IDX_HEADS]).T, queries(q), kk, vt_s, batch=bs, tq_total=tq_s,
               tq=tq_s, kb=_pick(s_pad, 512, LANES), causal=False, s_real=s_real,
               n_sel=min(B_TOPK_MAX, s_real // 4))
    o_s = o_s.reshape(bs, tq_s, o1)[:, :ts].reshape(bs * ts, o1)
    return (o_p, o_s), k32, v32, ik32


def _mixer_c(xn, w_in, rel_bias, cache_k, cache_v, dims):
    bp, tp, bs, ts, d, past = dims
    mp = bp * tp
    dh = d // C_HEADS
    hw = C_HEADS * dh
    win = C_LEFT_CHUNKS * CHUNK
    (q,) = _proj(xn, w_in[:, :hw].astype(BF16), (BF16,), out_scale=dh ** -0.5 * LOG2E)
    kv32, kv16 = _proj(xn, w_in[:, hw:].astype(BF16), (F32, BF16))
    rel_bias = rel_bias.astype(F32) * LOG2E

    chunk_bias = _band_bias(rel_bias, win, CHUNK, 0, win + CHUNK)
    o_p = _band_prompt(q, kv16, chunk_bias, batch=bp, seq=tp, heads=C_HEADS, dh=dh, row0=0)

    w_c = cache_k.shape[1]
    o_s = _band_sample(q, kv16, cache_k.reshape(bs, w_c, hw), cache_v.reshape(bs, w_c, hw),
                       _band_bias(rel_bias, past, ts, past - w_c, w_c + ts),
                       batch=bs, ts=ts, heads=C_HEADS, dh=dh, row0=mp)
    return (o_p, o_s), kv32[:, :hw], kv32[:, hw:]


def kernel(x_prompt, x_sample, cache_a_k, cache_a_v, cache_b_k, cache_b_v, cache_b_idx_k, cache_c_k, cache_c_v,
           norm_mix_g, norm_ffn_g, norm_out_g, a_w_in, a_w_out, a_lam_q1, a_lam_k1, a_lam_q2, a_lam_k2,
           a_subln_g, b_w_in, b_w_out, c_w_in, c_w_out, c_rel_bias, ffn_w_gu, ffn_w_down):
    bp, tp, d = x_prompt.shape
    bs, ts, _ = x_sample.shape
    past = cache_a_k.shape[2]
    depth = norm_mix_g.shape[0]
    assert ts <= CHUNK and past % CHUNK == 0 and tp % CHUNK == 0
    mp, ms = bp * tp, bs * ts
    dims = (bp, tp, bs, ts, d, past)
    i32 = jnp.int32

    h = jnp.concatenate([x_prompt.reshape(mp, d), x_sample.reshape(ms, d)], axis=0)
    pos = jnp.concatenate([jnp.tile(jnp.arange(tp, dtype=i32), bp), jnp.tile(past + jnp.arange(ts, dtype=i32), bs)])
    tabs = _rope_tables(pos)
    w_gu16 = ffn_w_gu.astype(BF16)
    w_down16 = ffn_w_down.astype(BF16)

    def split(a, tail):
        return a[:mp].reshape((bp, tp) + tail), a[mp:].reshape((bs, ts) + tail)

    st = {name: [] for name in ("a_k", "a_v", "b_k", "b_v", "b_ik", "c_k", "c_v")}
    for i in range(depth):
        j = i // N_MIXERS
        kind = i % N_MIXERS
        xn = _rmsnorm(h, norm_mix_g[i], BF16)
        if kind == 0:
            lam_init = 0.8 - 0.6 * math.exp(-0.3 * i)
            o, k, v = _mixer_a(xn, tabs, a_w_in[j], (a_lam_q1[j], a_lam_k1[j], a_lam_q2[j], a_lam_k2[j]),
                               a_subln_g[j], cache_a_k[j], cache_a_v[j], lam_init, dims)
            tail = cache_a_k.shape[3:]
            st["a_k"].append(split(k, tail))
            st["a_v"].append(split(v, tail))
            w_out = a_w_out[j]
        elif kind == 1:
            o, k, v, ik = _mixer_b(xn, tabs, b_w_in[j], cache_b_k[j], cache_b_v[j], cache_b_idx_k[j], dims)
            tail = cache_b_k.shape[3:]
            st["b_k"].append(split(k, tail))
            st["b_v"].append(split(v, tail))
            st["b_ik"].append(split(ik, cache_b_idx_k.shape[3:]))
            w_out = b_w_out[j]
        else:
            o, k, v = _mixer_c(xn, c_w_in[j], c_rel_bias[j], cache_c_k[j], cache_c_v[j], dims)
            tail = cache_c_k.shape[3:]
            new = []
            for a, cache in ((k, cache_c_k[j]), (v, cache_c_v[j])):
                a_p, a_s = split(a, tail)
                win = C_LEFT_CHUNKS * CHUNK
                kk_p = jnp.concatenate([jnp.zeros((bp, win) + tail, a.dtype), a_p], axis=1)
                kk_s = jnp.concatenate([cache, a_s], axis=1)
                keep_p, keep_s = min(win, tp), cache.shape[1]
                new.append((kk_p[:, kk_p.shape[1] - keep_p:], kk_s[:, kk_s.shape[1] - keep_s:]))
            st["c_k"].append(new[0])
            st["c_v"].append(new[1])
            w_out = c_w_out[j]
        h = _out_proj(o[0], o[1], w_out.astype(BF16), h)
        h = _ffn(h, norm_ffn_g[i], w_gu16, w_down16, i)

    y_p = _rmsnorm(h, norm_out_g, F32, 0, mp).reshape(bp, tp, d)
    y_s = _rmsnorm(h, norm_out_g, F32, mp, ms).reshape(bs, ts, d)
    stack = lambda name, g: jnp.stack([pair[g] for pair in st[name]], axis=0)
    names = ("a_k", "a_v", "b_k", "b_v", "b_ik", "c_k", "c_v")
    return (y_p, y_s) + tuple(stack(n, 0) for n in names) + tuple(stack(n, 1) for n in names)
```

```python
import functools
import math

import jax
import jax.numpy as jnp
import numpy as np
from jax import lax
from jax.experimental import pallas as pl
from jax.experimental.pallas import tpu as pltpu

F32 = jnp.float32
BF16 = jnp.bfloat16

CHUNK = 64
N_MIXERS = 3
EPS = 1e-6
ROPE_THETA = 500000.0
NEG = -1e30
A_HEADS = 16
A_KV_HEADS = 4
B_HEADS = 32
B_KV_HEADS = 4
B_IDX_HEADS = 32
B_IDX_DIM = 128
B_TOPK_MAX = 256
C_HEADS = 32
C_LEFT_CHUNKS = 8
C_REL_CLIP = 256

LANES = 128
VMEM_LIMIT = 56 * 1024 * 1024
FFN_VMEM_LIMIT = 58 * 1024 * 1024
INT_MIN = -2 ** 31
LOG2E = math.log2(math.e)


def _cparams(n_grid, vmem_limit=VMEM_LIMIT):
    return pltpu.CompilerParams(dimension_semantics=("arbitrary",) * n_grid,
                                vmem_limit_bytes=vmem_limit)


def _pick(n, target, mult=16):
    for t in range(min(n, target), 0, -1):
        if n % t == 0 and t % mult == 0:
            return t
    return n


def _rmsnorm_kernel(x_ref, g_ref, o_ref):
    x = x_ref[...]
    y = x * lax.rsqrt(jnp.mean(x * x, axis=-1, keepdims=True) + EPS)
    o_ref[...] = (y * g_ref[...]).astype(o_ref.dtype)


def _rmsnorm(x, g, out_dtype, row0=0, rows=None):
    d = x.shape[1]
    m = x.shape[0] if rows is None else rows
    tm = _pick(math.gcd(m, row0) if row0 else m, 512)
    b0 = row0 // tm
    return pl.pallas_call(
        _rmsnorm_kernel,
        grid=(m // tm,),
        in_specs=[pl.BlockSpec((tm, d), lambda i: (b0 + i, 0)),
                  pl.BlockSpec((1, d), lambda i: (0, 0))],
        out_specs=pl.BlockSpec((tm, d), lambda i: (i, 0)),
        out_shape=jax.ShapeDtypeStruct((m, d), out_dtype),
        compiler_params=_cparams(1),
        name="rmsnorm",
    )(x, g.reshape(1, d).astype(F32))


def _rope_tables(pos, dh=LANES):
    r = dh // 4
    half = r // 2
    inv = ROPE_THETA ** (-2.0 * jnp.arange(half, dtype=F32) / r)
    ang = pos.astype(F32)[:, None] * inv[None, :]
    cos, sin = jnp.cos(ang), jnp.sin(ang)
    rows = pos.shape[0]
    zeros = jnp.zeros((rows, dh - r), F32)
    zh = jnp.zeros((rows, half), F32)
    c = jnp.concatenate([cos, cos, jnp.ones((rows, dh - r), F32)], axis=1)
    sa = jnp.concatenate([zh, sin, zeros], axis=1)
    sb = jnp.concatenate([-sin, zh, zeros], axis=1)
    return c, sa, sb


def _rope_slab(x, c, sa, sb):
    return x * c + pltpu.roll(x, 16, 1) * sa + pltpu.roll(x, LANES - 16, 1) * sb


def _proj_kernel(*refs, rope, residual, n_out, out_scale):
    x_ref, w_ref = refs[0], refs[1]
    pos = 2
    if rope:
        c_ref, sa_ref, sb_ref = refs[pos:pos + 3]
        pos += 3
    if residual:
        r_ref = refs[pos]
        pos += 1
    outs = refs[pos:pos + n_out]
    acc = jnp.dot(x_ref[...], w_ref[...], preferred_element_type=F32)
    if rope:
        c, sa, sb = c_ref[...], sa_ref[...], sb_ref[...]
        tn = acc.shape[1]
        acc = jnp.concatenate(
            [_rope_slab(acc[:, s * LANES:(s + 1) * LANES], c, sa, sb) for s in range(tn // LANES)], axis=1)
    if residual:
        acc = acc + r_ref[...]
    if out_scale != 1.0:
        acc = acc * out_scale
    for o in outs:
        o[...] = acc.astype(o.dtype)


def _proj(x, w, out_dtypes, rope=None, residual=None, out_scale=1.0, tm_target=1024, tn_target=512):
    m, k = x.shape
    n = w.shape[1]
    tm = _pick(m, tm_target)
    tn = _pick(n, tn_target, LANES)
    in_specs = [pl.BlockSpec((tm, k), lambda i, j: (i, 0)),
                pl.BlockSpec((None, k, tn), lambda i, j: (j, 0, 0))]
    args = [x, _col_blocks(w, tn)]
    if rope is not None:
        in_specs += [pl.BlockSpec((tm, LANES), lambda i, j: (i, 0))] * 3
        args += list(rope)
    if residual is not None:
        in_specs.append(pl.BlockSpec((tm, tn), lambda i, j: (i, j)))
        args.append(residual)
    outs = pl.pallas_call(
        functools.partial(_proj_kernel, rope=rope is not None, residual=residual is not None,
                          n_out=len(out_dtypes), out_scale=out_scale),
        grid=(m // tm, n // tn),
        in_specs=in_specs,
        out_specs=[pl.BlockSpec((tm, tn), lambda i, j: (i, j)) for _ in out_dtypes],
        out_shape=[jax.ShapeDtypeStruct((m, n), dt) for dt in out_dtypes],
        compiler_params=_cparams(2),
        name="proj",
    )(*args)
    return outs


def _out_proj_kernel(xp_ref, xs_ref, w_ref, r_ref, o_ref, *, n_p):
    i = pl.program_id(0)

    @pl.when(i < n_p)
    def _():
        o_ref[...] = r_ref[...] + jnp.dot(xp_ref[...], w_ref[...], preferred_element_type=F32)

    @pl.when(i >= n_p)
    def _():
        o_ref[...] = r_ref[...] + jnp.dot(xs_ref[...], w_ref[...], preferred_element_type=F32)


def _out_proj(x_p, x_s, w, residual, tm_target=1024, tn_target=512):
    (m_p, k), m_s = x_p.shape, x_s.shape[0]
    n = w.shape[1]
    tm = _pick(math.gcd(m_p, m_s), tm_target)
    tn = _pick(n, tn_target, LANES)
    n_p = m_p // tm
    return pl.pallas_call(
        functools.partial(_out_proj_kernel, n_p=n_p),
        grid=((m_p + m_s) // tm, n // tn),
        in_specs=[pl.BlockSpec((tm, k), lambda i, j: (jnp.minimum(i, n_p - 1), 0)),
                  pl.BlockSpec((tm, k), lambda i, j: (jnp.maximum(i - n_p, 0), 0)),
                  pl.BlockSpec((None, k, tn), lambda i, j: (j, 0, 0)),
                  pl.BlockSpec((tm, tn), lambda i, j: (i, j))],
        out_specs=pl.BlockSpec((tm, tn), lambda i, j: (i, j)),
        out_shape=jax.ShapeDtypeStruct((m_p + m_s, n), F32),
        compiler_params=_cparams(2),
        name="out_proj",
    )(x_p, x_s, _col_blocks(w, tn), residual)


FFN_DOWN_CHUNKS = 4


def _ffn_kernel(*refs, next_norm):
    if next_norm:
        h_ref, g_ref, wg_ref, wu_ref, wd_ref, gn_ref, o_ref, xo_ref, xn_ref = refs
    else:
        h_ref, g_ref, wg_ref, wu_ref, wd_ref, o_ref, xn_ref = refs
    j = pl.program_id(1)

    @pl.when(j == 0)
    def _():
        x = h_ref[...]
        y = x * lax.rsqrt(jnp.mean(x * x, axis=-1, keepdims=True) + EPS)
        xn_ref[...] = (y * g_ref[...]).astype(BF16)
        o_ref[...] = x

    xn = xn_ref[...]
    gate = jnp.dot(xn, wg_ref[...], preferred_element_type=F32)
    up = jnp.dot(xn, wu_ref[...], preferred_element_type=F32)
    act = (gate * (1.0 / (1.0 + jnp.exp(-gate))) * up).astype(BF16)
    cw = o_ref.shape[1] // FFN_DOWN_CHUNKS
    for c in range(FFN_DOWN_CHUNKS):
        cs = slice(c * cw, (c + 1) * cw)
        o_ref[:, cs] += jnp.dot(act, wd_ref[:, cs], preferred_element_type=F32)

    if next_norm:
        @pl.when(j == pl.num_programs(1) - 1)
        def _():
            x = o_ref[...]
            y = x * lax.rsqrt(jnp.mean(x * x, axis=-1, keepdims=True) + EPS)
            xo_ref[...] = (y * gn_ref[...]).astype(xo_ref.dtype)


FFN_TF = 256


def _col_blocks(w, tn):
    *lead, k, n = w.shape
    return jnp.swapaxes(w.reshape(*lead, k, n // tn, tn), -3, -2)


def _ffn(h, g, w_gu, w_down, layer, g_next=None, tm_target=512):
    m, d = h.shape
    f = w_down.shape[1]
    tf = w_gu.shape[3]
    tm = _pick(m, tm_target)
    nf = f // tf
    row = pl.BlockSpec((tm, d), lambda i, j: (i, 0))
    gain = pl.BlockSpec((1, d), lambda i, j: (0, 0))
    in_specs = [pl.BlockSpec((tm, d), lambda i, j: (i, 0), pipeline_mode=pl.Buffered(1)),
                gain,
                pl.BlockSpec((None, None, d, tf), lambda i, j: (layer, j, 0, 0)),
                pl.BlockSpec((None, None, d, tf), lambda i, j: (layer, j + nf, 0, 0)),
                pl.BlockSpec((None, tf, d), lambda i, j: (layer, j, 0))]
    args = [h, g.reshape(1, d).astype(F32), w_gu, w_gu, w_down]
    out_specs, out_shape = row, jax.ShapeDtypeStruct((m, d), F32)
    if g_next is not None:
        in_specs.append(gain)
        args.append(g_next.reshape(1, d).astype(F32))
        out_specs, out_shape = [row, row], [out_shape, jax.ShapeDtypeStruct((m, d), BF16)]
    return pl.pallas_call(
        functools.partial(_ffn_kernel, next_norm=g_next is not None),
        grid=(m // tm, nf),
        in_specs=in_specs,
        out_specs=out_specs,
        out_shape=out_shape,
        scratch_shapes=[pltpu.VMEM((tm, d), BF16)],
        compiler_params=_cparams(2, FFN_VMEM_LIMIT),
        name="ffn",
    )(*args)


_DN_T = (((1,), (1,)), ((), ()))


BAND_HEADS_PER_STEP = 4


def _band_prompt_kernel(q_ref, kp_ref, kc_ref, vp_ref, vc_ref, bias_ref, o_ref, bias_scr, *, tq, dh):
    t = pl.program_id(2)
    hp = bias_ref.shape[0]

    @pl.when((pl.program_id(1) == 0) & (t == 0))
    def _():
        bias_scr[...] = jnp.full(bias_scr.shape, NEG, F32)
        width = bias_ref.shape[3]
        for u in range(hp):
            for cc in range(tq // CHUNK):
                c0 = (cc // 2) * LANES
                bias_scr[u, cc * CHUNK:(cc + 1) * CHUNK, c0:c0 + width] = bias_ref[u, cc % 2]

    col = lax.broadcasted_iota(jnp.int32, (1, 2 * tq), 1)
    before_start = jnp.where((col < tq) & (t == 0), NEG, 0.0)
    for u in range(hp):
        hs = slice(u * dh, (u + 1) * dh)
        k = jnp.concatenate([kp_ref[:, hs], kc_ref[:, hs]], axis=0)
        v = jnp.concatenate([vp_ref[:, hs], vc_ref[:, hs]], axis=0)
        s = lax.dot_general(q_ref[:, hs], k, _DN_T, preferred_element_type=F32) + bias_scr[u] + before_start
        m = jnp.max(s, axis=1, keepdims=True)
        p = jnp.exp2(s - m)
        l = jnp.sum(p, axis=1, keepdims=True)
        o = jnp.dot(p.astype(BF16), v, preferred_element_type=F32) / l
        o_ref[:, hs] = o.astype(o_ref.dtype)


def _band_prompt(q, kv, chunk_bias, *, batch, seq, heads, dh, row0):
    assert 2 * CHUNK == LANES
    hp = BAND_HEADS_PER_STEP
    tq = chunk_bias.shape[2] - CHUNK
    nt = seq // tq
    rb0 = row0 // tq
    nh = heads // hp
    pad = jnp.full((heads, CHUNK, CHUNK), NEG, F32)
    bias2 = jnp.stack([jnp.concatenate([chunk_bias, pad], axis=2), jnp.concatenate([pad, chunk_bias], axis=2)],
                      axis=1)
    cur = lambda h, b, t: (rb0 + b * nt + t, h)
    prev = lambda h, b, t: (rb0 + b * nt + jnp.maximum(t - 1, 0), h)
    cur_v = lambda h, b, t: (rb0 + b * nt + t, nh + h)
    prev_v = lambda h, b, t: (rb0 + b * nt + jnp.maximum(t - 1, 0), nh + h)
    blk = (tq, hp * dh)
    return pl.pallas_call(
        functools.partial(_band_prompt_kernel, tq=tq, dh=dh),
        grid=(nh, batch, nt),
        in_specs=[pl.BlockSpec(blk, cur),
                  pl.BlockSpec(blk, prev), pl.BlockSpec(blk, cur),
                  pl.BlockSpec(blk, prev_v), pl.BlockSpec(blk, cur_v),
                  pl.BlockSpec((hp, 2, CHUNK, tq + 2 * CHUNK), lambda h, b, t: (h, 0, 0, 0))],
        out_specs=pl.BlockSpec(blk, lambda h, b, t: (b * nt + t, h)),
        out_shape=jax.ShapeDtypeStruct((batch * seq, heads * dh), BF16),
        scratch_shapes=[pltpu.VMEM((hp, tq, 2 * tq), F32)],
        compiler_params=_cparams(3),
        name="band_prompt",
    )(q, kv, kv, kv, kv, bias2)


def _band_sample_kernel(q_ref, kc_ref, vc_ref, kn_ref, vn_ref, bias_ref, o_ref, *, w, dh):
    for u in range(bias_ref.shape[0]):
        hs = slice(u * dh, (u + 1) * dh)
        q = q_ref[:, hs]
        kc = kc_ref[0, :, hs].astype(BF16)
        vc = vc_ref[0, :, hs].astype(BF16)
        bias = bias_ref[u]
        sc = lax.dot_general(q, kc, _DN_T, preferred_element_type=F32) + bias[:, :w]
        sn = lax.dot_general(q, kn_ref[:, hs], _DN_T, preferred_element_type=F32) + bias[:, w:]
        m = jnp.maximum(jnp.max(sc, axis=1, keepdims=True), jnp.max(sn, axis=1, keepdims=True))
        pc = jnp.exp2(sc - m)
        pn = jnp.exp2(sn - m)
        l = jnp.sum(pc, axis=1, keepdims=True) + jnp.sum(pn, axis=1, keepdims=True)
        o = (jnp.dot(pc.astype(BF16), vc, preferred_element_type=F32)
             + jnp.dot(pn.astype(BF16), vn_ref[:, hs], preferred_element_type=F32)) / l
        o_ref[:, hs] = o.astype(o_ref.dtype)


def _band_sample(q, kv, cache_k, cache_v, bias, *, batch, ts, heads, dh, row0):
    hp = BAND_HEADS_PER_STEP
    w = cache_k.shape[1]
    rb0 = row0 // ts
    nh = heads // hp
    return pl.pallas_call(
        functools.partial(_band_sample_kernel, w=w, dh=dh),
        grid=(nh, batch),
        in_specs=[pl.BlockSpec((ts, hp * dh), lambda h, b: (rb0 + b, h)),
                  pl.BlockSpec((1, w, hp * dh), lambda h, b: (b, 0, h)),
                  pl.BlockSpec((1, w, hp * dh), lambda h, b: (b, 0, h)),
                  pl.BlockSpec((ts, hp * dh), lambda h, b: (rb0 + b, h)),
                  pl.BlockSpec((ts, hp * dh), lambda h, b: (rb0 + b, nh + h)),
                  pl.BlockSpec((hp, ts, w + ts), lambda h, b: (h, 0, 0))],
        out_specs=pl.BlockSpec((ts, hp * dh), lambda h, b: (b, h)),
        out_shape=jax.ShapeDtypeStruct((batch * ts, heads * dh), BF16),
        compiler_params=_cparams(2),
        name="band_sample",
    )(q, cache_k, cache_v, kv, kv, bias)


def _band_bias(rel_bias, q0, nq, k0, nk):
    diag = np.arange(nq + nk - 1) - (nq - 1)
    idx = np.clip(q0 - k0 - diag, -C_REL_CLIP, C_REL_CLIP) + C_REL_CLIP
    r = rel_bias[idx].T.astype(F32)
    bias = jnp.stack([r[:, nq - 1 - i:nq - 1 - i + nk] for i in range(nq)], axis=1)
    qpos = q0 + np.arange(nq)[:, None]
    kpos = k0 + np.arange(nk)[None, :]
    qch, kch = qpos // CHUNK, kpos // CHUNK
    ok = (kpos >= 0) & (kch <= qch) & (qch - kch <= C_LEFT_CHUNKS)
    return jnp.where(ok[None], bias, NEG)


def _diff_lambda(lq1_ref, lk1_ref, lq2_ref, lk2_ref, lam_init):
    s1 = jnp.sum(lq1_ref[...] * lk1_ref[...], axis=1, keepdims=True)
    s2 = jnp.sum(lq2_ref[...] * lk2_ref[...], axis=1, keepdims=True)
    return jnp.exp(s1) - jnp.exp(s2) + lam_init


def _stack_groups(q_ref, c, group, dh):
    return jnp.concatenate([q_ref[:, (g * 2 + c) * dh:(g * 2 + c + 1) * dh] for g in range(group)], axis=0)


def _diff_finish(o, g_ref, o_ref, *, group, tq, out_scale):
    y = o * lax.rsqrt(jnp.mean(o * o, axis=-1, keepdims=True) + EPS) * g_ref[...] * out_scale
    y = y.astype(o_ref.dtype)
    e = y.shape[1]
    for g in range(group):
        o_ref[:, g * e:(g + 1) * e] = y[g * tq:(g + 1) * tq]


def _diff_prompt_kernel(q_ref, k_ref, vt_ref, lq1_ref, lk1_ref, lq2_ref, lk2_ref, g_ref, o_ref,
                        acc_scr, s_scr, m_scr, l_scr, *, tq, kb, group, dh, lam_init):
    t = pl.program_id(2)
    rows = group * tq
    e = 2 * dh
    lam = _diff_lambda(lq1_ref, lk1_ref, lq2_ref, lk2_ref, lam_init)
    acc_scr[...] = jnp.zeros(acc_scr.shape, F32)
    m_scr[...] = jnp.full(m_scr.shape, NEG, F32)
    l_scr[...] = jnp.zeros(l_scr.shape, F32)
    n_blocks = ((t + 1) * tq + kb - 1) // kb
    n_free = n_blocks - 1

    def scores(i, slot):
        ks = pl.multiple_of(i * kb, kb)
        kblk = k_ref[pl.ds(ks, kb), :]
        for c in range(2):
            qc = _stack_groups(q_ref, c, group, dh)
            s_scr[slot, c] = lax.dot_general(kblk[:, c * dh:(c + 1) * dh], qc, _DN_T,
                                             preferred_element_type=F32)

    def accumulate(i, slot, masked):
        ks = pl.multiple_of(i * kb, kb)
        vtb = vt_ref[0, :, pl.ds(ks, kb)]
        if masked:
            qi = lax.broadcasted_iota(jnp.int32, (kb, rows), 1) % tq
            qend = ((t * tq + qi) // CHUNK + 1) * CHUNK
            kpos = ks + lax.broadcasted_iota(jnp.int32, (kb, rows), 0)
            bias = jnp.where(kpos < qend, 0.0, NEG)
        for c in range(2):
            s = s_scr[slot, c]
            if masked:
                s = s + bias
            m = m_scr[c]
            m_new = jnp.maximum(m, jnp.max(s, axis=0, keepdims=True))
            alpha = jnp.exp2(m - m_new)
            p = jnp.exp2(s - m_new)
            l_scr[c] = alpha * l_scr[c] + jnp.sum(p, axis=0, keepdims=True)
            m_scr[c] = m_new
            acc_scr[c] = alpha * acc_scr[c] + jnp.dot(vtb, p.astype(BF16), preferred_element_type=F32)

    scores(0, 0)

    def pair(j, carry):
        i = 2 * j
        scores(i + 1, 1)
        accumulate(i, 0, False)
        scores(i + 2, 0)
        accumulate(i + 1, 1, False)
        return carry

    lax.fori_loop(0, n_free // 2, pair, 0)
    i0 = (n_free // 2) * 2

    @pl.when(n_free % 2 == 1)
    def _():
        scores(i0 + 1, 1)
        accumulate(i0, 0, False)
        accumulate(i0 + 1, 1, True)

    @pl.when(n_free % 2 == 0)
    def _():
        accumulate(i0, 0, True)

    o = acc_scr[0] / l_scr[0] - lam * (acc_scr[1] / l_scr[1])
    y = o * lax.rsqrt(jnp.mean(o * o, axis=0, keepdims=True) + EPS) * (1.0 - lam_init)
    for g in range(group):
        o_ref[:, g * e:(g + 1) * e] = (y[:, g * tq:(g + 1) * tq].T * g_ref[...]).astype(o_ref.dtype)


def _lam_specs(n_grid):
    zero = (lambda *_: (0, 0))
    return [pl.BlockSpec((1, LANES), zero)] * 4


def _diff_prompt(q, k, vt, lams, subln_g, *, batch, seq, kvh, group, dh, lam_init, tq=LANES, kb=512):
    kb = _pick(seq, kb, LANES)
    nt = seq // tq
    rows = group * tq
    e = 2 * dh
    return pl.pallas_call(
        functools.partial(_diff_prompt_kernel, tq=tq, kb=kb, group=group, dh=dh, lam_init=lam_init),
        grid=(batch, kvh, nt),
        in_specs=[pl.BlockSpec((tq, group * e), lambda b, h, t: (b * nt + t, h)),
                  pl.BlockSpec((seq, e), lambda b, h, t: (b, h)),
                  pl.BlockSpec((1, e, seq), lambda b, h, t: (b, h, 0))]
                 + _lam_specs(3) + [pl.BlockSpec((1, e), lambda b, h, t: (0, 0))],
        out_specs=pl.BlockSpec((tq, group * e), lambda b, h, t: (b * nt + t, h)),
        out_shape=jax.ShapeDtypeStruct((batch * seq, kvh * group * e), BF16),
        scratch_shapes=[pltpu.VMEM((2, e, rows), F32), pltpu.VMEM((2, 2, kb, rows), F32),
                        pltpu.VMEM((2, 1, rows), F32), pltpu.VMEM((2, 1, rows), F32)],
        compiler_params=_cparams(3),
        name="diff_prompt",
    )(q, k, vt, *lams, subln_g.reshape(1, e))


def _diff_sample_kernel(q_ref, kc_ref, vc_ref, kn_ref, vn_ref, lq1_ref, lk1_ref, lq2_ref, lk2_ref, g_ref, o_ref,
                        *, ts, group, dh, lam_init):
    lam = _diff_lambda(lq1_ref, lk1_ref, lq2_ref, lk2_ref, lam_init)
    kc = kc_ref[0].astype(BF16)
    vc = vc_ref[0].astype(BF16)
    kn = kn_ref[...]
    a_c, a_n = [], []
    for c in range(2):
        qc = _stack_groups(q_ref, c, group, dh)
        sc = lax.dot_general(qc, kc[:, c * dh:(c + 1) * dh], _DN_T, preferred_element_type=F32)
        sn = lax.dot_general(qc, kn[:, c * dh:(c + 1) * dh], _DN_T, preferred_element_type=F32)
        m = jnp.maximum(jnp.max(sc, axis=1, keepdims=True), jnp.max(sn, axis=1, keepdims=True))
        pc = jnp.exp2(sc - m)
        pn = jnp.exp2(sn - m)
        l = jnp.sum(pc, axis=1, keepdims=True) + jnp.sum(pn, axis=1, keepdims=True)
        a_c.append(pc / l)
        a_n.append(pn / l)
    ac = (a_c[0] - lam * a_c[1]).astype(BF16)
    an = (a_n[0] - lam * a_n[1]).astype(BF16)
    o = (jnp.dot(ac, vc, preferred_element_type=F32) + jnp.dot(an, vn_ref[...], preferred_element_type=F32))
    _diff_finish(o, g_ref, o_ref, group=group, tq=ts, out_scale=1.0 - lam_init)


def _diff_sample(q, k, v, cache_k, cache_v, lams, subln_g, *, batch, ts, kvh, group, dh, lam_init, row0):
    past = cache_k.shape[1]
    rb0 = row0 // ts
    e = 2 * dh
    return pl.pallas_call(
        functools.partial(_diff_sample_kernel, ts=ts, group=group, dh=dh, lam_init=lam_init),
        grid=(batch, kvh),
        in_specs=[pl.BlockSpec((ts, group * e), lambda b, h: (rb0 + b, h)),
                  pl.BlockSpec((1, past, e), lambda b, h: (b, 0, h)),
                  pl.BlockSpec((1, past, e), lambda b, h: (b, 0, h)),
                  pl.BlockSpec((ts, e), lambda b, h: (rb0 + b, h)),
                  pl.BlockSpec((ts, e), lambda b, h: (rb0 + b, h))]
                 + _lam_specs(2) + [pl.BlockSpec((1, e), lambda b, h: (0, 0))],
        out_specs=pl.BlockSpec((ts, group * e), lambda b, h: (b, h)),
        out_shape=jax.ShapeDtypeStruct((batch * ts, kvh * group * e), BF16),
        compiler_params=_cparams(2),
        name="diff_sample",
    )(q, cache_k, cache_v, k, v, *lams, subln_g)


def _dsa_kernel(iq_ref, ik_ref, iwt_ref, q_ref, k_ref, vt_ref, o_ref, key_scr, acc_scr,
                *, tq, kb, causal, s_real, n_sel, n_idx, kvh, group, dh, iscale):
    t = pl.program_id(1)
    s_pad = ik_ref.shape[0]
    lane = lax.broadcasted_iota(jnp.int32, (1, tq), 1)
    if causal:
        qend = ((t * tq + lane) // CHUNK + 1) * CHUNK
        n_blocks = ((t + 1) * tq + kb - 1) // kb
    else:
        qend = jnp.full((1, tq), s_real, jnp.int32)
        n_blocks = s_pad // kb

    w = iwt_ref[...] * iscale

    def score_body(i, carry):
        ks = pl.multiple_of(i * kb, kb)
        ikb = ik_ref[pl.ds(ks, kb), :]
        sc = jnp.zeros((kb, tq), F32)
        for n in range(n_idx):
            logit = lax.dot_general(ikb, iq_ref[:, n * dh:(n + 1) * dh], _DN_T, preferred_element_type=F32)
            sc = sc + w[n:n + 1, :] * jnp.maximum(logit, 0.0)
        bits = lax.bitcast_convert_type(sc, jnp.int32)
        key = jnp.where(bits < 0, bits ^ 0x7FFFFFFF, bits)
        kpos = ks + lax.broadcasted_iota(jnp.int32, (kb, tq), 0)
        key_scr[pl.ds(ks, kb), :] = jnp.where(kpos < qend, key, INT_MIN)
        return carry

    lax.fori_loop(0, n_blocks, score_body, 0)

    def count_ge(cand):
        def body(i, acc):
            ks = pl.multiple_of(i * kb, kb)
            ge = key_scr[pl.ds(ks, kb), :] >= cand
            return acc + jnp.sum(ge.astype(jnp.int32), axis=0, keepdims=True)
        return lax.fori_loop(0, n_blocks, body, jnp.zeros((1, tq), jnp.int32))

    zero = jnp.zeros((1, tq), jnp.int32)
    thr = jnp.where(count_ge(zero) >= n_sel, zero, jnp.full((1, tq), INT_MIN, jnp.int32))

    def bit_body(i, thr):
        cand = thr | lax.shift_left(jnp.int32(1), 30 - i)
        return jnp.where(count_ge(cand) >= n_sel, cand, thr)

    thr = lax.fori_loop(0, 31, bit_body, thr)
    thr = jnp.maximum(thr, INT_MIN + 1)

    def bias_body(i, carry):
        ks = pl.multiple_of(i * kb, kb)
        bias = jnp.where(key_scr[pl.ds(ks, kb), :] >= thr, 0.0, NEG)
        key_scr[pl.ds(ks, kb), :] = lax.bitcast_convert_type(bias, jnp.int32)
        return carry

    lax.fori_loop(0, n_blocks, bias_body, 0)

    cols = group * tq
    for h in range(kvh):
        acc_scr[...] = jnp.zeros(acc_scr.shape, F32)

        def kv_body(i, st, h=h):
            m, l = st
            ks = pl.multiple_of(i * kb, kb)
            kblk = k_ref[pl.ds(ks, kb), h * dh:(h + 1) * dh]
            qh = jnp.concatenate([q_ref[:, (h * group + g) * dh:(h * group + g + 1) * dh] for g in range(group)],
                                 axis=0)
            bias = lax.bitcast_convert_type(key_scr[pl.ds(ks, kb), :], F32)
            s = lax.dot_general(kblk, qh, _DN_T, preferred_element_type=F32)
            s = s + jnp.concatenate([bias] * group, axis=1)
            m_new = jnp.maximum(m, jnp.max(s, axis=0, keepdims=True))
            alpha = jnp.exp2(m - m_new)
            p = jnp.exp2(s - m_new)
            l = alpha * l + jnp.sum(p, axis=0, keepdims=True)
            vtb = vt_ref[0, h * dh:(h + 1) * dh, pl.ds(ks, kb)]
            acc_scr[...] = alpha * acc_scr[...] + jnp.dot(vtb, p.astype(BF16), preferred_element_type=F32)
            return m_new, l

        _, l = lax.fori_loop(0, n_blocks, kv_body, (jnp.full((1, cols), NEG, F32), jnp.zeros((1, cols), F32)))
        out = acc_scr[...] / l
        for g in range(group):
            c0 = (h * group + g) * dh
            o_ref[:, c0:c0 + dh] = out[:, g * tq:(g + 1) * tq].T.astype(o_ref.dtype)


def _dsa(iq, ik, iwt, q, k, vt, *, batch, tq_total, tq, kb, causal, s_real, n_sel):
    dh = ik.shape[1]
    s_pad = vt.shape[2]
    kvh = k.shape[1] // dh
    heads = q.shape[1] // dh
    n_idx = iq.shape[1] // dh
    nt = tq_total // tq
    cols = heads // kvh * tq
    return pl.pallas_call(
        functools.partial(_dsa_kernel, tq=tq, kb=kb, causal=causal, s_real=s_real, n_sel=n_sel, n_idx=n_idx,
                          kvh=kvh, group=heads // kvh, dh=dh, iscale=(n_idx * dh) ** -0.5),
        grid=(batch, nt),
        in_specs=[pl.BlockSpec((tq, n_idx * dh), lambda b, t: (b * nt + t, 0)),
                  pl.BlockSpec((s_pad, dh), lambda b, t: (b, 0)),
                  pl.BlockSpec((n_idx, tq), lambda b, t: (0, b * nt + t)),
                  pl.BlockSpec((tq, heads * dh), lambda b, t: (b * nt + t, 0)),
                  pl.BlockSpec((s_pad, kvh * dh), lambda b, t: (b, 0)),
                  pl.BlockSpec((1, kvh * dh, s_pad), lambda b, t: (b, 0, 0))],
        out_specs=pl.BlockSpec((tq, heads * dh), lambda b, t: (b * nt + t, 0)),
        out_shape=jax.ShapeDtypeStruct((batch * tq_total, heads * dh), BF16),
        scratch_shapes=[pltpu.VMEM((s_pad, tq), jnp.int32), pltpu.VMEM((dh, cols), F32)],
        compiler_params=_cparams(2),
        name="dsa",
    )(iq, ik, iwt, q, k, vt)


def _mixer_a(xn, tabs, w_in, lams, subln_g, cache_k, cache_v, lam_init, dims):
    bp, tp, bs, ts, d, past = dims
    mp = bp * tp
    dh = d // (2 * A_HEADS)
    a_q = A_HEADS * 2 * dh
    a_kv = A_KV_HEADS * 2 * dh
    group = A_HEADS // A_KV_HEADS
    (q,) = _proj(xn, w_in[:, :a_q].astype(BF16), (BF16,), rope=tabs, out_scale=dh ** -0.5 * LOG2E)
    k32, k16 = _proj(xn, w_in[:, a_q:a_q + a_kv].astype(BF16), (F32, BF16), rope=tabs)
    v32, v16 = _proj(xn, w_in[:, a_q + a_kv:].astype(BF16), (F32, BF16))
    lams = tuple(x.reshape(1, dh).astype(F32) for x in lams)
    subln_g = subln_g.astype(F32)
    vt_p = jnp.swapaxes(v16[:mp].reshape(bp, tp, a_kv), 1, 2)
    o_p = _diff_prompt(q, k16, vt_p, lams, subln_g, batch=bp, seq=tp, kvh=A_KV_HEADS, group=group, dh=dh,
                       lam_init=lam_init)
    o_s = _diff_sample(q, k16, v16, cache_k.reshape(bs, past, a_kv), cache_v.reshape(bs, past, a_kv), lams,
                       subln_g.reshape(1, 2 * dh), batch=bs, ts=ts, kvh=A_KV_HEADS, group=group, dh=dh,
                       lam_init=lam_init, row0=mp)
    return (o_p, o_s), k32, v32


def _mixer_b(xn, tabs, w_in, cache_k, cache_v, cache_ik, dims):
    bp, tp, bs, ts, d, past = dims
    mp = bp * tp
    dh = d // B_HEADS
    o1 = B_HEADS * dh
    o2 = o1 + B_KV_HEADS * dh
    o3 = o2 + B_KV_HEADS * dh
    o4 = o3 + B_IDX_HEADS * B_IDX_DIM
    o5 = o4 + B_IDX_DIM
    kvw = B_KV_HEADS * dh
    (q,) = _proj(xn, w_in[:, :o1].astype(BF16), (BF16,), rope=tabs, out_scale=dh ** -0.5 * LOG2E)
    k32, k16 = _proj(xn, w_in[:, o1:o2].astype(BF16), (F32, BF16), rope=tabs)
    v32, v16 = _proj(xn, w_in[:, o2:o3].astype(BF16), (F32, BF16))
    (iq,) = _proj(xn, w_in[:, o3:o4].astype(BF16), (BF16,), rope=tabs)
    ik32, ik16 = _proj(xn, w_in[:, o4:o5].astype(BF16), (F32, BF16), rope=tabs)
    w_iw = jnp.pad(w_in[:, o5:], ((0, 0), (0, LANES - B_IDX_HEADS))).astype(BF16)
    (iw,) = _proj(xn, w_iw, (F32,))
    iwt = iw[:, :B_IDX_HEADS].T

    vt_p = jnp.swapaxes(v16[:mp].reshape(bp, tp, kvw), 1, 2)
    tq_p = _pick(tp, 256, LANES)
    o_p = _dsa(iq, ik16, iwt, q, k16, vt_p, batch=bp, tq_total=tp, tq=tq_p, kb=_pick(tp, 512, LANES),
               causal=True, s_real=tp, n_sel=min(B_TOPK_MAX, tp // 4))

    past = cache_k.shape[1]
    s_real = past + ts
    s_pad = -(-s_real // LANES) * LANES
    tq_s = -(-ts // LANES) * LANES

    def keys(cache, new):
        new = new[mp:].reshape(bs, ts, -1)
        kk = jnp.concatenate([cache.reshape(bs, past, -1).astype(BF16), new,
                              jnp.zeros((bs, s_pad - s_real, new.shape[-1]), BF16)], axis=1)
        return kk

    def queries(a):
        a = jnp.pad(a[mp:].reshape(bs, ts, -1), ((0, 0), (0, tq_s - ts), (0, 0)))
        return a.reshape(bs * tq_s, -1)

    kk = keys(cache_k, k16).reshape(bs * s_pad, kvw)
    ikk = keys(cache_ik, ik16).reshape(bs * s_pad, B_IDX_DIM)
    vt_s = jnp.swapaxes(keys(cache_v, v16), 1, 2)
    o_s = _dsa(queries(iq), ikk, queries(iw[:, :B_IDX_HEADS]).T, queries(q), kk, vt_s, batch=bs, tq_total=tq_s,
               tq=tq_s, kb=_pick(s_pad, 512, LANES), causal=False, s_real=s_real,
               n_sel=min(B_TOPK_MAX, s_real // 4))
    o_s = o_s.reshape(bs, tq_s, o1)[:, :ts].reshape(bs * ts, o1)
    return (o_p, o_s), k32, v32, ik32


def _mixer_c(xn, w_in, rel_bias, cache_k, cache_v, dims):
    bp, tp, bs, ts, d, past = dims
    mp = bp * tp
    dh = d // C_HEADS
    hw = C_HEADS * dh
    win = C_LEFT_CHUNKS * CHUNK
    (q,) = _proj(xn, w_in[:, :hw].astype(BF16), (BF16,), out_scale=dh ** -0.5 * LOG2E)
    w_kv = w_in[:, hw:].astype(BF16)
    (kv16,) = _proj(xn, w_kv, (BF16,))
    keep = min(win, tp)
    x_new = jnp.concatenate([xn[:mp].reshape(bp, tp, d)[:, tp - keep:].reshape(bp * keep, d), xn[mp:]], axis=0)
    (kv32,) = _proj(x_new, w_kv, (F32,))
    rel_bias = rel_bias.astype(F32) * LOG2E

    chunk_bias = _band_bias(rel_bias, win, CHUNK, 0, win + CHUNK)
    o_p = _band_prompt(q, kv16, chunk_bias, batch=bp, seq=tp, heads=C_HEADS, dh=dh, row0=0)

    w_c = cache_k.shape[1]
    o_s = _band_sample(q, kv16, cache_k.reshape(bs, w_c, hw), cache_v.reshape(bs, w_c, hw),
                       _band_bias(rel_bias, past, ts, past - w_c, w_c + ts),
                       batch=bs, ts=ts, heads=C_HEADS, dh=dh, row0=mp)
    new_p = kv32[:bp * keep].reshape(bp, keep, 2, C_HEADS, dh)
    new_s = kv32[bp * keep:].reshape(bs, ts, 2, C_HEADS, dh)
    return (o_p, o_s), (new_p[:, :, 0], new_s[:, :, 0]), (new_p[:, :, 1], new_s[:, :, 1])


def kernel(x_prompt, x_sample, cache_a_k, cache_a_v, cache_b_k, cache_b_v, cache_b_idx_k, cache_c_k, cache_c_v,
           norm_mix_g, norm_ffn_g, norm_out_g, a_w_in, a_w_out, a_lam_q1, a_lam_k1, a_lam_q2, a_lam_k2,
           a_subln_g, b_w_in, b_w_out, c_w_in, c_w_out, c_rel_bias, ffn_w_gu, ffn_w_down):
    bp, tp, d = x_prompt.shape
    bs, ts, _ = x_sample.shape
    past = cache_a_k.shape[2]
    depth = norm_mix_g.shape[0]
    assert ts <= CHUNK and past % CHUNK == 0 and tp % CHUNK == 0
    mp, ms = bp * tp, bs * ts
    dims = (bp, tp, bs, ts, d, past)
    i32 = jnp.int32

    h = jnp.concatenate([x_prompt.reshape(mp, d), x_sample.reshape(ms, d)], axis=0)
    pos = jnp.concatenate([jnp.tile(jnp.arange(tp, dtype=i32), bp), jnp.tile(past + jnp.arange(ts, dtype=i32), bs)])
    tabs = _rope_tables(pos)
    w_gu16 = _col_blocks(ffn_w_gu.astype(BF16), FFN_TF)
    w_down16 = ffn_w_down.astype(BF16)

    def split(a, tail):
        return a[:mp].reshape((bp, tp) + tail), a[mp:].reshape((bs, ts) + tail)

    st = {name: [] for name in ("a_k", "a_v", "b_k", "b_v", "b_ik", "c_k", "c_v")}
    xn = _rmsnorm(h, norm_mix_g[0], BF16)
    for i in range(depth):
        j = i // N_MIXERS
        kind = i % N_MIXERS
        if kind == 0:
            lam_init = 0.8 - 0.6 * math.exp(-0.3 * i)
            o, k, v = _mixer_a(xn, tabs, a_w_in[j], (a_lam_q1[j], a_lam_k1[j], a_lam_q2[j], a_lam_k2[j]),
                               a_subln_g[j], cache_a_k[j], cache_a_v[j], lam_init, dims)
            tail = cache_a_k.shape[3:]
            st["a_k"].append(split(k, tail))
            st["a_v"].append(split(v, tail))
            w_out = a_w_out[j]
        elif kind == 1:
            o, k, v, ik = _mixer_b(xn, tabs, b_w_in[j], cache_b_k[j], cache_b_v[j], cache_b_idx_k[j], dims)
            tail = cache_b_k.shape[3:]
            st["b_k"].append(split(k, tail))
            st["b_v"].append(split(v, tail))
            st["b_ik"].append(split(ik, cache_b_idx_k.shape[3:]))
            w_out = b_w_out[j]
        else:
            o, k, v = _mixer_c(xn, c_w_in[j], c_rel_bias[j], cache_c_k[j], cache_c_v[j], dims)
            for name, (a_p, a_s), cache in (("c_k", k, cache_c_k[j]), ("c_v", v, cache_c_v[j])):
                kk_s = jnp.concatenate([cache, a_s], axis=1)
                st[name].append((a_p, kk_s[:, kk_s.shape[1] - cache.shape[1]:]))
            w_out = c_w_out[j]
        h = _out_proj(o[0], o[1], w_out.astype(BF16), h)
        if i + 1 < depth:
            h, xn = _ffn(h, norm_ffn_g[i], w_gu16, w_down16, i, g_next=norm_mix_g[i + 1])
        else:
            h = _ffn(h, norm_ffn_g[i], w_gu16, w_down16, i)

    y_p = _rmsnorm(h, norm_out_g, F32, 0, mp).reshape(bp, tp, d)
    y_s = _rmsnorm(h, norm_out_g, F32, mp, ms).reshape(bs, ts, d)
    stack = lambda name, g: jnp.stack([pair[g] for pair in st[name]], axis=0)
    names = ("a_k", "a_v", "b_k", "b_v", "b_ik", "c_k", "c_v")
    return (y_p, y_s) + tuple(stack(n, 0) for n in names) + tuple(stack(n, 1) for n in names)
```

```python
import functools
import math

import jax
import jax.numpy as jnp
import numpy as np
from jax import lax
from jax.experimental import pallas as pl
from jax.experimental.pallas import tpu as pltpu

F32 = jnp.float32
BF16 = jnp.bfloat16

CHUNK = 64
N_MIXERS = 3
EPS = 1e-6
ROPE_THETA = 500000.0
NEG = -1e30
A_HEADS = 16
A_KV_HEADS = 4
B_HEADS = 32
B_KV_HEADS = 4
B_IDX_HEADS = 32
B_IDX_DIM = 128
B_TOPK_MAX = 256
C_HEADS = 32
C_LEFT_CHUNKS = 8
C_REL_CLIP = 256

LANES = 128
VMEM_LIMIT = 56 * 1024 * 1024
FFN_VMEM_LIMIT = 58 * 1024 * 1024
INT_MIN = -2 ** 31
LOG2E = math.log2(math.e)


def _cparams(n_grid, vmem_limit=VMEM_LIMIT):
    return pltpu.CompilerParams(dimension_semantics=("arbitrary",) * n_grid,
                                vmem_limit_bytes=vmem_limit)


def _pick(n, target, mult=16):
    for t in range(min(n, target), 0, -1):
        if n % t == 0 and t % mult == 0:
            return t
    return n


def _rmsnorm_kernel(x_ref, g_ref, o_ref):
    x = x_ref[...]
    y = x * lax.rsqrt(jnp.mean(x * x, axis=-1, keepdims=True) + EPS)
    o_ref[...] = (y * g_ref[...]).astype(o_ref.dtype)


def _rmsnorm(x, g, out_dtype, row0=0, rows=None):
    d = x.shape[1]
    m = x.shape[0] if rows is None else rows
    tm = _pick(math.gcd(m, row0) if row0 else m, 512)
    b0 = row0 // tm
    return pl.pallas_call(
        _rmsnorm_kernel,
        grid=(m // tm,),
        in_specs=[pl.BlockSpec((tm, d), lambda i: (b0 + i, 0)),
                  pl.BlockSpec((1, d), lambda i: (0, 0))],
        out_specs=pl.BlockSpec((tm, d), lambda i: (i, 0)),
        out_shape=jax.ShapeDtypeStruct((m, d), out_dtype),
        compiler_params=_cparams(1),
        name="rmsnorm",
    )(x, g.reshape(1, d).astype(F32))


def _join_norm_kernel(xp_ref, xs_ref, g_ref, h_ref, xn_ref, *, n_p):
    i = pl.program_id(0)

    def emit(x_ref):
        x = x_ref[...]
        h_ref[...] = x
        y = x * lax.rsqrt(jnp.mean(x * x, axis=-1, keepdims=True) + EPS)
        xn_ref[...] = (y * g_ref[...]).astype(xn_ref.dtype)

    @pl.when(i < n_p)
    def _():
        emit(xp_ref)

    @pl.when(i >= n_p)
    def _():
        emit(xs_ref)


def _join_norm(x_p, x_s, g):
    (m_p, d), m_s = x_p.shape, x_s.shape[0]
    tm = _pick(math.gcd(m_p, m_s), 256)
    n_p = m_p // tm
    row = pl.BlockSpec((tm, d), lambda i: (i, 0))
    return pl.pallas_call(
        functools.partial(_join_norm_kernel, n_p=n_p),
        grid=((m_p + m_s) // tm,),
        in_specs=[pl.BlockSpec((tm, d), lambda i: (jnp.minimum(i, n_p - 1), 0)),
                  pl.BlockSpec((tm, d), lambda i: (jnp.maximum(i - n_p, 0), 0)),
                  pl.BlockSpec((1, d), lambda i: (0, 0))],
        out_specs=[row, row],
        out_shape=[jax.ShapeDtypeStruct((m_p + m_s, d), F32), jax.ShapeDtypeStruct((m_p + m_s, d), BF16)],
        compiler_params=_cparams(1),
        name="join_norm",
    )(x_p, x_s, g.reshape(1, d).astype(F32))


def _rope_tables(pos, dh=LANES):
    r = dh // 4
    half = r // 2
    inv = ROPE_THETA ** (-2.0 * jnp.arange(half, dtype=F32) / r)
    ang = pos.astype(F32)[:, None] * inv[None, :]
    cos, sin = jnp.cos(ang), jnp.sin(ang)
    rows = pos.shape[0]
    zeros = jnp.zeros((rows, dh - r), F32)
    zh = jnp.zeros((rows, half), F32)
    c = jnp.concatenate([cos, cos, jnp.ones((rows, dh - r), F32)], axis=1)
    sa = jnp.concatenate([zh, sin, zeros], axis=1)
    sb = jnp.concatenate([-sin, zh, zeros], axis=1)
    return c, sa, sb


def _rope_slab(x, c, sa, sb):
    return x * c + pltpu.roll(x, 16, 1) * sa + pltpu.roll(x, LANES - 16, 1) * sb


def _proj_kernel(*refs, rope, residual, n_out, out_scale):
    x_ref, w_ref = refs[0], refs[1]
    pos = 2
    if rope:
        c_ref, sa_ref, sb_ref = refs[pos:pos + 3]
        pos += 3
    if residual:
        r_ref = refs[pos]
        pos += 1
    outs = refs[pos:pos + n_out]
    acc = jnp.dot(x_ref[...], w_ref[...], preferred_element_type=F32)
    if rope:
        c, sa, sb = c_ref[...], sa_ref[...], sb_ref[...]
        tn = acc.shape[1]
        acc = jnp.concatenate(
            [_rope_slab(acc[:, s * LANES:(s + 1) * LANES], c, sa, sb) for s in range(tn // LANES)], axis=1)
    if residual:
        acc = acc + r_ref[...]
    if out_scale != 1.0:
        acc = acc * out_scale
    for o in outs:
        o[...] = acc.astype(o.dtype)


def _proj(x, w, out_dtypes, rope=None, residual=None, out_scale=1.0, tm_target=1024, tn_target=None):
    m, k = x.shape
    n = w.shape[1]
    if tn_target is None:
        tn_target = 1024 if residual is None and all(dt == BF16 for dt in out_dtypes) else 512
    tm = _pick(m, tm_target)
    tn = _pick(n, tn_target, LANES)
    in_specs = [pl.BlockSpec((tm, k), lambda i, j: (i, 0)),
                pl.BlockSpec((k, tn), lambda i, j: (0, j))]
    args = [x, w]
    if rope is not None:
        in_specs += [pl.BlockSpec((tm, LANES), lambda i, j: (i, 0))] * 3
        args += list(rope)
    if residual is not None:
        in_specs.append(pl.BlockSpec((tm, tn), lambda i, j: (i, j)))
        args.append(residual)
    outs = pl.pallas_call(
        functools.partial(_proj_kernel, rope=rope is not None, residual=residual is not None,
                          n_out=len(out_dtypes), out_scale=out_scale),
        grid=(m // tm, n // tn),
        in_specs=in_specs,
        out_specs=[pl.BlockSpec((tm, tn), lambda i, j: (i, j)) for _ in out_dtypes],
        out_shape=[jax.ShapeDtypeStruct((m, n), dt) for dt in out_dtypes],
        compiler_params=_cparams(2),
        name="proj",
    )(*args)
    return outs


def _out_proj_kernel(xp_ref, xs_ref, w_ref, r_ref, o_ref, *, n_p):
    i = pl.program_id(0)

    @pl.when(i < n_p)
    def _():
        o_ref[...] = r_ref[...] + jnp.dot(xp_ref[...], w_ref[...], preferred_element_type=F32)

    @pl.when(i >= n_p)
    def _():
        o_ref[...] = r_ref[...] + jnp.dot(xs_ref[...], w_ref[...], preferred_element_type=F32)


def _out_proj(x_p, x_s, w, residual, tm_target=1024, tn_target=512):
    (m_p, k), m_s = x_p.shape, x_s.shape[0]
    n = w.shape[1]
    tm = _pick(math.gcd(m_p, m_s), tm_target)
    tn = _pick(n, tn_target, LANES)
    n_p = m_p // tm
    return pl.pallas_call(
        functools.partial(_out_proj_kernel, n_p=n_p),
        grid=((m_p + m_s) // tm, n // tn),
        in_specs=[pl.BlockSpec((tm, k), lambda i, j: (jnp.minimum(i, n_p - 1), 0)),
                  pl.BlockSpec((tm, k), lambda i, j: (jnp.maximum(i - n_p, 0), 0)),
                  pl.BlockSpec((k, tn), lambda i, j: (0, j)),
                  pl.BlockSpec((tm, tn), lambda i, j: (i, j))],
        out_specs=pl.BlockSpec((tm, tn), lambda i, j: (i, j)),
        out_shape=jax.ShapeDtypeStruct((m_p + m_s, n), F32),
        compiler_params=_cparams(2),
        name="out_proj",
    )(x_p, x_s, w, residual)


FFN_DOWN_CHUNKS = 4


def _ffn_kernel(*refs, next_norm):
    if next_norm:
        h_ref, g_ref, wg_ref, wu_ref, wd_ref, gn_ref, o_ref, xo_ref, xn_ref = refs
    else:
        h_ref, g_ref, wg_ref, wu_ref, wd_ref, o_ref, xn_ref = refs
    j = pl.program_id(1)

    @pl.when(j == 0)
    def _():
        x = h_ref[...]
        y = x * lax.rsqrt(jnp.mean(x * x, axis=-1, keepdims=True) + EPS)
        xn_ref[...] = (y * g_ref[...]).astype(BF16)
        o_ref[...] = x

    xn = xn_ref[...]
    gate = jnp.dot(xn, wg_ref[...], preferred_element_type=F32)
    up = jnp.dot(xn, wu_ref[...], preferred_element_type=F32)
    act = (gate * (1.0 / (1.0 + jnp.exp(-gate))) * up).astype(BF16)
    cw = o_ref.shape[1] // FFN_DOWN_CHUNKS
    for c in range(FFN_DOWN_CHUNKS):
        cs = slice(c * cw, (c + 1) * cw)
        o_ref[:, cs] += jnp.dot(act, wd_ref[:, cs], preferred_element_type=F32)

    if next_norm:
        @pl.when(j == pl.num_programs(1) - 1)
        def _():
            x = o_ref[...]
            y = x * lax.rsqrt(jnp.mean(x * x, axis=-1, keepdims=True) + EPS)
            xo_ref[...] = (y * gn_ref[...]).astype(xo_ref.dtype)


def _ffn(h, g, w_gu, w_down, layer, g_next=None, tm_target=512, tf=256):
    m, d = h.shape
    f = w_down.shape[1]
    tm = _pick(m, tm_target)
    nf = f // tf
    row = pl.BlockSpec((tm, d), lambda i, j: (i, 0))
    gain = pl.BlockSpec((1, d), lambda i, j: (0, 0))
    in_specs = [pl.BlockSpec((tm, d), lambda i, j: (i, 0), pipeline_mode=pl.Buffered(1)),
                gain,
                pl.BlockSpec((None, d, tf), lambda i, j: (layer, 0, j)),
                pl.BlockSpec((None, d, tf), lambda i, j: (layer, 0, j + nf)),
                pl.BlockSpec((None, tf, d), lambda i, j: (layer, j, 0))]
    args = [h, g.reshape(1, d).astype(F32), w_gu, w_gu, w_down]
    out_specs, out_shape = row, jax.ShapeDtypeStruct((m, d), F32)
    if g_next is not None:
        in_specs.append(gain)
        args.append(g_next.reshape(1, d).astype(F32))
        out_specs, out_shape = [row, row], [out_shape, jax.ShapeDtypeStruct((m, d), BF16)]
    return pl.pallas_call(
        functools.partial(_ffn_kernel, next_norm=g_next is not None),
        grid=(m // tm, nf),
        in_specs=in_specs,
        out_specs=out_specs,
        out_shape=out_shape,
        scratch_shapes=[pltpu.VMEM((tm, d), BF16)],
        compiler_params=_cparams(2, FFN_VMEM_LIMIT),
        name="ffn",
    )(*args)


_DN_T = (((1,), (1,)), ((), ()))


BAND_HEADS_PER_STEP = 4


def _band_prompt_kernel(q_ref, kp_ref, kc_ref, vp_ref, vc_ref, bias_ref, o_ref, bias_scr, *, tq, dh):
    t = pl.program_id(2)
    hp = bias_ref.shape[0]

    @pl.when((pl.program_id(1) == 0) & (t == 0))
    def _():
        bias_scr[...] = jnp.full(bias_scr.shape, NEG, F32)
        width = bias_ref.shape[3]
        for u in range(hp):
            for cc in range(tq // CHUNK):
                c0 = (cc // 2) * LANES
                bias_scr[u, cc * CHUNK:(cc + 1) * CHUNK, c0:c0 + width] = bias_ref[u, cc % 2]

    col = lax.broadcasted_iota(jnp.int32, (1, 2 * tq), 1)
    before_start = jnp.where((col < tq) & (t == 0), NEG, 0.0)
    for u in range(hp):
        hs = slice(u * dh, (u + 1) * dh)
        k = jnp.concatenate([kp_ref[:, hs], kc_ref[:, hs]], axis=0)
        v = jnp.concatenate([vp_ref[:, hs], vc_ref[:, hs]], axis=0)
        s = lax.dot_general(q_ref[:, hs], k, _DN_T, preferred_element_type=F32) + bias_scr[u] + before_start
        m = jnp.max(s, axis=1, keepdims=True)
        p = jnp.exp2(s - m)
        l = jnp.sum(p, axis=1, keepdims=True)
        o = jnp.dot(p.astype(BF16), v, preferred_element_type=F32) / l
        o_ref[:, hs] = o.astype(o_ref.dtype)


def _band_prompt(q, kv, chunk_bias, *, batch, seq, heads, dh, row0):
    assert 2 * CHUNK == LANES
    hp = BAND_HEADS_PER_STEP
    tq = chunk_bias.shape[2] - CHUNK
    nt = seq // tq
    rb0 = row0 // tq
    nh = heads // hp
    pad = jnp.full((heads, CHUNK, CHUNK), NEG, F32)
    bias2 = jnp.stack([jnp.concatenate([chunk_bias, pad], axis=2), jnp.concatenate([pad, chunk_bias], axis=2)],
                      axis=1)
    cur = lambda h, b, t: (rb0 + b * nt + t, h)
    prev = lambda h, b, t: (rb0 + b * nt + jnp.maximum(t - 1, 0), h)
    cur_v = lambda h, b, t: (rb0 + b * nt + t, nh + h)
    prev_v = lambda h, b, t: (rb0 + b * nt + jnp.maximum(t - 1, 0), nh + h)
    blk = (tq, hp * dh)
    return pl.pallas_call(
        functools.partial(_band_prompt_kernel, tq=tq, dh=dh),
        grid=(nh, batch, nt),
        in_specs=[pl.BlockSpec(blk, cur),
                  pl.BlockSpec(blk, prev), pl.BlockSpec(blk, cur),
                  pl.BlockSpec(blk, prev_v), pl.BlockSpec(blk, cur_v),
                  pl.BlockSpec((hp, 2, CHUNK, tq + 2 * CHUNK), lambda h, b, t: (h, 0, 0, 0))],
        out_specs=pl.BlockSpec(blk, lambda h, b, t: (b * nt + t, h)),
        out_shape=jax.ShapeDtypeStruct((batch * seq, heads * dh), BF16),
        scratch_shapes=[pltpu.VMEM((hp, tq, 2 * tq), F32)],
        compiler_params=_cparams(3),
        name="band_prompt",
    )(q, kv, kv, kv, kv, bias2)


def _band_sample_kernel(q_ref, kc_ref, vc_ref, kn_ref, vn_ref, bias_ref, o_ref, *, w, dh):
    for u in range(bias_ref.shape[0]):
        hs = slice(u * dh, (u + 1) * dh)
        q = q_ref[:, hs]
        kc = kc_ref[0, :, hs].astype(BF16)
        vc = vc_ref[0, :, hs].astype(BF16)
        bias = bias_ref[u]
        sc = lax.dot_general(q, kc, _DN_T, preferred_element_type=F32) + bias[:, :w]
        sn = lax.dot_general(q, kn_ref[:, hs], _DN_T, preferred_element_type=F32) + bias[:, w:]
        m = jnp.maximum(jnp.max(sc, axis=1, keepdims=True), jnp.max(sn, axis=1, keepdims=True))
        pc = jnp.exp2(sc - m)
        pn = jnp.exp2(sn - m)
        l = jnp.sum(pc, axis=1, keepdims=True) + jnp.sum(pn, axis=1, keepdims=True)
        o = (jnp.dot(pc.astype(BF16), vc, preferred_element_type=F32)
             + jnp.dot(pn.astype(BF16), vn_ref[:, hs], preferred_element_type=F32)) / l
        o_ref[:, hs] = o.astype(o_ref.dtype)


def _band_sample(q, kv, cache_k, cache_v, bias, *, batch, ts, heads, dh, row0):
    hp = BAND_HEADS_PER_STEP
    w = cache_k.shape[1]
    rb0 = row0 // ts
    nh = heads // hp
    return pl.pallas_call(
        functools.partial(_band_sample_kernel, w=w, dh=dh),
        grid=(nh, batch),
        in_specs=[pl.BlockSpec((ts, hp * dh), lambda h, b: (rb0 + b, h)),
                  pl.BlockSpec((1, w, hp * dh), lambda h, b: (b, 0, h)),
                  pl.BlockSpec((1, w, hp * dh), lambda h, b: (b, 0, h)),
                  pl.BlockSpec((ts, hp * dh), lambda h, b: (rb0 + b, h)),
                  pl.BlockSpec((ts, hp * dh), lambda h, b: (rb0 + b, nh + h)),
                  pl.BlockSpec((hp, ts, w + ts), lambda h, b: (h, 0, 0))],
        out_specs=pl.BlockSpec((ts, hp * dh), lambda h, b: (b, h)),
        out_shape=jax.ShapeDtypeStruct((batch * ts, heads * dh), BF16),
        compiler_params=_cparams(2),
        name="band_sample",
    )(q, cache_k, cache_v, kv, kv, bias)


def _band_bias(rel_bias, q0, nq, k0, nk):
    diag = np.arange(nq + nk - 1) - (nq - 1)
    idx = np.clip(q0 - k0 - diag, -C_REL_CLIP, C_REL_CLIP) + C_REL_CLIP
    r = rel_bias[idx].T.astype(F32)
    bias = jnp.stack([r[:, nq - 1 - i:nq - 1 - i + nk] for i in range(nq)], axis=1)
    qpos = q0 + np.arange(nq)[:, None]
    kpos = k0 + np.arange(nk)[None, :]
    qch, kch = qpos // CHUNK, kpos // CHUNK
    ok = (kpos >= 0) & (kch <= qch) & (qch - kch <= C_LEFT_CHUNKS)
    return jnp.where(ok[None], bias, NEG)


def _diff_lambda(lq1_ref, lk1_ref, lq2_ref, lk2_ref, lam_init):
    s1 = jnp.sum(lq1_ref[...] * lk1_ref[...], axis=1, keepdims=True)
    s2 = jnp.sum(lq2_ref[...] * lk2_ref[...], axis=1, keepdims=True)
    return jnp.exp(s1) - jnp.exp(s2) + lam_init


def _stack_groups(q_ref, c, group, dh):
    return jnp.concatenate([q_ref[:, (g * 2 + c) * dh:(g * 2 + c + 1) * dh] for g in range(group)], axis=0)


def _diff_finish(o, g_ref, o_ref, *, group, tq, out_scale):
    y = o * lax.rsqrt(jnp.mean(o * o, axis=-1, keepdims=True) + EPS) * g_ref[...] * out_scale
    y = y.astype(o_ref.dtype)
    e = y.shape[1]
    for g in range(group):
        o_ref[:, g * e:(g + 1) * e] = y[g * tq:(g + 1) * tq]


def _diff_prompt_kernel(q_ref, k_ref, vt_ref, lq1_ref, lk1_ref, lq2_ref, lk2_ref, g_ref, o_ref,
                        acc_scr, s_scr, m_scr, l_scr, *, tq, kb, group, dh, lam_init):
    t = pl.program_id(2)
    rows = group * tq
    e = 2 * dh
    lam = _diff_lambda(lq1_ref, lk1_ref, lq2_ref, lk2_ref, lam_init)
    acc_scr[...] = jnp.zeros(acc_scr.shape, F32)
    m_scr[...] = jnp.full(m_scr.shape, NEG, F32)
    l_scr[...] = jnp.zeros(l_scr.shape, F32)
    n_blocks = ((t + 1) * tq + kb - 1) // kb
    n_free = n_blocks - 1

    def scores(i, slot):
        ks = pl.multiple_of(i * kb, kb)
        kblk = k_ref[pl.ds(ks, kb), :]
        for c in range(2):
            qc = _stack_groups(q_ref, c, group, dh)
            s_scr[slot, c] = lax.dot_general(kblk[:, c * dh:(c + 1) * dh], qc, _DN_T,
                                             preferred_element_type=F32)

    def accumulate(i, slot, masked):
        ks = pl.multiple_of(i * kb, kb)
        vtb = vt_ref[0, :, pl.ds(ks, kb)]
        if masked:
            qi = lax.broadcasted_iota(jnp.int32, (kb, rows), 1) % tq
            qend = ((t * tq + qi) // CHUNK + 1) * CHUNK
            kpos = ks + lax.broadcasted_iota(jnp.int32, (kb, rows), 0)
            bias = jnp.where(kpos < qend, 0.0, NEG)
        for c in range(2):
            s = s_scr[slot, c]
            if masked:
                s = s + bias
            m = m_scr[c]
            m_new = jnp.maximum(m, jnp.max(s, axis=0, keepdims=True))
            alpha = jnp.exp2(m - m_new)
            p = jnp.exp2(s - m_new)
            l_scr[c] = alpha * l_scr[c] + jnp.sum(p, axis=0, keepdims=True)
            m_scr[c] = m_new
            acc_scr[c] = alpha * acc_scr[c] + jnp.dot(vtb, p.astype(BF16), preferred_element_type=F32)

    scores(n_free, 0)

    @pl.when(n_free == 0)
    def _():
        accumulate(n_free, 0, True)

    @pl.when(n_free > 0)
    def _():
        scores(0, 1)
        accumulate(n_free, 0, True)

        def pair(j, carry):
            i = 2 * j
            scores(i + 1, 0)
            accumulate(i, 1, False)
            scores(jnp.minimum(i + 2, n_free - 1), 1)
            accumulate(i + 1, 0, False)
            return carry

        lax.fori_loop(0, n_free // 2, pair, 0)

        @pl.when(n_free % 2 == 1)
        def _():
            accumulate(n_free - 1, 1, False)

    o = acc_scr[0] / l_scr[0] - lam * (acc_scr[1] / l_scr[1])
    y = o * lax.rsqrt(jnp.mean(o * o, axis=0, keepdims=True) + EPS) * (1.0 - lam_init)
    for g in range(group):
        o_ref[:, g * e:(g + 1) * e] = (y[:, g * tq:(g + 1) * tq].T * g_ref[...]).astype(o_ref.dtype)


def _lam_specs(n_grid):
    zero = (lambda *_: (0, 0))
    return [pl.BlockSpec((1, LANES), zero)] * 4


def _diff_prompt(q, k, vt, lams, subln_g, *, batch, seq, kvh, group, dh, lam_init, tq=LANES, kb=512):
    kb = _pick(seq, kb, LANES)
    nt = seq // tq
    rows = group * tq
    e = 2 * dh
    return pl.pallas_call(
        functools.partial(_diff_prompt_kernel, tq=tq, kb=kb, group=group, dh=dh, lam_init=lam_init),
        grid=(batch, kvh, nt),
        in_specs=[pl.BlockSpec((tq, group * e), lambda b, h, t: (b * nt + t, h)),
                  pl.BlockSpec((seq, e), lambda b, h, t: (b, h)),
                  pl.BlockSpec((1, e, seq), lambda b, h, t: (b, h, 0))]
                 + _lam_specs(3) + [pl.BlockSpec((1, e), lambda b, h, t: (0, 0))],
        out_specs=pl.BlockSpec((tq, group * e), lambda b, h, t: (b * nt + t, h)),
        out_shape=jax.ShapeDtypeStruct((batch * seq, kvh * group * e), BF16),
        scratch_shapes=[pltpu.VMEM((2, e, rows), F32), pltpu.VMEM((2, 2, kb, rows), F32),
                        pltpu.VMEM((2, 1, rows), F32), pltpu.VMEM((2, 1, rows), F32)],
        compiler_params=_cparams(3),
        name="diff_prompt",
    )(q, k, vt, *lams, subln_g.reshape(1, e))


def _diff_sample_kernel(q_ref, kc_ref, vc_ref, kn_ref, vn_ref, lq1_ref, lk1_ref, lq2_ref, lk2_ref, g_ref, o_ref,
                        *, ts, group, dh, lam_init):
    lam = _diff_lambda(lq1_ref, lk1_ref, lq2_ref, lk2_ref, lam_init)
    kc = kc_ref[0].astype(BF16)
    vc = vc_ref[0].astype(BF16)
    kn = kn_ref[...]
    a_c, a_n = [], []
    for c in range(2):
        qc = _stack_groups(q_ref, c, group, dh)
        sc = lax.dot_general(qc, kc[:, c * dh:(c + 1) * dh], _DN_T, preferred_element_type=F32)
        sn = lax.dot_general(qc, kn[:, c * dh:(c + 1) * dh], _DN_T, preferred_element_type=F32)
        m = jnp.maximum(jnp.max(sc, axis=1, keepdims=True), jnp.max(sn, axis=1, keepdims=True))
        pc = jnp.exp2(sc - m)
        pn = jnp.exp2(sn - m)
        l = jnp.sum(pc, axis=1, keepdims=True) + jnp.sum(pn, axis=1, keepdims=True)
        a_c.append(pc / l)
        a_n.append(pn / l)
    ac = (a_c[0] - lam * a_c[1]).astype(BF16)
    an = (a_n[0] - lam * a_n[1]).astype(BF16)
    o = (jnp.dot(ac, vc, preferred_element_type=F32) + jnp.dot(an, vn_ref[...], preferred_element_type=F32))
    _diff_finish(o, g_ref, o_ref, group=group, tq=ts, out_scale=1.0 - lam_init)


def _diff_sample(q, k, v, cache_k, cache_v, lams, subln_g, *, batch, ts, kvh, group, dh, lam_init, row0):
    past = cache_k.shape[1]
    rb0 = row0 // ts
    e = 2 * dh
    return pl.pallas_call(
        functools.partial(_diff_sample_kernel, ts=ts, group=group, dh=dh, lam_init=lam_init),
        grid=(batch, kvh),
        in_specs=[pl.BlockSpec((ts, group * e), lambda b, h: (rb0 + b, h)),
                  pl.BlockSpec((1, past, e), lambda b, h: (b, 0, h)),
                  pl.BlockSpec((1, past, e), lambda b, h: (b, 0, h)),
                  pl.BlockSpec((ts, e), lambda b, h: (rb0 + b, h)),
                  pl.BlockSpec((ts, e), lambda b, h: (rb0 + b, h))]
                 + _lam_specs(2) + [pl.BlockSpec((1, e), lambda b, h: (0, 0))],
        out_specs=pl.BlockSpec((ts, group * e), lambda b, h: (b, h)),
        out_shape=jax.ShapeDtypeStruct((batch * ts, kvh * group * e), BF16),
        compiler_params=_cparams(2),
        name="diff_sample",
    )(q, cache_k, cache_v, k, v, *lams, subln_g)


def _dsa_kernel(iq_ref, ik_ref, iwt_ref, q_ref, k_ref, vt_ref, o_ref, key_scr, acc_scr,
                *, tq, kb, causal, s_real, n_sel, n_idx, kvh, group, dh, iscale):
    t = pl.program_id(1)
    s_pad = ik_ref.shape[0]
    lane = lax.broadcasted_iota(jnp.int32, (1, tq), 1)
    if causal:
        qend = ((t * tq + lane) // CHUNK + 1) * CHUNK
        n_blocks = ((t + 1) * tq + kb - 1) // kb
    else:
        qend = jnp.full((1, tq), s_real, jnp.int32)
        n_blocks = s_pad // kb

    w = iwt_ref[...] * iscale

    def score_body(i, carry):
        ks = pl.multiple_of(i * kb, kb)
        ikb = ik_ref[pl.ds(ks, kb), :]
        sc = jnp.zeros((kb, tq), F32)
        for n in range(n_idx):
            logit = lax.dot_general(ikb, iq_ref[:, n * dh:(n + 1) * dh], _DN_T, preferred_element_type=F32)
            sc = sc + w[n:n + 1, :] * jnp.maximum(logit, 0.0)
        bits = lax.bitcast_convert_type(sc, jnp.int32)
        key = jnp.where(bits < 0, bits ^ 0x7FFFFFFF, bits)
        kpos = ks + lax.broadcasted_iota(jnp.int32, (kb, tq), 0)
        key_scr[pl.ds(ks, kb), :] = jnp.where(kpos < qend, key, INT_MIN)
        return carry

    lax.fori_loop(0, n_blocks, score_body, 0)

    def count_ge(cand):
        def body(i, acc):
            ks = pl.multiple_of(i * kb, kb)
            ge = key_scr[pl.ds(ks, kb), :] >= cand
            return acc + jnp.sum(ge.astype(jnp.int32), axis=0, keepdims=True)
        return lax.fori_loop(0, n_blocks, body, jnp.zeros((1, tq), jnp.int32))

    zero = jnp.zeros((1, tq), jnp.int32)
    thr = jnp.where(count_ge(zero) >= n_sel, zero, jnp.full((1, tq), INT_MIN, jnp.int32))

    def bit_body(i, thr):
        cand = thr | lax.shift_left(jnp.int32(1), 30 - i)
        return jnp.where(count_ge(cand) >= n_sel, cand, thr)

    thr = lax.fori_loop(0, 31, bit_body, thr)
    thr = jnp.maximum(thr, INT_MIN + 1)

    def bias_body(i, carry):
        ks = pl.multiple_of(i * kb, kb)
        bias = jnp.where(key_scr[pl.ds(ks, kb), :] >= thr, 0.0, NEG)
        key_scr[pl.ds(ks, kb), :] = lax.bitcast_convert_type(bias, jnp.int32)
        return carry

    lax.fori_loop(0, n_blocks, bias_body, 0)

    cols = group * tq
    for h in range(kvh):
        acc_scr[...] = jnp.zeros(acc_scr.shape, F32)

        def kv_body(i, st, h=h):
            m, l = st
            ks = pl.multiple_of(i * kb, kb)
            kblk = k_ref[pl.ds(ks, kb), h * dh:(h + 1) * dh]
            qh = jnp.concatenate([q_ref[:, (h * group + g) * dh:(h * group + g + 1) * dh] for g in range(group)],
                                 axis=0)
            bias = lax.bitcast_convert_type(key_scr[pl.ds(ks, kb), :], F32)
            s = lax.dot_general(kblk, qh, _DN_T, preferred_element_type=F32)
            s = s + jnp.concatenate([bias] * group, axis=1)
            m_new = jnp.maximum(m, jnp.max(s, axis=0, keepdims=True))
            alpha = jnp.exp2(m - m_new)
            p = jnp.exp2(s - m_new)
            l = alpha * l + jnp.sum(p, axis=0, keepdims=True)
            vtb = vt_ref[0, h * dh:(h + 1) * dh, pl.ds(ks, kb)]
            acc_scr[...] = alpha * acc_scr[...] + jnp.dot(vtb, p.astype(BF16), preferred_element_type=F32)
            return m_new, l

        _, l = lax.fori_loop(0, n_blocks, kv_body, (jnp.full((1, cols), NEG, F32), jnp.zeros((1, cols), F32)))
        out = acc_scr[...] / l
        for g in range(group):
            c0 = (h * group + g) * dh
            o_ref[:, c0:c0 + dh] = out[:, g * tq:(g + 1) * tq].T.astype(o_ref.dtype)


def _dsa(iq, ik, iwt, q, k, vt, *, batch, tq_total, tq, kb, causal, s_real, n_sel):
    dh = ik.shape[1]
    s_pad = vt.shape[2]
    kvh = k.shape[1] // dh
    heads = q.shape[1] // dh
    n_idx = iq.shape[1] // dh
    nt = tq_total // tq
    cols = heads // kvh * tq
    return pl.pallas_call(
        functools.partial(_dsa_kernel, tq=tq, kb=kb, causal=causal, s_real=s_real, n_sel=n_sel, n_idx=n_idx,
                          kvh=kvh, group=heads // kvh, dh=dh, iscale=(n_idx * dh) ** -0.5),
        grid=(batch, nt),
        in_specs=[pl.BlockSpec((tq, n_idx * dh), lambda b, t: (b * nt + t, 0)),
                  pl.BlockSpec((s_pad, dh), lambda b, t: (b, 0)),
                  pl.BlockSpec((n_idx, tq), lambda b, t: (0, b * nt + t)),
                  pl.BlockSpec((tq, heads * dh), lambda b, t: (b * nt + t, 0)),
                  pl.BlockSpec((s_pad, kvh * dh), lambda b, t: (b, 0)),
                  pl.BlockSpec((1, kvh * dh, s_pad), lambda b, t: (b, 0, 0))],
        out_specs=pl.BlockSpec((tq, heads * dh), lambda b, t: (b * nt + t, 0)),
        out_shape=jax.ShapeDtypeStruct((batch * tq_total, heads * dh), BF16),
        scratch_shapes=[pltpu.VMEM((s_pad, tq), jnp.int32), pltpu.VMEM((dh, cols), F32)],
        compiler_params=_cparams(2),
        name="dsa",
    )(iq, ik, iwt, q, k, vt)


def _mixer_a(xn, tabs, w_in, lams, subln_g, cache_k, cache_v, lam_init, dims):
    bp, tp, bs, ts, d, past = dims
    mp = bp * tp
    dh = d // (2 * A_HEADS)
    a_q = A_HEADS * 2 * dh
    a_kv = A_KV_HEADS * 2 * dh
    group = A_HEADS // A_KV_HEADS
    (q,) = _proj(xn, w_in[:, :a_q].astype(BF16), (BF16,), rope=tabs, out_scale=dh ** -0.5 * LOG2E)
    k32, k16 = _proj(xn, w_in[:, a_q:a_q + a_kv].astype(BF16), (F32, BF16), rope=tabs)
    v32, v16 = _proj(xn, w_in[:, a_q + a_kv:].astype(BF16), (F32, BF16))
    lams = tuple(x.reshape(1, dh).astype(F32) for x in lams)
    subln_g = subln_g.astype(F32)
    vt_p = jnp.swapaxes(v16[:mp].reshape(bp, tp, a_kv), 1, 2)
    o_p = _diff_prompt(q, k16, vt_p, lams, subln_g, batch=bp, seq=tp, kvh=A_KV_HEADS, group=group, dh=dh,
                       lam_init=lam_init)
    o_s = _diff_sample(q, k16, v16, cache_k.reshape(bs, past, a_kv), cache_v.reshape(bs, past, a_kv), lams,
                       subln_g.reshape(1, 2 * dh), batch=bs, ts=ts, kvh=A_KV_HEADS, group=group, dh=dh,
                       lam_init=lam_init, row0=mp)
    return (o_p, o_s), k32, v32


def _mixer_b(xn, tabs, w_in, cache_k, cache_v, cache_ik, dims):
    bp, tp, bs, ts, d, past = dims
    mp = bp * tp
    dh = d // B_HEADS
    o1 = B_HEADS * dh
    o2 = o1 + B_KV_HEADS * dh
    o3 = o2 + B_KV_HEADS * dh
    o4 = o3 + B_IDX_HEADS * B_IDX_DIM
    o5 = o4 + B_IDX_DIM
    kvw = B_KV_HEADS * dh
    (q,) = _proj(xn, w_in[:, :o1].astype(BF16), (BF16,), rope=tabs, out_scale=dh ** -0.5 * LOG2E)
    k32, k16 = _proj(xn, w_in[:, o1:o2].astype(BF16), (F32, BF16), rope=tabs)
    v32, v16 = _proj(xn, w_in[:, o2:o3].astype(BF16), (F32, BF16))
    (iq,) = _proj(xn, w_in[:, o3:o4].astype(BF16), (BF16,), rope=tabs)
    ik32, ik16 = _proj(xn, w_in[:, o4:o5].astype(BF16), (F32, BF16), rope=tabs)
    w_iw = jnp.pad(w_in[:, o5:], ((0, 0), (0, LANES - B_IDX_HEADS))).astype(BF16)
    (iw,) = _proj(xn, w_iw, (F32,))
    iwt = iw[:, :B_IDX_HEADS].T

    vt_p = jnp.swapaxes(v16[:mp].reshape(bp, tp, kvw), 1, 2)
    tq_p = _pick(tp, 256, LANES)
    o_p = _dsa(iq, ik16, iwt, q, k16, vt_p, batch=bp, tq_total=tp, tq=tq_p, kb=_pick(tp, 512, LANES),
               causal=True, s_real=tp, n_sel=min(B_TOPK_MAX, tp // 4))

    past = cache_k.shape[1]
    s_real = past + ts
    s_pad = -(-s_real // LANES) * LANES
    tq_s = -(-ts // LANES) * LANES

    def keys(cache, new):
        new = new[mp:].reshape(bs, ts, -1)
        kk = jnp.concatenate([cache.reshape(bs, past, -1).astype(BF16), new,
                              jnp.zeros((bs, s_pad - s_real, new.shape[-1]), BF16)], axis=1)
        return kk

    def queries(a):
        a = jnp.pad(a[mp:].reshape(bs, ts, -1), ((0, 0), (0, tq_s - ts), (0, 0)))
        return a.reshape(bs * tq_s, -1)

    kk = keys(cache_k, k16).reshape(bs * s_pad, kvw)
    ikk = keys(cache_ik, ik16).reshape(bs * s_pad, B_IDX_DIM)
    vt_s = jnp.swapaxes(keys(cache_v, v16), 1, 2)
    o_s = _dsa(queries(iq), ikk, queries(iw[:, :B_IDX_HEADS]).T, queries(q), kk, vt_s, batch=bs, tq_total=tq_s,
               tq=tq_s, kb=_pick(s_pad, 512, LANES), causal=False, s_real=s_real,
               n_sel=min(B_TOPK_MAX, s_real // 4))
    o_s = o_s.reshape(bs, tq_s, o1)[:, :ts].reshape(bs * ts, o1)
    return (o_p, o_s), k32, v32, ik32


def _mixer_c(xn, w_in, rel_bias, cache_k, cache_v, dims):
    bp, tp, bs, ts, d, past = dims
    mp = bp * tp
    dh = d // C_HEADS
    hw = C_HEADS * dh
    win = C_LEFT_CHUNKS * CHUNK
    (q,) = _proj(xn, w_in[:, :hw].astype(BF16), (BF16,), out_scale=dh ** -0.5 * LOG2E)
    kv32, kv16 = _proj(xn, w_in[:, hw:].astype(BF16), (F32, BF16))
    rel_bias = rel_bias.astype(F32) * LOG2E

    chunk_bias = _band_bias(rel_bias, win, CHUNK, 0, win + CHUNK)
    o_p = _band_prompt(q, kv16, chunk_bias, batch=bp, seq=tp, heads=C_HEADS, dh=dh, row0=0)

    w_c = cache_k.shape[1]
    o_s = _band_sample(q, kv16, cache_k.reshape(bs, w_c, hw), cache_v.reshape(bs, w_c, hw),
                       _band_bias(rel_bias, past, ts, past - w_c, w_c + ts),
                       batch=bs, ts=ts, heads=C_HEADS, dh=dh, row0=mp)
    return (o_p, o_s), kv32[:, :hw], kv32[:, hw:]


def kernel(x_prompt, x_sample, cache_a_k, cache_a_v, cache_b_k, cache_b_v, cache_b_idx_k, cache_c_k, cache_c_v,
           norm_mix_g, norm_ffn_g, norm_out_g, a_w_in, a_w_out, a_lam_q1, a_lam_k1, a_lam_q2, a_lam_k2,
           a_subln_g, b_w_in, b_w_out, c_w_in, c_w_out, c_rel_bias, ffn_w_gu, ffn_w_down):
    bp, tp, d = x_prompt.shape
    bs, ts, _ = x_sample.shape
    past = cache_a_k.shape[2]
    depth = norm_mix_g.shape[0]
    assert ts <= CHUNK and past % CHUNK == 0 and tp % CHUNK == 0
    mp, ms = bp * tp, bs * ts
    dims = (bp, tp, bs, ts, d, past)
    i32 = jnp.int32

    h, xn = _join_norm(x_prompt.reshape(mp, d), x_sample.reshape(ms, d), norm_mix_g[0])
    pos =jnp.concatenate([jnp.tile(jnp.arange(tp, dtype=i32), bp), jnp.tile(past + jnp.arange(ts, dtype=i32), bs)])
    tabs = _rope_tables(pos)
    w_gu16 = ffn_w_gu.astype(BF16)
    w_down16 = ffn_w_down.astype(BF16)

    def split(a, tail):
        return a[:mp].reshape((bp, tp) + tail), a[mp:].reshape((bs, ts) + tail)

    st = {name: [] for name in ("a_k", "a_v", "b_k", "b_v", "b_ik", "c_k", "c_v")}
    for i in range(depth):
        j = i // N_MIXERS
        kind = i % N_MIXERS
        if kind == 0:
            lam_init = 0.8 - 0.6 * math.exp(-0.3 * i)
            o, k, v = _mixer_a(xn, tabs, a_w_in[j], (a_lam_q1[j], a_lam_k1[j], a_lam_q2[j], a_lam_k2[j]),
                               a_subln_g[j], cache_a_k[j], cache_a_v[j], lam_init, dims)
            tail = cache_a_k.shape[3:]
            st["a_k"].append(split(k, tail))
            st["a_v"].append(split(v, tail))
            w_out = a_w_out[j]
        elif kind == 1:
            o, k, v, ik = _mixer_b(xn, tabs, b_w_in[j], cache_b_k[j], cache_b_v[j], cache_b_idx_k[j], dims)
            tail = cache_b_k.shape[3:]
            st["b_k"].append(split(k, tail))
            st["b_v"].append(split(v, tail))
            st["b_ik"].append(split(ik, cache_b_idx_k.shape[3:]))
            w_out = b_w_out[j]
        else:
            o, k, v = _mixer_c(xn, c_w_in[j], c_rel_bias[j], cache_c_k[j], cache_c_v[j], dims)
            tail = cache_c_k.shape[3:]
            keep = min(C_LEFT_CHUNKS * CHUNK, tp)
            for name, a, cache in (("c_k", k, cache_c_k[j]), ("c_v", v, cache_c_v[j])):
                a_p, a_s = split(a, tail)
                kk_s = jnp.concatenate([cache, a_s], axis=1)
                st[name].append((a_p[:, tp - keep:], kk_s[:, kk_s.shape[1] - cache.shape[1]:]))
            w_out = c_w_out[j]
        h = _out_proj(o[0], o[1], w_out.astype(BF16), h)
        if i + 1 < depth:
            h, xn = _ffn(h, norm_ffn_g[i], w_gu16, w_down16, i, g_next=norm_mix_g[i + 1])
        else:
            h = _ffn(h, norm_ffn_g[i], w_gu16, w_down16, i)

    y_p = _rmsnorm(h, norm_out_g, F32, 0, mp).reshape(bp, tp, d)
    y_s = _rmsnorm(h, norm_out_g, F32, mp, ms).reshape(bs, ts, d)
    stack = lambda name, g: jnp.stack([pair[g] for pair in st[name]], axis=0)
    names = ("a_k", "a_v", "b_k", "b_v", "b_ik", "c_k", "c_v")
    return (y_p, y_s) + tuple(stack(n, 0) for n in names) + tuple(stack(n, 1) for n in names)
```

```python
import functools
import math

import jax
import jax.numpy as jnp
import numpy as np
from jax import lax
from jax.experimental import pallas as pl
from jax.experimental.pallas import tpu as pltpu

F32 = jnp.float32
BF16 = jnp.bfloat16

CHUNK = 64
N_MIXERS = 3
EPS = 1e-6
ROPE_THETA = 500000.0
NEG = -1e30
A_HEADS = 16
A_KV_HEADS = 4
B_HEADS = 32
B_KV_HEADS = 4
B_IDX_HEADS = 32
B_IDX_DIM = 128
B_TOPK_MAX = 256
C_HEADS = 32
C_LEFT_CHUNKS = 8
C_REL_CLIP = 256

LANES = 128
VMEM_LIMIT = 56 * 1024 * 1024
FFN_VMEM_LIMIT = 58 * 1024 * 1024
INT_MIN = -2 ** 31
LOG2E = math.log2(math.e)


def _cparams(n_grid, vmem_limit=VMEM_LIMIT):
    return pltpu.CompilerParams(dimension_semantics=("arbitrary",) * n_grid,
                                vmem_limit_bytes=vmem_limit)


def _pick(n, target, mult=16):
    for t in range(min(n, target), 0, -1):
        if n % t == 0 and t % mult == 0:
            return t
    return n


def _rmsnorm_kernel(x_ref, g_ref, o_ref):
    x = x_ref[...]
    y = x * lax.rsqrt(jnp.mean(x * x, axis=-1, keepdims=True) + EPS)
    o_ref[...] = (y * g_ref[...]).astype(o_ref.dtype)


def _rmsnorm(x, g, out_dtype, row0=0, rows=None):
    d = x.shape[1]
    m = x.shape[0] if rows is None else rows
    tm = _pick(math.gcd(m, row0) if row0 else m, 512)
    b0 = row0 // tm
    return pl.pallas_call(
        _rmsnorm_kernel,
        grid=(m // tm,),
        in_specs=[pl.BlockSpec((tm, d), lambda i: (b0 + i, 0)),
                  pl.BlockSpec((1, d), lambda i: (0, 0))],
        out_specs=pl.BlockSpec((tm, d), lambda i: (i, 0)),
        out_shape=jax.ShapeDtypeStruct((m, d), out_dtype),
        compiler_params=_cparams(1),
        name="rmsnorm",
    )(x, g.reshape(1, d).astype(F32))


def _join_norm_kernel(xp_ref, xs_ref, g_ref, h_ref, xn_ref, *, n_p):
    i = pl.program_id(0)

    def emit(x_ref):
        x = x_ref[...]
        h_ref[...] = x
        y = x * lax.rsqrt(jnp.mean(x * x, axis=-1, keepdims=True) + EPS)
        xn_ref[...] = (y * g_ref[...]).astype(xn_ref.dtype)

    @pl.when(i < n_p)
    def _():
        emit(xp_ref)

    @pl.when(i >= n_p)
    def _():
        emit(xs_ref)


def _join_norm(x_p, x_s, g):
    (m_p, d), m_s = x_p.shape, x_s.shape[0]
    tm = _pick(math.gcd(m_p, m_s), 256)
    n_p = m_p // tm
    row = pl.BlockSpec((tm, d), lambda i: (i, 0))
    return pl.pallas_call(
        functools.partial(_join_norm_kernel, n_p=n_p),
        grid=((m_p + m_s) // tm,),
        in_specs=[pl.BlockSpec((tm, d), lambda i: (jnp.minimum(i, n_p - 1), 0)),
                  pl.BlockSpec((tm, d), lambda i: (jnp.maximum(i - n_p, 0), 0)),
                  pl.BlockSpec((1, d), lambda i: (0, 0))],
        out_specs=[row, row],
        out_shape=[jax.ShapeDtypeStruct((m_p + m_s, d), F32), jax.ShapeDtypeStruct((m_p + m_s, d), BF16)],
        compiler_params=_cparams(1),
        name="join_norm",
    )(x_p, x_s, g.reshape(1, d).astype(F32))


def _rope_tables(pos, dh=LANES):
    r = dh // 4
    half = r // 2
    inv = ROPE_THETA ** (-2.0 * jnp.arange(half, dtype=F32) / r)
    ang = pos.astype(F32)[:, None] * inv[None, :]
    cos, sin = jnp.cos(ang), jnp.sin(ang)
    rows = pos.shape[0]
    zeros = jnp.zeros((rows, dh - r), F32)
    zh = jnp.zeros((rows, half), F32)
    c = jnp.concatenate([cos, cos, jnp.ones((rows, dh - r), F32)], axis=1)
    sa = jnp.concatenate([zh, sin, zeros], axis=1)
    sb = jnp.concatenate([-sin, zh, zeros], axis=1)
    return c, sa, sb


def _rope_slab(x, c, sa, sb):
    return x * c + pltpu.roll(x, 16, 1) * sa + pltpu.roll(x, LANES - 16, 1) * sb


def _proj_kernel(*refs, rope, residual, n_out, out_scale):
    x_ref, w_ref = refs[0], refs[1]
    pos = 2
    if rope:
        c_ref, sa_ref, sb_ref = refs[pos:pos + 3]
        pos += 3
    if residual:
        r_ref = refs[pos]
        pos += 1
    outs = refs[pos:pos + n_out]
    acc = jnp.dot(x_ref[...], w_ref[...], preferred_element_type=F32)
    if rope:
        c, sa, sb = c_ref[...], sa_ref[...], sb_ref[...]
        tn = acc.shape[1]
        acc = jnp.concatenate(
            [_rope_slab(acc[:, s * LANES:(s + 1) * LANES], c, sa, sb) for s in range(tn // LANES)], axis=1)
    if residual:
        acc = acc + r_ref[...]
    if out_scale != 1.0:
        acc = acc * out_scale
    for o in outs:
        o[...] = acc.astype(o.dtype)


def _proj(x, w, out_dtypes, rope=None, residual=None, out_scale=1.0, tm_target=1024, tn_target=None):
    m, k = x.shape
    n = w.shape[1]
    if tn_target is None:
        tn_target = 1024 if residual is None and all(dt == BF16 for dt in out_dtypes) else 512
    tm = _pick(m, tm_target)
    tn = _pick(n, tn_target, LANES)
    in_specs = [pl.BlockSpec((tm, k), lambda i, j: (i, 0)),
                pl.BlockSpec((k, tn), lambda i, j: (0, j))]
    args = [x, w]
    if rope is not None:
        in_specs += [pl.BlockSpec((tm, LANES), lambda i, j: (i, 0))] * 3
        args += list(rope)
    if residual is not None:
        in_specs.append(pl.BlockSpec((tm, tn), lambda i, j: (i, j)))
        args.append(residual)
    outs = pl.pallas_call(
        functools.partial(_proj_kernel, rope=rope is not None, residual=residual is not None,
                          n_out=len(out_dtypes), out_scale=out_scale),
        grid=(m // tm, n // tn),
        in_specs=in_specs,
        out_specs=[pl.BlockSpec((tm, tn), lambda i, j: (i, j)) for _ in out_dtypes],
        out_shape=[jax.ShapeDtypeStruct((m, n), dt) for dt in out_dtypes],
        compiler_params=_cparams(2),
        name="proj",
    )(*args)
    return outs


def _out_proj_kernel(xp_ref, xs_ref, w_ref, r_ref, o_ref, *, n_p):
    i = pl.program_id(0)

    @pl.when(i < n_p)
    def _():
        o_ref[...] = r_ref[...] + jnp.dot(xp_ref[...], w_ref[...], preferred_element_type=F32)

    @pl.when(i >= n_p)
    def _():
        o_ref[...] = r_ref[...] + jnp.dot(xs_ref[...], w_ref[...], preferred_element_type=F32)


def _out_proj(x_p, x_s, w, residual, tm_target=1024, tn_target=512):
    (m_p, k), m_s = x_p.shape, x_s.shape[0]
    n = w.shape[1]
    tm = _pick(math.gcd(m_p, m_s), tm_target)
    tn = _pick(n, tn_target, LANES)
    n_p = m_p // tm
    return pl.pallas_call(
        functools.partial(_out_proj_kernel, n_p=n_p),
        grid=((m_p + m_s) // tm, n // tn),
        in_specs=[pl.BlockSpec((tm, k), lambda i, j: (jnp.minimum(i, n_p - 1), 0)),
                  pl.BlockSpec((tm, k), lambda i, j: (jnp.maximum(i - n_p, 0), 0)),
                  pl.BlockSpec((k, tn), lambda i, j: (0, j)),
                  pl.BlockSpec((tm, tn), lambda i, j: (i, j))],
        out_specs=pl.BlockSpec((tm, tn), lambda i, j: (i, j)),
        out_shape=jax.ShapeDtypeStruct((m_p + m_s, n), F32),
        compiler_params=_cparams(2),
        name="out_proj",
    )(x_p, x_s, w, residual)


FFN_DOWN_CHUNKS = 4


def _ffn_kernel(*refs, next_norm):
    if next_norm:
        h_ref, g_ref, wg_ref, wu_ref, wd_ref, gn_ref, o_ref, xo_ref, xn_ref, act_ref = refs
    else:
        h_ref, g_ref, wg_ref, wu_ref, wd_ref, o_ref, xn_ref, act_ref = refs
    j = pl.program_id(1)
    last = pl.num_programs(1) - 1

    def activations():
        xn = xn_ref[...]
        gate = jnp.dot(xn, wg_ref[...], preferred_element_type=F32)
        up = jnp.dot(xn, wu_ref[...], preferred_element_type=F32)
        act_ref[j % 2] = (gate * (1.0 / (1.0 + jnp.exp(-gate))) * up).astype(BF16)

    def down():
        act = act_ref[(j - 1) % 2]
        cw = o_ref.shape[1] // FFN_DOWN_CHUNKS
        for c in range(FFN_DOWN_CHUNKS):
            cs = slice(c * cw, (c + 1) * cw)
            o_ref[:, cs] += jnp.dot(act, wd_ref[:, cs], preferred_element_type=F32)

    @pl.when(j == 0)
    def _():
        x = h_ref[...]
        y = x * lax.rsqrt(jnp.mean(x * x, axis=-1, keepdims=True) + EPS)
        xn_ref[...] = (y * g_ref[...]).astype(BF16)
        o_ref[...] = x
        activations()

    @pl.when((j > 0) & (j < last))
    def _():
        activations()
        down()

    @pl.when(j == last)
    def _():
        down()
        if next_norm:
            x = o_ref[...]
            y = x * lax.rsqrt(jnp.mean(x * x, axis=-1, keepdims=True) + EPS)
            xo_ref[...] = (y * gn_ref[...]).astype(xo_ref.dtype)


def _ffn(h, g, w_gu, w_down, layer, g_next=None, tm_target=512, tf=256):
    m, d = h.shape
    f = w_down.shape[1]
    tm = _pick(m, tm_target)
    nf = f // tf
    row = pl.BlockSpec((tm, d), lambda i, j: (i, 0))
    gain = pl.BlockSpec((1, d), lambda i, j: (0, 0))
    in_specs = [pl.BlockSpec((tm, d), lambda i, j: (i, 0), pipeline_mode=pl.Buffered(1)),
                gain,
                pl.BlockSpec((None, d, tf), lambda i, j: (layer, 0, jnp.minimum(j, nf - 1))),
                pl.BlockSpec((None, d, tf), lambda i, j: (layer, 0, jnp.minimum(j, nf - 1) + nf)),
                pl.BlockSpec((None, tf, d), lambda i, j: (layer, jnp.maximum(j - 1, 0), 0))]
    args = [h, g.reshape(1, d).astype(F32), w_gu, w_gu, w_down]
    out_specs, out_shape = row, jax.ShapeDtypeStruct((m, d), F32)
    if g_next is not None:
        in_specs.append(gain)
        args.append(g_next.reshape(1, d).astype(F32))
        out_specs, out_shape = [row, row], [out_shape, jax.ShapeDtypeStruct((m, d), BF16)]
    return pl.pallas_call(
        functools.partial(_ffn_kernel, next_norm=g_next is not None),
        grid=(m // tm, nf + 1),
        in_specs=in_specs,
        out_specs=out_specs,
        out_shape=out_shape,
        scratch_shapes=[pltpu.VMEM((tm, d), BF16), pltpu.VMEM((2, tm, tf), BF16)],
        compiler_params=_cparams(2, FFN_VMEM_LIMIT),
        name="ffn",
    )(*args)


_DN_T = (((1,), (1,)), ((), ()))


BAND_HEADS_PER_STEP = 4


def _band_prompt_kernel(q_ref, kp_ref, kc_ref, vp_ref, vc_ref, bias_ref, o_ref, bias_scr, *, tq, dh):
    t = pl.program_id(2)
    hp = bias_ref.shape[0]

    @pl.when((pl.program_id(1) == 0) & (t == 0))
    def _():
        bias_scr[...] = jnp.full(bias_scr.shape, NEG, F32)
        width = bias_ref.shape[3]
        for u in range(hp):
            for cc in range(tq // CHUNK):
                c0 = (cc // 2) * LANES
                bias_scr[u, cc * CHUNK:(cc + 1) * CHUNK, c0:c0 + width] = bias_ref[u, cc % 2]

    col = lax.broadcasted_iota(jnp.int32, (1, 2 * tq), 1)
    before_start = jnp.where((col < tq) & (t == 0), NEG, 0.0)
    for u in range(hp):
        hs = slice(u * dh, (u + 1) * dh)
        k = jnp.concatenate([kp_ref[:, hs], kc_ref[:, hs]], axis=0)
        v = jnp.concatenate([vp_ref[:, hs], vc_ref[:, hs]], axis=0)
        s = lax.dot_general(q_ref[:, hs], k, _DN_T, preferred_element_type=F32) + bias_scr[u] + before_start
        m = jnp.max(s, axis=1, keepdims=True)
        p = jnp.exp2(s - m)
        l = jnp.sum(p, axis=1, keepdims=True)
        o = jnp.dot(p.astype(BF16), v, preferred_element_type=F32) / l
        o_ref[:, hs] = o.astype(o_ref.dtype)


def _band_prompt(q, kv, chunk_bias, *, batch, seq, heads, dh, row0):
    assert 2 * CHUNK == LANES
    hp = BAND_HEADS_PER_STEP
    tq = chunk_bias.shape[2] - CHUNK
    nt = seq // tq
    rb0 = row0 // tq
    nh = heads // hp
    pad = jnp.full((heads, CHUNK, CHUNK), NEG, F32)
    bias2 = jnp.stack([jnp.concatenate([chunk_bias, pad], axis=2), jnp.concatenate([pad, chunk_bias], axis=2)],
                      axis=1)
    cur = lambda h, b, t: (rb0 + b * nt + t, h)
    prev = lambda h, b, t: (rb0 + b * nt + jnp.maximum(t - 1, 0), h)
    cur_v = lambda h, b, t: (rb0 + b * nt + t, nh + h)
    prev_v = lambda h, b, t: (rb0 + b * nt + jnp.maximum(t - 1, 0), nh + h)
    blk = (tq, hp * dh)
    return pl.pallas_call(
        functools.partial(_band_prompt_kernel, tq=tq, dh=dh),
        grid=(nh, batch, nt),
        in_specs=[pl.BlockSpec(blk, cur),
                  pl.BlockSpec(blk, prev), pl.BlockSpec(blk, cur),
                  pl.BlockSpec(blk, prev_v), pl.BlockSpec(blk, cur_v),
                  pl.BlockSpec((hp, 2, CHUNK, tq + 2 * CHUNK), lambda h, b, t: (h, 0, 0, 0))],
        out_specs=pl.BlockSpec(blk, lambda h, b, t: (b * nt + t, h)),
        out_shape=jax.ShapeDtypeStruct((batch * seq, heads * dh), BF16),
        scratch_shapes=[pltpu.VMEM((hp, tq, 2 * tq), F32)],
        compiler_params=_cparams(3),
        name="band_prompt",
    )(q, kv, kv, kv, kv, bias2)


def _band_sample_kernel(q_ref, kc_ref, vc_ref, kn_ref, vn_ref, bias_ref, o_ref, *, w, dh):
    for u in range(bias_ref.shape[0]):
        hs = slice(u * dh, (u + 1) * dh)
        q = q_ref[:, hs]
        kc = kc_ref[0, :, hs].astype(BF16)
        vc = vc_ref[0, :, hs].astype(BF16)
        bias = bias_ref[u]
        sc = lax.dot_general(q, kc, _DN_T, preferred_element_type=F32) + bias[:, :w]
        sn = lax.dot_general(q, kn_ref[:, hs], _DN_T, preferred_element_type=F32) + bias[:, w:]
        m = jnp.maximum(jnp.max(sc, axis=1, keepdims=True), jnp.max(sn, axis=1, keepdims=True))
        pc = jnp.exp2(sc - m)
        pn = jnp.exp2(sn - m)
        l = jnp.sum(pc, axis=1, keepdims=True) + jnp.sum(pn, axis=1, keepdims=True)
        o = (jnp.dot(pc.astype(BF16), vc, preferred_element_type=F32)
             + jnp.dot(pn.astype(BF16), vn_ref[:, hs], preferred_element_type=F32)) / l
        o_ref[:, hs] = o.astype(o_ref.dtype)


def _band_sample(q, kv, cache_k, cache_v, bias, *, batch, ts, heads, dh, row0):
    hp = BAND_HEADS_PER_STEP
    w = cache_k.shape[1]
    rb0 = row0 // ts
    nh = heads // hp
    return pl.pallas_call(
        functools.partial(_band_sample_kernel, w=w, dh=dh),
        grid=(nh, batch),
        in_specs=[pl.BlockSpec((ts, hp * dh), lambda h, b: (rb0 + b, h)),
                  pl.BlockSpec((1, w, hp * dh), lambda h, b: (b, 0, h)),
                  pl.BlockSpec((1, w, hp * dh), lambda h, b: (b, 0, h)),
                  pl.BlockSpec((ts, hp * dh), lambda h, b: (rb0 + b, h)),
                  pl.BlockSpec((ts, hp * dh), lambda h, b: (rb0 + b, nh + h)),
                  pl.BlockSpec((hp, ts, w + ts), lambda h, b: (h, 0, 0))],
        out_specs=pl.BlockSpec((ts, hp * dh), lambda h, b: (b, h)),
        out_shape=jax.ShapeDtypeStruct((batch * ts, heads * dh), BF16),
        compiler_params=_cparams(2),
        name="band_sample",
    )(q, cache_k, cache_v, kv, kv, bias)


def _band_bias(rel_bias, q0, nq, k0, nk):
    diag = np.arange(nq + nk - 1) - (nq - 1)
    idx = np.clip(q0 - k0 - diag, -C_REL_CLIP, C_REL_CLIP) + C_REL_CLIP
    r = rel_bias[idx].T.astype(F32)
    bias = jnp.stack([r[:, nq - 1 - i:nq - 1 - i + nk] for i in range(nq)], axis=1)
    qpos = q0 + np.arange(nq)[:, None]
    kpos = k0 + np.arange(nk)[None, :]
    qch, kch = qpos // CHUNK, kpos // CHUNK
    ok = (kpos >= 0) & (kch <= qch) & (qch - kch <= C_LEFT_CHUNKS)
    return jnp.where(ok[None], bias, NEG)


def _diff_lambda(lq1_ref, lk1_ref, lq2_ref, lk2_ref, lam_init):
    s1 = jnp.sum(lq1_ref[...] * lk1_ref[...], axis=1, keepdims=True)
    s2 = jnp.sum(lq2_ref[...] * lk2_ref[...], axis=1, keepdims=True)
    return jnp.exp(s1) - jnp.exp(s2) + lam_init


def _stack_groups(q_ref, c, group, dh):
    return jnp.concatenate([q_ref[:, (g * 2 + c) * dh:(g * 2 + c + 1) * dh] for g in range(group)], axis=0)


def _diff_finish(o, g_ref, o_ref, *, group, tq, out_scale):
    y = o * lax.rsqrt(jnp.mean(o * o, axis=-1, keepdims=True) + EPS) * g_ref[...] * out_scale
    y = y.astype(o_ref.dtype)
    e = y.shape[1]
    for g in range(group):
        o_ref[:, g * e:(g + 1) * e] = y[g * tq:(g + 1) * tq]


def _diff_prompt_kernel(q_ref, k_ref, vt_ref, lq1_ref, lk1_ref, lq2_ref, lk2_ref, g_ref, o_ref,
                        acc_scr, s_scr, m_scr, l_scr, *, tq, kb, group, dh, lam_init):
    t = pl.program_id(2)
    rows = group * tq
    e = 2 * dh
    lam = _diff_lambda(lq1_ref, lk1_ref, lq2_ref, lk2_ref, lam_init)
    acc_scr[...] = jnp.zeros(acc_scr.shape, F32)
    m_scr[...] = jnp.full(m_scr.shape, NEG, F32)
    l_scr[...] = jnp.zeros(l_scr.shape, F32)
    n_blocks = ((t + 1) * tq + kb - 1) // kb
    n_free = n_blocks - 1

    def scores(i, slot):
        ks = pl.multiple_of(i * kb, kb)
        kblk = k_ref[pl.ds(ks, kb), :]
        for c in range(2):
            qc = _stack_groups(q_ref, c, group, dh)
            s_scr[slot, c] = lax.dot_general(kblk[:, c * dh:(c + 1) * dh], qc, _DN_T,
                                             preferred_element_type=F32)

    def accumulate(i, slot, masked):
        ks = pl.multiple_of(i * kb, kb)
        vtb = vt_ref[0, :, pl.ds(ks, kb)]
        if masked:
            qi = lax.broadcasted_iota(jnp.int32, (kb, rows), 1) % tq
            qend = ((t * tq + qi) // CHUNK + 1) * CHUNK
            kpos = ks + lax.broadcasted_iota(jnp.int32, (kb, rows), 0)
            bias = jnp.where(kpos < qend, 0.0, NEG)
        for c in range(2):
            s = s_scr[slot, c]
            if masked:
                s = s + bias
            m = m_scr[c]
            m_new = jnp.maximum(m, jnp.max(s, axis=0, keepdims=True))
            alpha = jnp.exp2(m - m_new)
            p = jnp.exp2(s - m_new)
            l_scr[c] = alpha * l_scr[c] + jnp.sum(p, axis=0, keepdims=True)
            m_scr[c] = m_new
            acc_scr[c] = alpha * acc_scr[c] + jnp.dot(vtb, p.astype(BF16), preferred_element_type=F32)

    scores(0, 0)

    def pair(j, carry):
        i = 2 * j
        scores(i + 1, 1)
        accumulate(i, 0, False)
        scores(i + 2, 0)
        accumulate(i + 1, 1, False)
        return carry

    lax.fori_loop(0, n_free // 2, pair, 0)
    i0 = (n_free // 2) * 2

    @pl.when(n_free % 2 == 1)
    def _():
        scores(i0 + 1, 1)
        accumulate(i0, 0, False)
        accumulate(i0 + 1, 1, True)

    @pl.when(n_free % 2 == 0)
    def _():
        accumulate(i0, 0, True)

    o = acc_scr[0] / l_scr[0] - lam * (acc_scr[1] / l_scr[1])
    y = o * lax.rsqrt(jnp.mean(o * o, axis=0, keepdims=True) + EPS) * (1.0 - lam_init)
    for g in range(group):
        o_ref[:, g * e:(g + 1) * e] = (y[:, g * tq:(g + 1) * tq].T * g_ref[...]).astype(o_ref.dtype)


def _lam_specs(n_grid):
    zero = (lambda *_: (0, 0))
    return [pl.BlockSpec((1, LANES), zero)] * 4


def _diff_prompt(q, k, vt, lams, subln_g, *, batch, seq, kvh, group, dh, lam_init, tq=LANES, kb=512):
    kb = _pick(seq, kb, LANES)
    nt = seq // tq
    rows = group * tq
    e = 2 * dh
    return pl.pallas_call(
        functools.partial(_diff_prompt_kernel, tq=tq, kb=kb, group=group, dh=dh, lam_init=lam_init),
        grid=(batch, kvh, nt),
        in_specs=[pl.BlockSpec((tq, group * e), lambda b, h, t: (b * nt + t, h)),
                  pl.BlockSpec((seq, e), lambda b, h, t: (b, h)),
                  pl.BlockSpec((1, e, seq), lambda b, h, t: (b, h, 0))]
                 + _lam_specs(3) + [pl.BlockSpec((1, e), lambda b, h, t: (0, 0))],
        out_specs=pl.BlockSpec((tq, group * e), lambda b, h, t: (b * nt + t, h)),
        out_shape=jax.ShapeDtypeStruct((batch * seq, kvh * group * e), BF16),
        scratch_shapes=[pltpu.VMEM((2, e, rows), F32), pltpu.VMEM((2, 2, kb, rows), F32),
                        pltpu.VMEM((2, 1, rows), F32), pltpu.VMEM((2, 1, rows), F32)],
        compiler_params=_cparams(3),
        name="diff_prompt",
    )(q, k, vt, *lams, subln_g.reshape(1, e))


def _diff_sample_kernel(q_ref, kc_ref, vc_ref, kn_ref, vn_ref, lq1_ref, lk1_ref, lq2_ref, lk2_ref, g_ref, o_ref,
                        *, ts, group, dh, lam_init):
    lam = _diff_lambda(lq1_ref, lk1_ref, lq2_ref, lk2_ref, lam_init)
    kc = kc_ref[0].astype(BF16)
    vc = vc_ref[0].astype(BF16)
    kn = kn_ref[...]
    a_c, a_n = [], []
    for c in range(2):
        qc = _stack_groups(q_ref, c, group, dh)
        sc = lax.dot_general(qc, kc[:, c * dh:(c + 1) * dh], _DN_T, preferred_element_type=F32)
        sn = lax.dot_general(qc, kn[:, c * dh:(c + 1) * dh], _DN_T, preferred_element_type=F32)
        m = jnp.maximum(jnp.max(sc, axis=1, keepdims=True), jnp.max(sn, axis=1, keepdims=True))
        pc = jnp.exp2(sc - m)
        pn = jnp.exp2(sn - m)
        l = jnp.sum(pc, axis=1, keepdims=True) + jnp.sum(pn, axis=1, keepdims=True)
        a_c.append(pc / l)
        a_n.append(pn / l)
    ac = (a_c[0] - lam * a_c[1]).astype(BF16)
    an = (a_n[0] - lam * a_n[1]).astype(BF16)
    o = (jnp.dot(ac, vc, preferred_element_type=F32) + jnp.dot(an, vn_ref[...], preferred_element_type=F32))
    _diff_finish(o, g_ref, o_ref, group=group, tq=ts, out_scale=1.0 - lam_init)


def _diff_sample(q, k, v, cache_k, cache_v, lams, subln_g, *, batch, ts, kvh, group, dh, lam_init, row0):
    past = cache_k.shape[1]
    rb0 = row0 // ts
    e = 2 * dh
    return pl.pallas_call(
        functools.partial(_diff_sample_kernel, ts=ts, group=group, dh=dh, lam_init=lam_init),
        grid=(batch, kvh),
        in_specs=[pl.BlockSpec((ts, group * e), lambda b, h: (rb0 + b, h)),
                  pl.BlockSpec((1, past, e), lambda b, h: (b, 0, h)),
                  pl.BlockSpec((1, past, e), lambda b, h: (b, 0, h)),
                  pl.BlockSpec((ts, e), lambda b, h: (rb0 + b, h)),
                  pl.BlockSpec((ts, e), lambda b, h: (rb0 + b, h))]
                 + _lam_specs(2) + [pl.BlockSpec((1, e), lambda b, h: (0, 0))],
        out_specs=pl.BlockSpec((ts, group * e), lambda b, h: (b, h)),
        out_shape=jax.ShapeDtypeStruct((batch * ts, kvh * group * e), BF16),
        compiler_params=_cparams(2),
        name="diff_sample",
    )(q, cache_k, cache_v, k, v, *lams, subln_g)


def _dsa_kernel(iq_ref, ik_ref, iwt_ref, q_ref, k_ref, vt_ref, o_ref, key_scr, acc_scr,
                *, tq, kb, causal, s_real, n_sel, n_idx, kvh, group, dh, iscale):
    t = pl.program_id(1)
    s_pad = ik_ref.shape[0]
    lane = lax.broadcasted_iota(jnp.int32, (1, tq), 1)
    if causal:
        qend = ((t * tq + lane) // CHUNK + 1) * CHUNK
        n_blocks = ((t + 1) * tq + kb - 1) // kb
    else:
        qend = jnp.full((1, tq), s_real, jnp.int32)
        n_blocks = s_pad // kb

    w = iwt_ref[...] * iscale

    def score_body(i, carry):
        ks = pl.multiple_of(i * kb, kb)
        ikb = ik_ref[pl.ds(ks, kb), :]
        sc = jnp.zeros((kb, tq), F32)
        for n in range(n_idx):
            logit = lax.dot_general(ikb, iq_ref[:, n * dh:(n + 1) * dh], _DN_T, preferred_element_type=F32)
            sc = sc + w[n:n + 1, :] * jnp.maximum(logit, 0.0)
        bits = lax.bitcast_convert_type(sc, jnp.int32)
        key = jnp.where(bits < 0, bits ^ 0x7FFFFFFF, bits)
        kpos = ks + lax.broadcasted_iota(jnp.int32, (kb, tq), 0)
        key_scr[pl.ds(ks, kb), :] = jnp.where(kpos < qend, key, INT_MIN)
        return carry

    lax.fori_loop(0, n_blocks, score_body, 0)

    def count_ge(cand):
        def body(i, acc):
            ks = pl.multiple_of(i * kb, kb)
            ge = key_scr[pl.ds(ks, kb), :] >= cand
            return acc + jnp.sum(ge.astype(jnp.int32), axis=0, keepdims=True)
        return lax.fori_loop(0, n_blocks, body, jnp.zeros((1, tq), jnp.int32))

    zero = jnp.zeros((1, tq), jnp.int32)
    thr = jnp.where(count_ge(zero) >= n_sel, zero, jnp.full((1, tq), INT_MIN, jnp.int32))

    def bit_body(i, thr):
        cand = thr | lax.shift_left(jnp.int32(1), 30 - i)
        return jnp.where(count_ge(cand) >= n_sel, cand, thr)

    thr = lax.fori_loop(0, 31, bit_body, thr)
    thr = jnp.maximum(thr, INT_MIN + 1)

    def bias_body(i, carry):
        ks = pl.multiple_of(i * kb, kb)
        bias = jnp.where(key_scr[pl.ds(ks, kb), :] >= thr, 0.0, NEG)
        key_scr[pl.ds(ks, kb), :] = lax.bitcast_convert_type(bias, jnp.int32)
        return carry

    lax.fori_loop(0, n_blocks, bias_body, 0)

    cols = group * tq
    for h in range(kvh):
        acc_scr[...] = jnp.zeros(acc_scr.shape, F32)

        def kv_body(i, st, h=h):
            m, l = st
            ks = pl.multiple_of(i * kb, kb)
            kblk = k_ref[pl.ds(ks, kb), h * dh:(h + 1) * dh]
            qh = jnp.concatenate([q_ref[:, (h * group + g) * dh:(h * group + g + 1) * dh] for g in range(group)],
                                 axis=0)
            bias = lax.bitcast_convert_type(key_scr[pl.ds(ks, kb), :], F32)
            s = lax.dot_general(kblk, qh, _DN_T, preferred_element_type=F32)
            s = s + jnp.concatenate([bias] * group, axis=1)
            m_new = jnp.maximum(m, jnp.max(s, axis=0, keepdims=True))
            alpha = jnp.exp2(m - m_new)
            p = jnp.exp2(s - m_new)
            l = alpha * l + jnp.sum(p, axis=0, keepdims=True)
            vtb = vt_ref[0, h * dh:(h + 1) * dh, pl.ds(ks, kb)]
            acc_scr[...] = alpha * acc_scr[...] + jnp.dot(vtb, p.astype(BF16), preferred_element_type=F32)
            return m_new, l

        _, l = lax.fori_loop(0, n_blocks, kv_body, (jnp.full((1, cols), NEG, F32), jnp.zeros((1, cols), F32)))
        out = acc_scr[...] / l
        for g in range(group):
            c0 = (h * group + g) * dh
            o_ref[:, c0:c0 + dh] = out[:, g * tq:(g + 1) * tq].T.astype(o_ref.dtype)


def _dsa(iq, ik, iwt, q, k, vt, *, batch, tq_total, tq, kb, causal, s_real, n_sel):
    dh = ik.shape[1]
    s_pad = vt.shape[2]
    kvh = k.shape[1] // dh
    heads = q.shape[1] // dh
    n_idx = iq.shape[1] // dh
    nt = tq_total // tq
    cols = heads // kvh * tq
    return pl.pallas_call(
        functools.partial(_dsa_kernel, tq=tq, kb=kb, causal=causal, s_real=s_real, n_sel=n_sel, n_idx=n_idx,
                          kvh=kvh, group=heads // kvh, dh=dh, iscale=(n_idx * dh) ** -0.5),
        grid=(batch, nt),
        in_specs=[pl.BlockSpec((tq, n_idx * dh), lambda b, t: (b * nt + t, 0)),
                  pl.BlockSpec((s_pad, dh), lambda b, t: (b, 0)),
                  pl.BlockSpec((n_idx, tq), lambda b, t: (0, b * nt + t)),
                  pl.BlockSpec((tq, heads * dh), lambda b, t: (b * nt + t, 0)),
                  pl.BlockSpec((s_pad, kvh * dh), lambda b, t: (b, 0)),
                  pl.BlockSpec((1, kvh * dh, s_pad), lambda b, t: (b, 0, 0))],
        out_specs=pl.BlockSpec((tq, heads * dh), lambda b, t: (b * nt + t, 0)),
        out_shape=jax.ShapeDtypeStruct((batch * tq_total, heads * dh), BF16),
        scratch_shapes=[pltpu.VMEM((s_pad, tq), jnp.int32), pltpu.VMEM((dh, cols), F32)],
        compiler_params=_cparams(2),
        name="dsa",
    )(iq, ik, iwt, q, k, vt)


def _mixer_a(xn, tabs, w_in, lams, subln_g, cache_k, cache_v, lam_init, dims):
    bp, tp, bs, ts, d, past = dims
    mp = bp * tp
    dh = d // (2 * A_HEADS)
    a_q = A_HEADS * 2 * dh
    a_kv = A_KV_HEADS * 2 * dh
    group = A_HEADS // A_KV_HEADS
    (q,) = _proj(xn, w_in[:, :a_q].astype(BF16), (BF16,), rope=tabs, out_scale=dh ** -0.5 * LOG2E)
    k32, k16 = _proj(xn, w_in[:, a_q:a_q + a_kv].astype(BF16), (F32, BF16), rope=tabs)
    v32, v16 = _proj(xn, w_in[:, a_q + a_kv:].astype(BF16), (F32, BF16))
    lams = tuple(x.reshape(1, dh).astype(F32) for x in lams)
    subln_g = subln_g.astype(F32)
    vt_p = jnp.swapaxes(v16[:mp].reshape(bp, tp, a_kv), 1, 2)
    o_p = _diff_prompt(q, k16, vt_p, lams, subln_g, batch=bp, seq=tp, kvh=A_KV_HEADS, group=group, dh=dh,
                       lam_init=lam_init)
    o_s = _diff_sample(q, k16, v16, cache_k.reshape(bs, past, a_kv), cache_v.reshape(bs, past, a_kv), lams,
                       subln_g.reshape(1, 2 * dh), batch=bs, ts=ts, kvh=A_KV_HEADS, group=group, dh=dh,
                       lam_init=lam_init, row0=mp)
    return (o_p, o_s), k32, v32


def _mixer_b(xn, tabs, w_in, cache_k, cache_v, cache_ik, dims):
    bp, tp, bs, ts, d, past = dims
    mp = bp * tp
    dh = d // B_HEADS
    o1 = B_HEADS * dh
    o2 = o1 + B_KV_HEADS * dh
    o3 = o2 + B_KV_HEADS * dh
    o4 = o3 + B_IDX_HEADS * B_IDX_DIM
    o5 = o4 + B_IDX_DIM
    kvw = B_KV_HEADS * dh
    (q,) = _proj(xn, w_in[:, :o1].astype(BF16), (BF16,), rope=tabs, out_scale=dh ** -0.5 * LOG2E)
    k32, k16 = _proj(xn, w_in[:, o1:o2].astype(BF16), (F32, BF16), rope=tabs)
    v32, v16 = _proj(xn, w_in[:, o2:o3].astype(BF16), (F32, BF16))
    (iq,) = _proj(xn, w_in[:, o3:o4].astype(BF16), (BF16,), rope=tabs)
    ik32, ik16 = _proj(xn, w_in[:, o4:o5].astype(BF16), (F32, BF16), rope=tabs)
    w_iw = jnp.pad(w_in[:, o5:], ((0, 0), (0, LANES - B_IDX_HEADS))).astype(BF16)
    (iw,) = _proj(xn, w_iw, (F32,))
    iwt = iw[:, :B_IDX_HEADS].T

    vt_p = jnp.swapaxes(v16[:mp].reshape(bp, tp, kvw), 1, 2)
    tq_p = _pick(tp, 256, LANES)
    o_p = _dsa(iq, ik16, iwt, q, k16, vt_p, batch=bp, tq_total=tp, tq=tq_p, kb=_pick(tp, 512, LANES),
               causal=True, s_real=tp, n_sel=min(B_TOPK_MAX, tp // 4))

    past = cache_k.shape[1]
    s_real = past + ts
    s_pad = -(-s_real // LANES) * LANES
    tq_s = -(-ts // LANES) * LANES

    def keys(cache, new):
        new = new[mp:].reshape(bs, ts, -1)
        kk = jnp.concatenate([cache.reshape(bs, past, -1).astype(BF16), new,
                              jnp.zeros((bs, s_pad - s_real, new.shape[-1]), BF16)], axis=1)
        return kk

    def queries(a):
        a = jnp.pad(a[mp:].reshape(bs, ts, -1), ((0, 0), (0, tq_s - ts), (0, 0)))
        return a.reshape(bs * tq_s, -1)

    kk = keys(cache_k, k16).reshape(bs * s_pad, kvw)
    ikk = keys(cache_ik, ik16).reshape(bs * s_pad, B_IDX_DIM)
    vt_s = jnp.swapaxes(keys(cache_v, v16), 1, 2)
    o_s = _dsa(queries(iq), ikk, queries(iw[:, :B_IDX_HEADS]).T, queries(q), kk, vt_s, batch=bs, tq_total=tq_s,
               tq=tq_s, kb=_pick(s_pad, 512, LANES), causal=False, s_real=s_real,
               n_sel=min(B_TOPK_MAX, s_real // 4))
    o_s = o_s.reshape(bs, tq_s, o1)[:, :ts].reshape(bs * ts, o1)
    return (o_p, o_s), k32, v32, ik32


def _mixer_c(xn, w_in, rel_bias, cache_k, cache_v, dims):
    bp, tp, bs, ts, d, past = dims
    mp = bp * tp
    dh = d // C_HEADS
    hw = C_HEADS * dh
    win = C_LEFT_CHUNKS * CHUNK
    (q,) = _proj(xn, w_in[:, :hw].astype(BF16), (BF16,), out_scale=dh ** -0.5 * LOG2E)
    kv32, kv16 = _proj(xn, w_in[:, hw:].astype(BF16), (F32, BF16))
    rel_bias = rel_bias.astype(F32) * LOG2E

    chunk_bias = _band_bias(rel_bias, win, CHUNK, 0, win + CHUNK)
    o_p = _band_prompt(q, kv16, chunk_bias, batch=bp, seq=tp, heads=C_HEADS, dh=dh, row0=0)

    w_c = cache_k.shape[1]
    o_s = _band_sample(q, kv16, cache_k.reshape(bs, w_c, hw), cache_v.reshape(bs, w_c, hw),
                       _band_bias(rel_bias, past, ts, past - w_c, w_c + ts),
                       batch=bs, ts=ts, heads=C_HEADS, dh=dh, row0=mp)
    return (o_p, o_s), kv32[:, :hw], kv32[:, hw:]


def kernel(x_prompt, x_sample, cache_a_k, cache_a_v, cache_b_k, cache_b_v, cache_b_idx_k, cache_c_k, cache_c_v,
           norm_mix_g, norm_ffn_g, norm_out_g, a_w_in, a_w_out, a_lam_q1, a_lam_k1, a_lam_q2, a_lam_k2,
           a_subln_g, b_w_in, b_w_out, c_w_in, c_w_out, c_rel_bias, ffn_w_gu, ffn_w_down):
    bp, tp, d = x_prompt.shape
    bs, ts, _ = x_sample.shape
    past = cache_a_k.shape[2]
    depth = norm_mix_g.shape[0]
    assert ts <= CHUNK and past % CHUNK == 0 and tp % CHUNK == 0
    mp, ms = bp * tp, bs * ts
    dims = (bp, tp, bs, ts, d, past)
    i32 = jnp.int32

    h, xn = _join_norm(x_prompt.reshape(mp, d), x_sample.reshape(ms, d), norm_mix_g[0])
    pos =jnp.concatenate([jnp.tile(jnp.arange(tp, dtype=i32), bp), jnp.tile(past + jnp.arange(ts, dtype=i32), bs)])
    tabs = _rope_tables(pos)
    w_gu16 = ffn_w_gu.astype(BF16)
    w_down16 = ffn_w_down.astype(BF16)

    def split(a, tail):
        return a[:mp].reshape((bp, tp) + tail), a[mp:].reshape((bs, ts) + tail)

    st = {name: [] for name in ("a_k", "a_v", "b_k", "b_v", "b_ik", "c_k", "c_v")}
    for i in range(depth):
        j = i // N_MIXERS
        kind = i % N_MIXERS
        if kind == 0:
            lam_init = 0.8 - 0.6 * math.exp(-0.3 * i)
            o, k, v = _mixer_a(xn, tabs, a_w_in[j], (a_lam_q1[j], a_lam_k1[j], a_lam_q2[j], a_lam_k2[j]),
                               a_subln_g[j], cache_a_k[j], cache_a_v[j], lam_init, dims)
            tail = cache_a_k.shape[3:]
            st["a_k"].append(split(k, tail))
            st["a_v"].append(split(v, tail))
            w_out = a_w_out[j]
        elif kind == 1:
            o, k, v, ik = _mixer_b(xn, tabs, b_w_in[j], cache_b_k[j], cache_b_v[j], cache_b_idx_k[j], dims)
            tail = cache_b_k.shape[3:]
            st["b_k"].append(split(k, tail))
            st["b_v"].append(split(v, tail))
            st["b_ik"].append(split(ik, cache_b_idx_k.shape[3:]))
            w_out = b_w_out[j]
        else:
            o, k, v = _mixer_c(xn, c_w_in[j], c_rel_bias[j], cache_c_k[j], cache_c_v[j], dims)
            tail = cache_c_k.shape[3:]
            keep = min(C_LEFT_CHUNKS * CHUNK, tp)
            for name, a, cache in (("c_k", k, cache_c_k[j]), ("c_v", v, cache_c_v[j])):
                a_p, a_s = split(a, tail)
                kk_s = jnp.concatenate([cache, a_s], axis=1)
                st[name].append((a_p[:, tp - keep:], kk_s[:, kk_s.shape[1] - cache.shape[1]:]))
            w_out = c_w_out[j]
        h = _out_proj(o[0], o[1], w_out.astype(BF16), h)
        if i + 1 < depth:
            h, xn = _ffn(h, norm_ffn_g[i], w_gu16, w_down16, i, g_next=norm_mix_g[i + 1])
        else:
            h = _ffn(h, norm_ffn_g[i], w_gu16, w_down16, i)

    y_p = _rmsnorm(h, norm_out_g, F32, 0, mp).reshape(bp, tp, d)
    y_s = _rmsnorm(h, norm_out_g, F32, mp, ms).reshape(bs, ts, d)
    stack = lambda name, g: jnp.stack([pair[g] for pair in st[name]], axis=0)
    names = ("a_k", "a_v", "b_k", "b_v", "b_ik", "c_k", "c_v")
    return (y_p, y_s) + tuple(stack(n, 0) for n in names) + tuple(stack(n, 1) for n in names)
```

```python
import functools
import math

import jax
import jax.numpy as jnp
import numpy as np
from jax import lax
from jax.experimental import pallas as pl
from jax.experimental.pallas import tpu as pltpu

F32 = jnp.float32
BF16 = jnp.bfloat16

CHUNK = 64
N_MIXERS = 3
EPS = 1e-6
ROPE_THETA = 500000.0
NEG = -1e30
A_HEADS = 16
A_KV_HEADS = 4
B_HEADS = 32
B_KV_HEADS = 4
B_IDX_HEADS = 32
B_IDX_DIM = 128
B_TOPK_MAX = 256
C_HEADS = 32
C_LEFT_CHUNKS = 8
C_REL_CLIP = 256

LANES = 128
VMEM_LIMIT = 56 * 1024 * 1024
FFN_VMEM_LIMIT = 58 * 1024 * 1024
INT_MIN = -2 ** 31
LOG2E = math.log2(math.e)


def _cparams(n_grid, vmem_limit=VMEM_LIMIT):
    return pltpu.CompilerParams(dimension_semantics=("arbitrary",) * n_grid,
                                vmem_limit_bytes=vmem_limit)


def _pick(n, target, mult=16):
    for t in range(min(n, target), 0, -1):
        if n % t == 0 and t % mult == 0:
            return t
    return n


def _rmsnorm_kernel(x_ref, g_ref, o_ref):
    x = x_ref[...]
    y = x * lax.rsqrt(jnp.mean(x * x, axis=-1, keepdims=True) + EPS)
    o_ref[...] = (y * g_ref[...]).astype(o_ref.dtype)


def _rmsnorm(x, g, out_dtype, row0=0, rows=None):
    d = x.shape[1]
    m = x.shape[0] if rows is None else rows
    tm = _pick(math.gcd(m, row0) if row0 else m, 512)
    b0 = row0 // tm
    return pl.pallas_call(
        _rmsnorm_kernel,
        grid=(m // tm,),
        in_specs=[pl.BlockSpec((tm, d), lambda i: (b0 + i, 0)),
                  pl.BlockSpec((1, d), lambda i: (0, 0))],
        out_specs=pl.BlockSpec((tm, d), lambda i: (i, 0)),
        out_shape=jax.ShapeDtypeStruct((m, d), out_dtype),
        compiler_params=_cparams(1),
        name="rmsnorm",
    )(x, g.reshape(1, d).astype(F32))


def _join_norm_kernel(xp_ref, xs_ref, g_ref, h_ref, xn_ref, *, n_p):
    i = pl.program_id(0)

    def emit(x_ref):
        x = x_ref[...]
        h_ref[...] = x
        y = x * lax.rsqrt(jnp.mean(x * x, axis=-1, keepdims=True) + EPS)
        xn_ref[...] = (y * g_ref[...]).astype(xn_ref.dtype)

    @pl.when(i < n_p)
    def _():
        emit(xp_ref)

    @pl.when(i >= n_p)
    def _():
        emit(xs_ref)


def _join_norm(x_p, x_s, g):
    (m_p, d), m_s = x_p.shape, x_s.shape[0]
    tm = _pick(math.gcd(m_p, m_s), 256)
    n_p = m_p // tm
    row = pl.BlockSpec((tm, d), lambda i: (i, 0))
    return pl.pallas_call(
        functools.partial(_join_norm_kernel, n_p=n_p),
        grid=((m_p + m_s) // tm,),
        in_specs=[pl.BlockSpec((tm, d), lambda i: (jnp.minimum(i, n_p - 1), 0)),
                  pl.BlockSpec((tm, d), lambda i: (jnp.maximum(i - n_p, 0), 0)),
                  pl.BlockSpec((1, d), lambda i: (0, 0))],
        out_specs=[row, row],
        out_shape=[jax.ShapeDtypeStruct((m_p + m_s, d), F32), jax.ShapeDtypeStruct((m_p + m_s, d), BF16)],
        compiler_params=_cparams(1),
        name="join_norm",
    )(x_p, x_s, g.reshape(1, d).astype(F32))


def _rope_tables(pos, dh=LANES):
    r = dh // 4
    half = r // 2
    inv = ROPE_THETA ** (-2.0 * jnp.arange(half, dtype=F32) / r)
    ang = pos.astype(F32)[:, None] * inv[None, :]
    cos, sin = jnp.cos(ang), jnp.sin(ang)
    rows = pos.shape[0]
    zeros = jnp.zeros((rows, dh - r), F32)
    zh = jnp.zeros((rows, half), F32)
    c = jnp.concatenate([cos, cos, jnp.ones((rows, dh - r), F32)], axis=1)
    sa = jnp.concatenate([zh, sin, zeros], axis=1)
    sb = jnp.concatenate([-sin, zh, zeros], axis=1)
    return c, sa, sb


def _rope_slab(x, c, sa, sb):
    return x * c + pltpu.roll(x, 16, 1) * sa + pltpu.roll(x, LANES - 16, 1) * sb


def _proj_kernel(*refs, rope, residual, n_out, out_scale):
    x_ref, w_ref = refs[0], refs[1]
    pos = 2
    if rope:
        c_ref, sa_ref, sb_ref = refs[pos:pos + 3]
        pos += 3
    if residual:
        r_ref = refs[pos]
        pos += 1
    outs = refs[pos:pos + n_out]
    acc = jnp.dot(x_ref[...], w_ref[...], preferred_element_type=F32)
    if rope:
        c, sa, sb = c_ref[...], sa_ref[...], sb_ref[...]
        tn = acc.shape[1]
        acc = jnp.concatenate(
            [_rope_slab(acc[:, s * LANES:(s + 1) * LANES], c, sa, sb) for s in range(tn // LANES)], axis=1)
    if residual:
        acc = acc + r_ref[...]
    if out_scale != 1.0:
        acc = acc * out_scale
    for o in outs:
        o[...] = acc.astype(o.dtype)


def _proj(x, w, out_dtypes, rope=None, residual=None, out_scale=1.0, tm_target=1024, tn_target=None):
    m, k = x.shape
    n = w.shape[1]
    if tn_target is None:
        tn_target = 1024 if residual is None and all(dt == BF16 for dt in out_dtypes) else 512
    tm = _pick(m, tm_target)
    tn = _pick(n, tn_target, LANES)
    in_specs = [pl.BlockSpec((tm, k), lambda i, j: (i, 0)),
                pl.BlockSpec((k, tn), lambda i, j: (0, j))]
    args = [x, w]
    if rope is not None:
        in_specs += [pl.BlockSpec((tm, LANES), lambda i, j: (i, 0))] * 3
        args += list(rope)
    if residual is not None:
        in_specs.append(pl.BlockSpec((tm, tn), lambda i, j: (i, j)))
        args.append(residual)
    outs = pl.pallas_call(
        functools.partial(_proj_kernel, rope=rope is not None, residual=residual is not None,
                          n_out=len(out_dtypes), out_scale=out_scale),
        grid=(m // tm, n // tn),
        in_specs=in_specs,
        out_specs=[pl.BlockSpec((tm, tn), lambda i, j: (i, j)) for _ in out_dtypes],
        out_shape=[jax.ShapeDtypeStruct((m, n), dt) for dt in out_dtypes],
        compiler_params=_cparams(2),
        name="proj",
    )(*args)
    return outs


def _out_proj_kernel(xp_ref, xs_ref, w_ref, r_ref, o_ref, *, n_p):
    i = pl.program_id(0)

    @pl.when(i < n_p)
    def _():
        o_ref[...] = r_ref[...] + jnp.dot(xp_ref[...], w_ref[...], preferred_element_type=F32)

    @pl.when(i >= n_p)
    def _():
        o_ref[...] = r_ref[...] + jnp.dot(xs_ref[...], w_ref[...], preferred_element_type=F32)


def _out_proj(x_p, x_s, w, residual, tm_target=1024, tn_target=512):
    (m_p, k), m_s = x_p.shape, x_s.shape[0]
    n = w.shape[1]
    tm = _pick(math.gcd(m_p, m_s), tm_target)
    tn = _pick(n, tn_target, LANES)
    n_p = m_p // tm
    return pl.pallas_call(
        functools.partial(_out_proj_kernel, n_p=n_p),
        grid=((m_p + m_s) // tm, n // tn),
        in_specs=[pl.BlockSpec((tm, k), lambda i, j: (jnp.minimum(i, n_p - 1), 0)),
                  pl.BlockSpec((tm, k), lambda i, j: (jnp.maximum(i - n_p, 0), 0)),
                  pl.BlockSpec((k, tn), lambda i, j: (0, j)),
                  pl.BlockSpec((tm, tn), lambda i, j: (i, j))],
        out_specs=pl.BlockSpec((tm, tn), lambda i, j: (i, j)),
        out_shape=jax.ShapeDtypeStruct((m_p + m_s, n), F32),
        compiler_params=_cparams(2),
        name="out_proj",
    )(x_p, x_s, w, residual)


FFN_DOWN_CHUNKS = 4


def _ffn_kernel(*refs, next_norm):
    if next_norm:
        h_ref, g_ref, wg_ref, wu_ref, wd_ref, gn_ref, o_ref, xo_ref, xn_ref = refs
    else:
        h_ref, g_ref, wg_ref, wu_ref, wd_ref, o_ref, xn_ref = refs
    j = pl.program_id(1)

    @pl.when(j == 0)
    def _():
        x = h_ref[...]
        y = x * lax.rsqrt(jnp.mean(x * x, axis=-1, keepdims=True) + EPS)
        xn_ref[...] = (y * g_ref[...]).astype(BF16)
        o_ref[...] = x

    xn = xn_ref[...]
    gate = jnp.dot(xn, wg_ref[...], preferred_element_type=F32)
    up = jnp.dot(xn, wu_ref[...], preferred_element_type=F32)
    act = (gate * (1.0 / (1.0 + jnp.exp(-gate))) * up).astype(BF16)
    cw = o_ref.shape[1] // FFN_DOWN_CHUNKS
    for c in range(FFN_DOWN_CHUNKS):
        cs = slice(c * cw, (c + 1) * cw)
        o_ref[:, cs] += jnp.dot(act, wd_ref[:, cs], preferred_element_type=F32)

    if next_norm:
        @pl.when(j == pl.num_programs(1) - 1)
        def _():
            x = o_ref[...]
            y = x * lax.rsqrt(jnp.mean(x * x, axis=-1, keepdims=True) + EPS)
            xo_ref[...] = (y * gn_ref[...]).astype(xo_ref.dtype)


def _ffn(h, g, w_gu, w_down, layer, g_next=None, tm_target=512, tf=256):
    m, d = h.shape
    f = w_down.shape[1]
    tm = _pick(m, tm_target)
    nf = f // tf
    row = pl.BlockSpec((tm, d), lambda i, j: (i, 0))
    gain = pl.BlockSpec((1, d), lambda i, j: (0, 0))
    in_specs = [pl.BlockSpec((tm, d), lambda i, j: (i, 0), pipeline_mode=pl.Buffered(1)),
                gain,
                pl.BlockSpec((None, d, tf), lambda i, j: (layer, 0, j)),
                pl.BlockSpec((None, d, tf), lambda i, j: (layer, 0, j + nf)),
                pl.BlockSpec((None, tf, d), lambda i, j: (layer, j, 0))]
    args = [h, g.reshape(1, d).astype(F32), w_gu, w_gu, w_down]
    out_specs, out_shape = row, jax.ShapeDtypeStruct((m, d), F32)
    if g_next is not None:
        in_specs.append(gain)
        args.append(g_next.reshape(1, d).astype(F32))
        out_specs, out_shape = [row, row], [out_shape, jax.ShapeDtypeStruct((m, d), BF16)]
    return pl.pallas_call(
        functools.partial(_ffn_kernel, next_norm=g_next is not None),
        grid=(m // tm, nf),
        in_specs=in_specs,
        out_specs=out_specs,
        out_shape=out_shape,
        scratch_shapes=[pltpu.VMEM((tm, d), BF16)],
        compiler_params=_cparams(2, FFN_VMEM_LIMIT),
        name="ffn",
    )(*args)


_DN_T = (((1,), (1,)), ((), ()))


BAND_HEADS_PER_STEP = 4


def _band_prompt_kernel(q_ref, kp_ref, kc_ref, vp_ref, vc_ref, bias_ref, o_ref, bias_scr, *, tq, dh):
    t = pl.program_id(2)
    hp = bias_ref.shape[0]

    @pl.when((pl.program_id(1) == 0) & (t == 0))
    def _():
        bias_scr[...] = jnp.full(bias_scr.shape, NEG, F32)
        width = bias_ref.shape[3]
        for u in range(hp):
            for cc in range(tq // CHUNK):
                c0 = (cc // 2) * LANES
                bias_scr[u, cc * CHUNK:(cc + 1) * CHUNK, c0:c0 + width] = bias_ref[u, cc % 2]

    col = lax.broadcasted_iota(jnp.int32, (1, 2 * tq), 1)
    before_start = jnp.where((col < tq) & (t == 0), NEG, 0.0)
    for u in range(hp):
        hs = slice(u * dh, (u + 1) * dh)
        k = jnp.concatenate([kp_ref[:, hs], kc_ref[:, hs]], axis=0)
        v = jnp.concatenate([vp_ref[:, hs], vc_ref[:, hs]], axis=0)
        s = lax.dot_general(q_ref[:, hs], k, _DN_T, preferred_element_type=F32) + bias_scr[u] + before_start
        m = jnp.max(s, axis=1, keepdims=True)
        p = jnp.exp2(s - m)
        l = jnp.sum(p, axis=1, keepdims=True)
        o = jnp.dot(p.astype(BF16), v, preferred_element_type=F32) / l
        o_ref[:, hs] = o.astype(o_ref.dtype)


def _band_prompt(q, kv, chunk_bias, *, batch, seq, heads, dh, row0):
    assert 2 * CHUNK == LANES
    hp = BAND_HEADS_PER_STEP
    tq = chunk_bias.shape[2] - CHUNK
    nt = seq // tq
    rb0 = row0 // tq
    nh = heads // hp
    pad = jnp.full((heads, CHUNK, CHUNK), NEG, F32)
    bias2 = jnp.stack([jnp.concatenate([chunk_bias, pad], axis=2), jnp.concatenate([pad, chunk_bias], axis=2)],
                      axis=1)
    cur = lambda h, b, t: (rb0 + b * nt + t, h)
    prev = lambda h, b, t: (rb0 + b * nt + jnp.maximum(t - 1, 0), h)
    cur_v = lambda h, b, t: (rb0 + b * nt + t, nh + h)
    prev_v = lambda h, b, t: (rb0 + b * nt + jnp.maximum(t - 1, 0), nh + h)
    blk = (tq, hp * dh)
    return pl.pallas_call(
        functools.partial(_band_prompt_kernel, tq=tq, dh=dh),
        grid=(nh, batch, nt),
        in_specs=[pl.BlockSpec(blk, cur),
                  pl.BlockSpec(blk, prev), pl.BlockSpec(blk, cur),
                  pl.BlockSpec(blk, prev_v), pl.BlockSpec(blk, cur_v),
                  pl.BlockSpec((hp, 2, CHUNK, tq + 2 * CHUNK), lambda h, b, t: (h, 0, 0, 0))],
        out_specs=pl.BlockSpec(blk, lambda h, b, t: (b * nt + t, h)),
        out_shape=jax.ShapeDtypeStruct((batch * seq, heads * dh), BF16),
        scratch_shapes=[pltpu.VMEM((hp, tq, 2 * tq), F32)],
        compiler_params=_cparams(3),
        name="band_prompt",
    )(q, kv, kv, kv, kv, bias2)


def _band_sample_kernel(q_ref, kc_ref, vc_ref, kn_ref, vn_ref, bias_ref, o_ref, *, w, dh):
    for u in range(bias_ref.shape[0]):
        hs = slice(u * dh, (u + 1) * dh)
        q = q_ref[:, hs]
        kc = kc_ref[0, :, hs].astype(BF16)
        vc = vc_ref[0, :, hs].astype(BF16)
        bias = bias_ref[u]
        sc = lax.dot_general(q, kc, _DN_T, preferred_element_type=F32) + bias[:, :w]
        sn = lax.dot_general(q, kn_ref[:, hs], _DN_T, preferred_element_type=F32) + bias[:, w:]
        m = jnp.maximum(jnp.max(sc, axis=1, keepdims=True), jnp.max(sn, axis=1, keepdims=True))
        pc = jnp.exp2(sc - m)
        pn = jnp.exp2(sn - m)
        l = jnp.sum(pc, axis=1, keepdims=True) + jnp.sum(pn, axis=1, keepdims=True)
        o = (jnp.dot(pc.astype(BF16), vc, preferred_element_type=F32)
             + jnp.dot(pn.astype(BF16), vn_ref[:, hs], preferred_element_type=F32)) / l
        o_ref[:, hs] = o.astype(o_ref.dtype)


def _band_sample(q, kv, cache_k, cache_v, bias, *, batch, ts, heads, dh, row0):
    hp = BAND_HEADS_PER_STEP
    w = cache_k.shape[1]
    rb0 = row0 // ts
    nh = heads // hp
    return pl.pallas_call(
        functools.partial(_band_sample_kernel, w=w, dh=dh),
        grid=(nh, batch),
        in_specs=[pl.BlockSpec((ts, hp * dh), lambda h, b: (rb0 + b, h)),
                  pl.BlockSpec((1, w, hp * dh), lambda h, b: (b, 0, h)),
                  pl.BlockSpec((1, w, hp * dh), lambda h, b: (b, 0, h)),
                  pl.BlockSpec((ts, hp * dh), lambda h, b: (rb0 + b, h)),
                  pl.BlockSpec((ts, hp * dh), lambda h, b: (rb0 + b, nh + h)),
                  pl.BlockSpec((hp, ts, w + ts), lambda h, b: (h, 0, 0))],
        out_specs=pl.BlockSpec((ts, hp * dh), lambda h, b: (b, h)),
        out_shape=jax.ShapeDtypeStruct((batch * ts, heads * dh), BF16),
        compiler_params=_cparams(2),
        name="band_sample",
    )(q, cache_k, cache_v, kv, kv, bias)


def _band_bias(rel_bias, q0, nq, k0, nk):
    diag = np.arange(nq + nk - 1) - (nq - 1)
    idx = np.clip(q0 - k0 - diag, -C_REL_CLIP, C_REL_CLIP) + C_REL_CLIP
    r = rel_bias[idx].T.astype(F32)
    bias = jnp.stack([r[:, nq - 1 - i:nq - 1 - i + nk] for i in range(nq)], axis=1)
    qpos = q0 + np.arange(nq)[:, None]
    kpos = k0 + np.arange(nk)[None, :]
    qch, kch = qpos // CHUNK, kpos // CHUNK
    ok = (kpos >= 0) & (kch <= qch) & (qch - kch <= C_LEFT_CHUNKS)
    return jnp.where(ok[None], bias, NEG)


def _diff_lambda(lq1_ref, lk1_ref, lq2_ref, lk2_ref, lam_init):
    s1 = jnp.sum(lq1_ref[...] * lk1_ref[...], axis=1, keepdims=True)
    s2 = jnp.sum(lq2_ref[...] * lk2_ref[...], axis=1, keepdims=True)
    return jnp.exp(s1) - jnp.exp(s2) + lam_init


def _stack_groups(q_ref, c, group, dh):
    return jnp.concatenate([q_ref[:, (g * 2 + c) * dh:(g * 2 + c + 1) * dh] for g in range(group)], axis=0)


def _diff_finish(o, g_ref, o_ref, *, group, tq, out_scale):
    y = o * lax.rsqrt(jnp.mean(o * o, axis=-1, keepdims=True) + EPS) * g_ref[...] * out_scale
    y = y.astype(o_ref.dtype)
    e = y.shape[1]
    for g in range(group):
        o_ref[:, g * e:(g + 1) * e] = y[g * tq:(g + 1) * tq]


def _diff_prompt_kernel(q_ref, k_ref, vt_ref, lq1_ref, lk1_ref, lq2_ref, lk2_ref, g_ref, o_ref,
                        acc_scr, s_scr, m_scr, l_scr, *, tq, kb, group, dh, lam_init):
    t = pl.program_id(2)
    rows = group * tq
    e = 2 * dh
    lam = _diff_lambda(lq1_ref, lk1_ref, lq2_ref, lk2_ref, lam_init)
    acc_scr[...] = jnp.zeros(acc_scr.shape, F32)
    m_scr[...] = jnp.full(m_scr.shape, NEG, F32)
    l_scr[...] = jnp.zeros(l_scr.shape, F32)
    n_blocks = ((t + 1) * tq + kb - 1) // kb
    n_free = n_blocks - 1

    def scores(i, slot):
        ks = pl.multiple_of(i * kb, kb)
        kblk = k_ref[pl.ds(ks, kb), :]
        for c in range(2):
            qc = _stack_groups(q_ref, c, group, dh)
            s_scr[slot, c] = lax.dot_general(kblk[:, c * dh:(c + 1) * dh], qc, _DN_T,
                                             preferred_element_type=F32)

    def accumulate(i, slot, masked):
        ks = pl.multiple_of(i * kb, kb)
        vtb = vt_ref[0, :, pl.ds(ks, kb)]
        if masked:
            qi = lax.broadcasted_iota(jnp.int32, (kb, rows), 1) % tq
            qend = ((t * tq + qi) // CHUNK + 1) * CHUNK
            kpos = ks + lax.broadcasted_iota(jnp.int32, (kb, rows), 0)
            bias = jnp.where(kpos < qend, 0.0, NEG)
        for c in range(2):
            s = s_scr[slot, c]
            if masked:
                s = s + bias
            m = m_scr[c]
            m_new = jnp.maximum(m, jnp.max(s, axis=0, keepdims=True))
            alpha = jnp.exp2(m - m_new)
            p = jnp.exp2(s - m_new)
            l_scr[c] = alpha * l_scr[c] + jnp.sum(p, axis=0, keepdims=True)
            m_scr[c] = m_new
            acc_scr[c] = alpha * acc_scr[c] + jnp.dot(vtb, p.astype(BF16), preferred_element_type=F32)

    scores(0, 0)

    def pair(j, carry):
        i = 2 * j
        scores(i + 1, 1)
        accumulate(i, 0, False)
        scores(i + 2, 0)
        accumulate(i + 1, 1, False)
        return carry

    lax.fori_loop(0, n_free // 2, pair, 0)
    i0 = (n_free // 2) * 2

    @pl.when(n_free % 2 == 1)
    def _():
        scores(i0 + 1, 1)
        accumulate(i0, 0, False)
        accumulate(i0 + 1, 1, True)

    @pl.when(n_free % 2 == 0)
    def _():
        accumulate(i0, 0, True)

    o = acc_scr[0] / l_scr[0] - lam * (acc_scr[1] / l_scr[1])
    y = o * lax.rsqrt(jnp.mean(o * o, axis=0, keepdims=True) + EPS) * (1.0 - lam_init)
    for g in range(group):
        o_ref[:, g * e:(g + 1) * e] = (y[:, g * tq:(g + 1) * tq].T * g_ref[...]).astype(o_ref.dtype)


def _lam_specs(n_grid):
    zero = (lambda *_: (0, 0))
    return [pl.BlockSpec((1, LANES), zero)] * 4


def _diff_prompt(q, k, vt, lams, subln_g, *, batch, seq, kvh, group, dh, lam_init, tq=LANES, kb=512):
    kb = _pick(seq, kb, LANES)
    nt = seq // tq
    rows = group * tq
    e = 2 * dh
    return pl.pallas_call(
        functools.partial(_diff_prompt_kernel, tq=tq, kb=kb, group=group, dh=dh, lam_init=lam_init),
        grid=(batch, kvh, nt),
        in_specs=[pl.BlockSpec((tq, group * e), lambda b, h, t: (b * nt + t, h)),
                  pl.BlockSpec((seq, e), lambda b, h, t: (b, h)),
                  pl.BlockSpec((1, e, seq), lambda b, h, t: (b, h, 0))]
                 + _lam_specs(3) + [pl.BlockSpec((1, e), lambda b, h, t: (0, 0))],
        out_specs=pl.BlockSpec((tq, group * e), lambda b, h, t: (b * nt + t, h)),
        out_shape=jax.ShapeDtypeStruct((batch * seq, kvh * group * e), BF16),
        scratch_shapes=[pltpu.VMEM((2, e, rows), F32), pltpu.VMEM((2, 2, kb, rows), F32),
                        pltpu.VMEM((2, 1, rows), F32), pltpu.VMEM((2, 1, rows), F32)],
        compiler_params=_cparams(3),
        name="diff_prompt",
    )(q, k, vt, *lams, subln_g.reshape(1, e))


def _diff_sample_kernel(q_ref, kc_ref, vc_ref, kn_ref, vn_ref, lq1_ref, lk1_ref, lq2_ref, lk2_ref, g_ref, o_ref,
                        *, ts, group, dh, lam_init):
    lam = _diff_lambda(lq1_ref, lk1_ref, lq2_ref, lk2_ref, lam_init)
    kc = kc_ref[0].astype(BF16)
    vc = vc_ref[0].astype(BF16)
    kn = kn_ref[...]
    a_c, a_n = [], []
    for c in range(2):
        qc = _stack_groups(q_ref, c, group, dh)
        sc = lax.dot_general(qc, kc[:, c * dh:(c + 1) * dh], _DN_T, preferred_element_type=F32)
        sn = lax.dot_general(qc, kn[:, c * dh:(c + 1) * dh], _DN_T, preferred_element_type=F32)
        m = jnp.maximum(jnp.max(sc, axis=1, keepdims=True), jnp.max(sn, axis=1, keepdims=True))
        pc = jnp.exp2(sc - m)
        pn = jnp.exp2(sn - m)
        l = jnp.sum(pc, axis=1, keepdims=True) + jnp.sum(pn, axis=1, keepdims=True)
        a_c.append(pc / l)
        a_n.append(pn / l)
    ac = (a_c[0] - lam * a_c[1]).astype(BF16)
    an = (a_n[0] - lam * a_n[1]).astype(BF16)
    o = (jnp.dot(ac, vc, preferred_element_type=F32) + jnp.dot(an, vn_ref[...], preferred_element_type=F32))
    _diff_finish(o, g_ref, o_ref, group=group, tq=ts, out_scale=1.0 - lam_init)


def _diff_sample(q, k, v, cache_k, cache_v, lams, subln_g, *, batch, ts, kvh, group, dh, lam_init, row0):
    past = cache_k.shape[1]
    rb0 = row0 // ts
    e = 2 * dh
    return pl.pallas_call(
        functools.partial(_diff_sample_kernel, ts=ts, group=group, dh=dh, lam_init=lam_init),
        grid=(batch, kvh),
        in_specs=[pl.BlockSpec((ts, group * e), lambda b, h: (rb0 + b, h)),
                  pl.BlockSpec((1, past, e), lambda b, h: (b, 0, h)),
                  pl.BlockSpec((1, past, e), lambda b, h: (b, 0, h)),
                  pl.BlockSpec((ts, e), lambda b, h: (rb0 + b, h)),
                  pl.BlockSpec((ts, e), lambda b, h: (rb0 + b, h))]
                 + _lam_specs(2) + [pl.BlockSpec((1, e), lambda b, h: (0, 0))],
        out_specs=pl.BlockSpec((ts, group * e), lambda b, h: (b, h)),
        out_shape=jax.ShapeDtypeStruct((batch * ts, kvh * group * e), BF16),
        compiler_params=_cparams(2),
        name="diff_sample",
    )(q, cache_k, cache_v, k, v, *lams, subln_g)


def _dsa_kernel(iq_ref, ik_ref, iwt_ref, q_ref, k_ref, vt_ref, o_ref, key_scr, acc_scr, cut_scr,
                *, tq, kb, causal, s_real, n_sel, n_idx, kvh, group, dh, iscale):
    t = pl.program_id(1)
    s_pad = ik_ref.shape[0]
    lane = lax.broadcasted_iota(jnp.int32, (1, tq), 1)
    if causal:
        qend = ((t * tq + lane) // CHUNK + 1) * CHUNK
        n_blocks = ((t + 1) * tq + kb - 1) // kb
    else:
        qend = jnp.full((1, tq), s_real, jnp.int32)
        n_blocks = s_pad // kb

    w = iwt_ref[...] * iscale

    def score_body(i, carry):
        ks = pl.multiple_of(i * kb, kb)
        ikb = ik_ref[pl.ds(ks, kb), :]
        sc = jnp.zeros((kb, tq), F32)
        for n in range(n_idx):
            logit = lax.dot_general(ikb, iq_ref[:, n * dh:(n + 1) * dh], _DN_T, preferred_element_type=F32)
            sc = sc + w[n:n + 1, :] * jnp.maximum(logit, 0.0)
        bits = lax.bitcast_convert_type(sc, jnp.int32)
        key = jnp.where(bits < 0, bits ^ 0x7FFFFFFF, bits)
        kpos = ks + lax.broadcasted_iota(jnp.int32, (kb, tq), 0)
        key_scr[pl.ds(ks, kb), :] = jnp.where(kpos < qend, key, INT_MIN)
        return carry

    lax.fori_loop(0, n_blocks, score_body, 0)

    def count(pred):
        def body(i, acc):
            ks = pl.multiple_of(i * kb, kb)
            hit = pred(key_scr[pl.ds(ks, kb), :], ks)
            return acc + jnp.sum(hit.astype(jnp.int32), axis=0, keepdims=True)
        return lax.fori_loop(0, n_blocks, body, jnp.zeros((1, tq), jnp.int32))

    def count_ge(cand):
        return count(lambda key, ks: key >= cand)

    zero = jnp.zeros((1, tq), jnp.int32)
    thr = jnp.where(count_ge(zero) >= n_sel, zero, jnp.full((1, tq), INT_MIN, jnp.int32))

    def bit_body(i, thr):
        cand = thr | lax.shift_left(jnp.int32(1), 30 - i)
        return jnp.where(count_ge(cand) >= n_sel, cand, thr)

    thr = lax.fori_loop(0, 31, bit_body, thr)

    n_gt = count(lambda key, ks: key > thr)
    need = n_sel - n_gt
    surplus = (count_ge(thr) - n_gt > need) & (thr > INT_MIN)
    cut_scr[...] = jnp.where(thr > INT_MIN, s_pad, 0) + jnp.zeros(cut_scr.shape, jnp.int32)

    @pl.when(jnp.max(surplus.astype(jnp.int32)) > 0)
    def _():
        def tied_before(cand):
            def pred(key, ks):
                idx = ks + lax.broadcasted_iota(jnp.int32, (kb, tq), 0)
                return (key == thr) & (idx < cand)
            return count(pred)

        def idx_body(i, c):
            cand = c | lax.shift_left(jnp.int32(1), (s_pad - 1).bit_length() - 1 - i)
            return jnp.where(tied_before(cand) < need, cand, c)

        c = lax.fori_loop(0, (s_pad - 1).bit_length(), idx_body, zero)
        cut_scr[...] = jnp.where(surplus, c + 1, cut_scr[0:1, :]) + jnp.zeros(cut_scr.shape, jnp.int32)

    cut = cut_scr[0:1, :]

    def bias_body(i, carry):
        ks = pl.multiple_of(i * kb, kb)
        key = key_scr[pl.ds(ks, kb), :]
        idx = ks + lax.broadcasted_iota(jnp.int32, (kb, tq), 0)
        sel = (key > thr) | ((key == thr) & (idx < cut))
        key_scr[pl.ds(ks, kb), :] = lax.bitcast_convert_type(jnp.where(sel, 0.0, NEG), jnp.int32)
        return carry

    lax.fori_loop(0, n_blocks, bias_body, 0)

    cols = group * tq
    for h in range(kvh):
        acc_scr[...] = jnp.zeros(acc_scr.shape, F32)

        def kv_body(i, st, h=h):
            m, l = st
            ks = pl.multiple_of(i * kb, kb)
            kblk = k_ref[pl.ds(ks, kb), h * dh:(h + 1) * dh]
            qh = jnp.concatenate([q_ref[:, (h * group + g) * dh:(h * group + g + 1) * dh] for g in range(group)],
                                 axis=0)
            bias = lax.bitcast_convert_type(key_scr[pl.ds(ks, kb), :], F32)
            s = lax.dot_general(kblk, qh, _DN_T, preferred_element_type=F32)
            s = s + jnp.concatenate([bias] * group, axis=1)
            m_new = jnp.maximum(m, jnp.max(s, axis=0, keepdims=True))
            alpha = jnp.exp2(m - m_new)
            p = jnp.exp2(s - m_new)
            l = alpha * l + jnp.sum(p, axis=0, keepdims=True)
            vtb = vt_ref[0, h * dh:(h + 1) * dh, pl.ds(ks, kb)]
            acc_scr[...] = alpha * acc_scr[...] + jnp.dot(vtb, p.astype(BF16), preferred_element_type=F32)
            return m_new, l

        _, l = lax.fori_loop(0, n_blocks, kv_body, (jnp.full((1, cols), NEG, F32), jnp.zeros((1, cols), F32)))
        out = acc_scr[...] / l
        for g in range(group):
            c0 = (h * group + g) * dh
            o_ref[:, c0:c0 + dh] = out[:, g * tq:(g + 1) * tq].T.astype(o_ref.dtype)


def _dsa(iq, ik, iwt, q, k, vt, *, batch, tq_total, tq, kb, causal, s_real, n_sel):
    dh = ik.shape[1]
    s_pad = vt.shape[2]
    kvh = k.shape[1] // dh
    heads = q.shape[1] // dh
    n_idx = iq.shape[1] // dh
    nt = tq_total // tq
    cols = heads // kvh * tq
    return pl.pallas_call(
        functools.partial(_dsa_kernel, tq=tq, kb=kb, causal=causal, s_real=s_real, n_sel=n_sel, n_idx=n_idx,
                          kvh=kvh, group=heads // kvh, dh=dh, iscale=(n_idx * dh) ** -0.5),
        grid=(batch, nt),
        in_specs=[pl.BlockSpec((tq, n_idx * dh), lambda b, t: (b * nt + t, 0)),
                  pl.BlockSpec((s_pad, dh), lambda b, t: (b, 0)),
                  pl.BlockSpec((n_idx, tq), lambda b, t: (0, b * nt + t)),
                  pl.BlockSpec((tq, heads * dh), lambda b, t: (b * nt + t, 0)),
                  pl.BlockSpec((s_pad, kvh * dh), lambda b, t: (b, 0)),
                  pl.BlockSpec((1, kvh * dh, s_pad), lambda b, t: (b, 0, 0))],
        out_specs=pl.BlockSpec((tq, heads * dh), lambda b, t: (b * nt + t, 0)),
        out_shape=jax.ShapeDtypeStruct((batch * tq_total, heads * dh), BF16),
        scratch_shapes=[pltpu.VMEM((s_pad, tq), jnp.int32), pltpu.VMEM((dh, cols), F32),
                        pltpu.VMEM((8, tq), jnp.int32)],
        compiler_params=_cparams(2),
        name="dsa",
    )(iq, ik, iwt, q, k, vt)


def _mixer_a(xn, tabs, w_in, lams, subln_g, cache_k, cache_v, lam_init, dims):
    bp, tp, bs, ts, d, past = dims
    mp = bp * tp
    dh = d // (2 * A_HEADS)
    a_q = A_HEADS * 2 * dh
    a_kv = A_KV_HEADS * 2 * dh
    group = A_HEADS // A_KV_HEADS
    (q,) = _proj(xn, w_in[:, :a_q].astype(BF16), (BF16,), rope=tabs, out_scale=dh ** -0.5 * LOG2E)
    k32, k16 = _proj(xn, w_in[:, a_q:a_q + a_kv].astype(BF16), (F32, BF16), rope=tabs)
    v32, v16 = _proj(xn, w_in[:, a_q + a_kv:].astype(BF16), (F32, BF16))
    lams = tuple(x.reshape(1, dh).astype(F32) for x in lams)
    subln_g = subln_g.astype(F32)
    vt_p = jnp.swapaxes(v16[:mp].reshape(bp, tp, a_kv), 1, 2)
    o_p = _diff_prompt(q, k16, vt_p, lams, subln_g, batch=bp, seq=tp, kvh=A_KV_HEADS, group=group, dh=dh,
                       lam_init=lam_init)
    o_s = _diff_sample(q, k16, v16, cache_k.reshape(bs, past, a_kv), cache_v.reshape(bs, past, a_kv), lams,
                       subln_g.reshape(1, 2 * dh), batch=bs, ts=ts, kvh=A_KV_HEADS, group=group, dh=dh,
                       lam_init=lam_init, row0=mp)
    return (o_p, o_s), k32, v32


def _mixer_b(xn, tabs, w_in, cache_k, cache_v, cache_ik, dims):
    bp, tp, bs, ts, d, past = dims
    mp = bp * tp
    dh = d // B_HEADS
    o1 = B_HEADS * dh
    o2 = o1 + B_KV_HEADS * dh
    o3 = o2 + B_KV_HEADS * dh
    o4 = o3 + B_IDX_HEADS * B_IDX_DIM
    o5 = o4 + B_IDX_DIM
    kvw = B_KV_HEADS * dh
    (q,) = _proj(xn, w_in[:, :o1].astype(BF16), (BF16,), rope=tabs, out_scale=dh ** -0.5 * LOG2E)
    k32, k16 = _proj(xn, w_in[:, o1:o2].astype(BF16), (F32, BF16), rope=tabs)
    v32, v16 = _proj(xn, w_in[:, o2:o3].astype(BF16), (F32, BF16))
    (iq,) = _proj(xn, w_in[:, o3:o4].astype(BF16), (BF16,), rope=tabs)
    ik32, ik16 = _proj(xn, w_in[:, o4:o5].astype(BF16), (F32, BF16), rope=tabs)
    w_iw = jnp.pad(w_in[:, o5:], ((0, 0), (0, LANES - B_IDX_HEADS))).astype(BF16)
    (iw,) = _proj(xn, w_iw, (F32,))
    iwt = iw[:, :B_IDX_HEADS].T

    vt_p = jnp.swapaxes(v16[:mp].reshape(bp, tp, kvw), 1, 2)
    tq_p = _pick(tp, 256, LANES)
    o_p = _dsa(iq, ik16, iwt, q, k16, vt_p, batch=bp, tq_total=tp, tq=tq_p, kb=_pick(tp, 512, LANES),
               causal=True, s_real=tp, n_sel=min(B_TOPK_MAX, tp // 4))

    past = cache_k.shape[1]
    s_real = past + ts
    s_pad = -(-s_real // LANES) * LANES
    tq_s = -(-ts // LANES) * LANES

    def keys(cache, new):
        new = new[mp:].reshape(bs, ts, -1)
        kk = jnp.concatenate([cache.reshape(bs, past, -1).astype(BF16), new,
                              jnp.zeros((bs, s_pad - s_real, new.shape[-1]), BF16)], axis=1)
        return kk

    def queries(a):
        a = jnp.pad(a[mp:].reshape(bs, ts, -1), ((0, 0), (0, tq_s - ts), (0, 0)))
        return a.reshape(bs * tq_s, -1)

    kk = keys(cache_k, k16).reshape(bs * s_pad, kvw)
    ikk = keys(cache_ik, ik16).reshape(bs * s_pad, B_IDX_DIM)
    vt_s = jnp.swapaxes(keys(cache_v, v16), 1, 2)
    o_s = _dsa(queries(iq), ikk, queries(iw[:, :B_IDX_HEADS]).T, queries(q), kk, vt_s, batch=bs, tq_total=tq_s,
               tq=tq_s, kb=_pick(s_pad, 512, LANES), causal=False, s_real=s_real,
               n_sel=min(B_TOPK_MAX, s_real // 4))
    o_s = o_s.reshape(bs, tq_s, o1)[:, :ts].reshape(bs * ts, o1)
    return (o_p, o_s), k32, v32, ik32


def _mixer_c(xn, w_in, rel_bias, cache_k, cache_v, dims):
    bp, tp, bs, ts, d, past = dims
    mp = bp * tp
    dh = d // C_HEADS
    hw = C_HEADS * dh
    win = C_LEFT_CHUNKS * CHUNK
    (q,) = _proj(xn, w_in[:, :hw].astype(BF16), (BF16,), out_scale=dh ** -0.5 * LOG2E)
    kv32, kv16 = _proj(xn, w_in[:, hw:].astype(BF16), (F32, BF16))
    rel_bias = rel_bias.astype(F32) * LOG2E

    chunk_bias = _band_bias(rel_bias, win, CHUNK, 0, win + CHUNK)
    o_p = _band_prompt(q, kv16, chunk_bias, batch=bp, seq=tp, heads=C_HEADS, dh=dh, row0=0)

    w_c = cache_k.shape[1]
    o_s = _band_sample(q, kv16, cache_k.reshape(bs, w_c, hw), cache_v.reshape(bs, w_c, hw),
                       _band_bias(rel_bias, past, ts, past - w_c, w_c + ts),
                       batch=bs, ts=ts, heads=C_HEADS, dh=dh, row0=mp)
    return (o_p, o_s), kv32[:, :hw], kv32[:, hw:]


def kernel(x_prompt, x_sample, cache_a_k, cache_a_v, cache_b_k, cache_b_v, cache_b_idx_k, cache_c_k, cache_c_v,
           norm_mix_g, norm_ffn_g, norm_out_g, a_w_in, a_w_out, a_lam_q1, a_lam_k1, a_lam_q2, a_lam_k2,
           a_subln_g, b_w_in, b_w_out, c_w_in, c_w_out, c_rel_bias, ffn_w_gu, ffn_w_down):
    bp, tp, d = x_prompt.shape
    bs, ts, _ = x_sample.shape
    past = cache_a_k.shape[2]
    depth = norm_mix_g.shape[0]
    assert ts <= CHUNK and past % CHUNK == 0 and tp % CHUNK == 0
    mp, ms = bp * tp, bs * ts
    dims = (bp, tp, bs, ts, d, past)
    i32 = jnp.int32

    h, xn = _join_norm(x_prompt.reshape(mp, d), x_sample.reshape(ms, d), norm_mix_g[0])
    pos =jnp.concatenate([jnp.tile(jnp.arange(tp, dtype=i32), bp), jnp.tile(past + jnp.arange(ts, dtype=i32), bs)])
    tabs = _rope_tables(pos)
    w_gu16 = ffn_w_gu.astype(BF16)
    w_down16 = ffn_w_down.astype(BF16)

    def split(a, tail):
        return a[:mp].reshape((bp, tp) + tail), a[mp:].reshape((bs, ts) + tail)

    st = {name: [] for name in ("a_k", "a_v", "b_k", "b_v", "b_ik", "c_k", "c_v")}
    for i in range(depth):
        j = i // N_MIXERS
        kind = i % N_MIXERS
        if kind == 0:
            lam_init = 0.8 - 0.6 * math.exp(-0.3 * i)
            o, k, v = _mixer_a(xn, tabs, a_w_in[j], (a_lam_q1[j], a_lam_k1[j], a_lam_q2[j], a_lam_k2[j]),
                               a_subln_g[j], cache_a_k[j], cache_a_v[j], lam_init, dims)
            tail = cache_a_k.shape[3:]
            st["a_k"].append(split(k, tail))
            st["a_v"].append(split(v, tail))
            w_out = a_w_out[j]
        elif kind == 1:
            o, k, v, ik = _mixer_b(xn, tabs, b_w_in[j], cache_b_k[j], cache_b_v[j], cache_b_idx_k[j], dims)
            tail = cache_b_k.shape[3:]
            st["b_k"].append(split(k, tail))
            st["b_v"].append(split(v, tail))
            st["b_ik"].append(split(ik, cache_b_idx_k.shape[3:]))
            w_out = b_w_out[j]
        else:
            o, k, v = _mixer_c(xn, c_w_in[j], c_rel_bias[j], cache_c_k[j], cache_c_v[j], dims)
            tail = cache_c_k.shape[3:]
            keep = min(C_LEFT_CHUNKS * CHUNK, tp)
            for name, a, cache in (("c_k", k, cache_c_k[j]), ("c_v", v, cache_c_v[j])):
                a_p, a_s = split(a, tail)
                kk_s = jnp.concatenate([cache, a_s], axis=1)
                st[name].append((a_p[:, tp - keep:], kk_s[:, kk_s.shape[1] - cache.shape[1]:]))
            w_out = c_w_out[j]
        h = _out_proj(o[0], o[1], w_out.astype(BF16), h)
        if i + 1 < depth:
            h, xn = _ffn(h, norm_ffn_g[i], w_gu16, w_down16, i, g_next=norm_mix_g[i + 1])
        else:
            h = _ffn(h, norm_ffn_g[i], w_gu16, w_down16, i)

    y_p = _rmsnorm(h, norm_out_g, F32, 0, mp).reshape(bp, tp, d)
    y_s = _rmsnorm(h, norm_out_g, F32, mp, ms).reshape(bs, ts, d)
    stack = lambda name, g: jnp.stack([pair[g] for pair in st[name]], axis=0)
    names = ("a_k", "a_v", "b_k", "b_v", "b_ik", "c_k", "c_v")
    return (y_p, y_s) + tuple(stack(n, 0) for n in names) + tuple(stack(n, 1) for n in names)
```

```python
import functools
import math

import jax
import jax.numpy as jnp
import numpy as np
from jax import lax
from jax.experimental import pallas as pl
from jax.experimental.pallas import tpu as pltpu

F32 = jnp.float32
BF16 = jnp.bfloat16

CHUNK = 64
N_MIXERS = 3
EPS = 1e-6
ROPE_THETA = 500000.0
NEG = -1e30
A_HEADS = 16
A_KV_HEADS = 4
B_HEADS = 32
B_KV_HEADS = 4
B_IDX_HEADS = 32
B_IDX_DIM = 128
B_TOPK_MAX = 256
C_HEADS = 32
C_LEFT_CHUNKS = 8
C_REL_CLIP = 256

LANES = 128
VMEM_LIMIT = 56 * 1024 * 1024
FFN_VMEM_LIMIT = 58 * 1024 * 1024
INT_MIN = -2 ** 31
LOG2E = math.log2(math.e)


def _cparams(n_grid, vmem_limit=VMEM_LIMIT):
    return pltpu.CompilerParams(dimension_semantics=("arbitrary",) * n_grid,
                                vmem_limit_bytes=vmem_limit)


def _pick(n, target, mult=16):
    for t in range(min(n, target), 0, -1):
        if n % t == 0 and t % mult == 0:
            return t
    return n


def _rmsnorm_kernel(x_ref, g_ref, o_ref):
    x = x_ref[...]
    y = x * lax.rsqrt(jnp.mean(x * x, axis=-1, keepdims=True) + EPS)
    o_ref[...] = (y * g_ref[...]).astype(o_ref.dtype)


def _rmsnorm(x, g, out_dtype, row0=0, rows=None):
    d = x.shape[1]
    m = x.shape[0] if rows is None else rows
    tm = _pick(math.gcd(m, row0) if row0 else m, 512)
    b0 = row0 // tm
    return pl.pallas_call(
        _rmsnorm_kernel,
        grid=(m // tm,),
        in_specs=[pl.BlockSpec((tm, d), lambda i: (b0 + i, 0)),
                  pl.BlockSpec((1, d), lambda i: (0, 0))],
        out_specs=pl.BlockSpec((tm, d), lambda i: (i, 0)),
        out_shape=jax.ShapeDtypeStruct((m, d), out_dtype),
        compiler_params=_cparams(1),
        name="rmsnorm",
    )(x, g.reshape(1, d).astype(F32))


def _join_norm_kernel(xp_ref, xs_ref, g_ref, h_ref, xn_ref, *, n_p):
    i = pl.program_id(0)

    def emit(x_ref):
        x = x_ref[...]
        h_ref[...] = x
        y = x * lax.rsqrt(jnp.mean(x * x, axis=-1, keepdims=True) + EPS)
        xn_ref[...] = (y * g_ref[...]).astype(xn_ref.dtype)

    @pl.when(i < n_p)
    def _():
        emit(xp_ref)

    @pl.when(i >= n_p)
    def _():
        emit(xs_ref)


def _join_norm(x_p, x_s, g):
    (m_p, d), m_s = x_p.shape, x_s.shape[0]
    tm = _pick(math.gcd(m_p, m_s), 256)
    n_p = m_p // tm
    row = pl.BlockSpec((tm, d), lambda i: (i, 0))
    return pl.pallas_call(
        functools.partial(_join_norm_kernel, n_p=n_p),
        grid=((m_p + m_s) // tm,),
        in_specs=[pl.BlockSpec((tm, d), lambda i: (jnp.minimum(i, n_p - 1), 0)),
                  pl.BlockSpec((tm, d), lambda i: (jnp.maximum(i - n_p, 0), 0)),
                  pl.BlockSpec((1, d), lambda i: (0, 0))],
        out_specs=[row, row],
        out_shape=[jax.ShapeDtypeStruct((m_p + m_s, d), F32), jax.ShapeDtypeStruct((m_p + m_s, d), BF16)],
        compiler_params=_cparams(1),
        name="join_norm",
    )(x_p, x_s, g.reshape(1, d).astype(F32))


def _rope_tables(pos, dh=LANES):
    r = dh // 4
    half = r // 2
    inv = ROPE_THETA ** (-2.0 * jnp.arange(half, dtype=F32) / r)
    ang = pos.astype(F32)[:, None] * inv[None, :]
    cos, sin = jnp.cos(ang), jnp.sin(ang)
    rows = pos.shape[0]
    zeros = jnp.zeros((rows, dh - r), F32)
    zh = jnp.zeros((rows, half), F32)
    c = jnp.concatenate([cos, cos, jnp.ones((rows, dh - r), F32)], axis=1)
    sa = jnp.concatenate([zh, sin, zeros], axis=1)
    sb = jnp.concatenate([-sin, zh, zeros], axis=1)
    return c, sa, sb


def _rope_slab(x, c, sa, sb):
    return x * c + pltpu.roll(x, 16, 1) * sa + pltpu.roll(x, LANES - 16, 1) * sb


def _proj_kernel(*refs, rope, out_scale):
    x_ref, w_ref = refs[0], refs[1]
    outs = refs[2:]
    if rope:
        c_ref, sa_ref, sb_ref = refs[2:5]
        outs = refs[5:]
    acc = jnp.dot(x_ref[...], w_ref[...], preferred_element_type=F32)
    if rope:
        c, sa, sb = c_ref[...], sa_ref[...], sb_ref[...]
        tn = acc.shape[1]
        acc = jnp.concatenate(
            [_rope_slab(acc[:, s * LANES:(s + 1) * LANES], c, sa, sb) for s in range(tn // LANES)], axis=1)
    if out_scale != 1.0:
        acc = acc * out_scale
    for o in outs:
        o[...] = acc.astype(o.dtype)


def _proj(x, w, out_dtypes, rope=None, out_scale=1.0, tm_target=1024):
    m, k = x.shape
    n = w.shape[1]
    tn_target = 1024 if all(dt == BF16 for dt in out_dtypes) else 512
    tm = _pick(m, tm_target)
    tn = _pick(n, tn_target, LANES)
    in_specs = [pl.BlockSpec((tm, k), lambda i, j: (i, 0)),
                pl.BlockSpec((k, tn), lambda i, j: (0, j))]
    args = [x, w]
    if rope is not None:
        in_specs += [pl.BlockSpec((tm, LANES), lambda i, j: (i, 0))] * 3
        args += list(rope)
    outs = pl.pallas_call(
        functools.partial(_proj_kernel, rope=rope is not None, out_scale=out_scale),
        grid=(m // tm, n // tn),
        in_specs=in_specs,
        out_specs=[pl.BlockSpec((tm, tn), lambda i, j: (i, j)) for _ in out_dtypes],
        out_shape=[jax.ShapeDtypeStruct((m, n), dt) for dt in out_dtypes],
        compiler_params=_cparams(2),
        name="proj",
    )(*args)
    return outs


def _out_proj_kernel(xp_ref, xs_ref, w_ref, r_ref, o_ref, *, n_p):
    i = pl.program_id(0)

    @pl.when(i < n_p)
    def _():
        o_ref[...] = r_ref[...] + jnp.dot(xp_ref[...], w_ref[...], preferred_element_type=F32)

    @pl.when(i >= n_p)
    def _():
        o_ref[...] = r_ref[...] + jnp.dot(xs_ref[...], w_ref[...], preferred_element_type=F32)


def _out_proj(x_p, x_s, w, residual, tm_target=1024, tn_target=512):
    (m_p, k), m_s = x_p.shape, x_s.shape[0]
    n = w.shape[1]
    tm = _pick(math.gcd(m_p, m_s), tm_target)
    tn = _pick(n, tn_target, LANES)
    n_p = m_p // tm
    return pl.pallas_call(
        functools.partial(_out_proj_kernel, n_p=n_p),
        grid=((m_p + m_s) // tm, n // tn),
        in_specs=[pl.BlockSpec((tm, k), lambda i, j: (jnp.minimum(i, n_p - 1), 0)),
                  pl.BlockSpec((tm, k), lambda i, j: (jnp.maximum(i - n_p, 0), 0)),
                  pl.BlockSpec((k, tn), lambda i, j: (0, j)),
                  pl.BlockSpec((tm, tn), lambda i, j: (i, j))],
        out_specs=pl.BlockSpec((tm, tn), lambda i, j: (i, j)),
        out_shape=jax.ShapeDtypeStruct((m_p + m_s, n), F32),
        compiler_params=_cparams(2),
        name="out_proj",
    )(x_p, x_s, w, residual)


FFN_DOWN_CHUNKS = 4


def _ffn_kernel(*refs, next_norm):
    if next_norm:
        h_ref, g_ref, wg_ref, wu_ref, wd_ref, gn_ref, o_ref, xo_ref, xn_ref = refs
    else:
        h_ref, g_ref, wg_ref, wu_ref, wd_ref, o_ref, xn_ref = refs
    j = pl.program_id(1)

    @pl.when(j == 0)
    def _():
        x = h_ref[...]
        y = x * lax.rsqrt(jnp.mean(x * x, axis=-1, keepdims=True) + EPS)
        xn_ref[...] = (y * g_ref[...]).astype(BF16)
        o_ref[...] = x

    xn = xn_ref[...]
    gate = jnp.dot(xn, wg_ref[...], preferred_element_type=F32)
    up = jnp.dot(xn, wu_ref[...], preferred_element_type=F32)
    act = (gate * (1.0 / (1.0 + jnp.exp(-gate))) * up).astype(BF16)
    cw = o_ref.shape[1] // FFN_DOWN_CHUNKS
    for c in range(FFN_DOWN_CHUNKS):
        cs = slice(c * cw, (c + 1) * cw)
        o_ref[:, cs] += jnp.dot(act, wd_ref[:, cs], preferred_element_type=F32)

    if next_norm:
        @pl.when(j == pl.num_programs(1) - 1)
        def _():
            x = o_ref[...]
            y = x * lax.rsqrt(jnp.mean(x * x, axis=-1, keepdims=True) + EPS)
            xo_ref[...] = (y * gn_ref[...]).astype(xo_ref.dtype)


def _ffn(h, g, w_gu, w_down, layer, g_next=None, tm_target=512, tf=256):
    m, d = h.shape
    f = w_down.shape[1]
    tm = _pick(m, tm_target)
    nf = f // tf
    row = pl.BlockSpec((tm, d), lambda i, j: (i, 0))
    gain = pl.BlockSpec((1, d), lambda i, j: (0, 0))
    h_spec = (pl.BlockSpec((tm, d), lambda i, j: (i, 0), pipeline_mode=pl.Buffered(1)) if g_next is not None
              else row)
    in_specs = [h_spec,
                gain,
                pl.BlockSpec((None, d, tf), lambda i, j: (layer, 0, j)),
                pl.BlockSpec((None, d, tf), lambda i, j: (layer, 0, j + nf)),
                pl.BlockSpec((None, tf, d), lambda i, j: (layer, j, 0))]
    args = [h, g.reshape(1, d).astype(F32), w_gu, w_gu, w_down]
    out_specs, out_shape = row, jax.ShapeDtypeStruct((m, d), F32)
    if g_next is not None:
        in_specs.append(gain)
        args.append(g_next.reshape(1, d).astype(F32))
        out_specs, out_shape = [row, row], [out_shape, jax.ShapeDtypeStruct((m, d), BF16)]
    return pl.pallas_call(
        functools.partial(_ffn_kernel, next_norm=g_next is not None),
        grid=(m // tm, nf),
        in_specs=in_specs,
        out_specs=out_specs,
        out_shape=out_shape,
        scratch_shapes=[pltpu.VMEM((tm, d), BF16)],
        compiler_params=_cparams(2, FFN_VMEM_LIMIT),
        name="ffn",
    )(*args)


_DN_T = (((1,), (1,)), ((), ()))


BAND_HEADS_PER_STEP = 4


def _band_prompt_kernel(q_ref, kp_ref, kc_ref, vp_ref, vc_ref, bias_ref, o_ref, bias_scr, *, tq, dh):
    t = pl.program_id(2)
    hp = bias_ref.shape[0]

    @pl.when((pl.program_id(1) == 0) & (t == 0))
    def _():
        bias_scr[...] = jnp.full(bias_scr.shape, NEG, F32)
        width = bias_ref.shape[3]
        for u in range(hp):
            for cc in range(tq // CHUNK):
                c0 = (cc // 2) * LANES
                bias_scr[u, cc * CHUNK:(cc + 1) * CHUNK, c0:c0 + width] = bias_ref[u, cc % 2]

    col = lax.broadcasted_iota(jnp.int32, (1, 2 * tq), 1)
    before_start = jnp.where((col < tq) & (t == 0), NEG, 0.0)
    for u in range(hp):
        hs = slice(u * dh, (u + 1) * dh)
        k = jnp.concatenate([kp_ref[:, hs], kc_ref[:, hs]], axis=0)
        v = jnp.concatenate([vp_ref[:, hs], vc_ref[:, hs]], axis=0)
        s = lax.dot_general(q_ref[:, hs], k, _DN_T, preferred_element_type=F32) + bias_scr[u] + before_start
        m = jnp.max(s, axis=1, keepdims=True)
        p = jnp.exp2(s - m)
        l = jnp.sum(p, axis=1, keepdims=True)
        o = jnp.dot(p.astype(BF16), v, preferred_element_type=F32) / l
        o_ref[:, hs] = o.astype(o_ref.dtype)


def _band_prompt(q, kv, chunk_bias, *, batch, seq, heads, dh, row0):
    assert 2 * CHUNK == LANES
    hp = BAND_HEADS_PER_STEP
    tq = chunk_bias.shape[2] - CHUNK
    nt = seq // tq
    rb0 = row0 // tq
    nh = heads // hp
    pad = jnp.full((heads, CHUNK, CHUNK), NEG, F32)
    bias2 = jnp.stack([jnp.concatenate([chunk_bias, pad], axis=2), jnp.concatenate([pad, chunk_bias], axis=2)],
                      axis=1)
    cur = lambda h, b, t: (rb0 + b * nt + t, h)
    prev = lambda h, b, t: (rb0 + b * nt + jnp.maximum(t - 1, 0), h)
    cur_v = lambda h, b, t: (rb0 + b * nt + t, nh + h)
    prev_v = lambda h, b, t: (rb0 + b * nt + jnp.maximum(t - 1, 0), nh + h)
    blk = (tq, hp * dh)
    return pl.pallas_call(
        functools.partial(_band_prompt_kernel, tq=tq, dh=dh),
        grid=(nh, batch, nt),
        in_specs=[pl.BlockSpec(blk, cur),
                  pl.BlockSpec(blk, prev), pl.BlockSpec(blk, cur),
                  pl.BlockSpec(blk, prev_v), pl.BlockSpec(blk, cur_v),
                  pl.BlockSpec((hp, 2, CHUNK, tq + 2 * CHUNK), lambda h, b, t: (h, 0, 0, 0))],
        out_specs=pl.BlockSpec(blk, lambda h, b, t: (b * nt + t, h)),
        out_shape=jax.ShapeDtypeStruct((batch * seq, heads * dh), BF16),
        scratch_shapes=[pltpu.VMEM((hp, tq, 2 * tq), F32)],
        compiler_params=_cparams(3),
        name="band_prompt",
    )(q, kv, kv, kv, kv, bias2)


def _band_sample_kernel(q_ref, kc_ref, vc_ref, kn_ref, vn_ref, bias_ref, o_ref, *, w, dh):
    for u in range(bias_ref.shape[0]):
        hs = slice(u * dh, (u + 1) * dh)
        q = q_ref[:, hs]
        kc = kc_ref[0, :, hs].astype(BF16)
        vc = vc_ref[0, :, hs].astype(BF16)
        bias = bias_ref[u]
        sc = lax.dot_general(q, kc, _DN_T, preferred_element_type=F32) + bias[:, :w]
        sn = lax.dot_general(q, kn_ref[:, hs], _DN_T, preferred_element_type=F32) + bias[:, w:]
        m = jnp.maximum(jnp.max(sc, axis=1, keepdims=True), jnp.max(sn, axis=1, keepdims=True))
        pc = jnp.exp2(sc - m)
        pn = jnp.exp2(sn - m)
        l = jnp.sum(pc, axis=1, keepdims=True) + jnp.sum(pn, axis=1, keepdims=True)
        o = (jnp.dot(pc.astype(BF16), vc, preferred_element_type=F32)
             + jnp.dot(pn.astype(BF16), vn_ref[:, hs], preferred_element_type=F32)) / l
        o_ref[:, hs] = o.astype(o_ref.dtype)


def _band_sample(q, kv, cache_k, cache_v, bias, *, batch, ts, heads, dh, row0):
    hp = BAND_HEADS_PER_STEP
    w = cache_k.shape[1]
    rb0 = row0 // ts
    nh = heads // hp
    return pl.pallas_call(
        functools.partial(_band_sample_kernel, w=w, dh=dh),
        grid=(nh, batch),
        in_specs=[pl.BlockSpec((ts, hp * dh), lambda h, b: (rb0 + b, h)),
                  pl.BlockSpec((1, w, hp * dh), lambda h, b: (b, 0, h)),
                  pl.BlockSpec((1, w, hp * dh), lambda h, b: (b, 0, h)),
                  pl.BlockSpec((ts, hp * dh), lambda h, b: (rb0 + b, h)),
                  pl.BlockSpec((ts, hp * dh), lambda h, b: (rb0 + b, nh + h)),
                  pl.BlockSpec((hp, ts, w + ts), lambda h, b: (h, 0, 0))],
        out_specs=pl.BlockSpec((ts, hp * dh), lambda h, b: (b, h)),
        out_shape=jax.ShapeDtypeStruct((batch * ts, heads * dh), BF16),
        compiler_params=_cparams(2),
        name="band_sample",
    )(q, cache_k, cache_v, kv, kv, bias)


def _band_bias(rel_bias, q0, nq, k0, nk):
    diag = np.arange(nq + nk - 1) - (nq - 1)
    idx = np.clip(q0 - k0 - diag, -C_REL_CLIP, C_REL_CLIP) + C_REL_CLIP
    r = rel_bias[idx].T.astype(F32)
    bias = jnp.stack([r[:, nq - 1 - i:nq - 1 - i + nk] for i in range(nq)], axis=1)
    qpos = q0 + np.arange(nq)[:, None]
    kpos = k0 + np.arange(nk)[None, :]
    qch, kch = qpos // CHUNK, kpos // CHUNK
    ok = (kpos >= 0) & (kch <= qch) & (qch - kch <= C_LEFT_CHUNKS)
    return jnp.where(ok[None], bias, NEG)


def _diff_lambda(lq1_ref, lk1_ref, lq2_ref, lk2_ref, lam_init):
    s1 = jnp.sum(lq1_ref[...] * lk1_ref[...], axis=1, keepdims=True)
    s2 = jnp.sum(lq2_ref[...] * lk2_ref[...], axis=1, keepdims=True)
    return jnp.exp(s1) - jnp.exp(s2) + lam_init


def _stack_groups(q_ref, c, group, dh):
    return jnp.concatenate([q_ref[:, (g * 2 + c) * dh:(g * 2 + c + 1) * dh] for g in range(group)], axis=0)


def _diff_finish(o, g_ref, o_ref, *, group, tq, out_scale):
    y = o * lax.rsqrt(jnp.mean(o * o, axis=-1, keepdims=True) + EPS) * g_ref[...] * out_scale
    y = y.astype(o_ref.dtype)
    e = y.shape[1]
    for g in range(group):
        o_ref[:, g * e:(g + 1) * e] = y[g * tq:(g + 1) * tq]


def _diff_prompt_kernel(q_ref, k_ref, vt_ref, lq1_ref, lk1_ref, lq2_ref, lk2_ref, g_ref, o_ref,
                        acc_scr, s_scr, m_scr, l_scr, *, tq, kb, group, dh, lam_init):
    t = pl.program_id(2)
    rows = group * tq
    e = 2 * dh
    lam = _diff_lambda(lq1_ref, lk1_ref, lq2_ref, lk2_ref, lam_init)
    acc_scr[...] = jnp.zeros(acc_scr.shape, F32)
    m_scr[...] = jnp.full(m_scr.shape, NEG, F32)
    l_scr[...] = jnp.zeros(l_scr.shape, F32)
    n_blocks = ((t + 1) * tq + kb - 1) // kb
    n_free = n_blocks - 1

    def scores(i, slot):
        ks = pl.multiple_of(i * kb, kb)
        kblk = k_ref[pl.ds(ks, kb), :]
        for c in range(2):
            qc = _stack_groups(q_ref, c, group, dh)
            s_scr[slot, c] = lax.dot_general(kblk[:, c * dh:(c + 1) * dh], qc, _DN_T,
                                             preferred_element_type=F32)

    def accumulate(i, slot, masked):
        ks = pl.multiple_of(i * kb, kb)
        vtb = vt_ref[0, :, pl.ds(ks, kb)]
        if masked:
            qi = lax.broadcasted_iota(jnp.int32, (kb, rows), 1) % tq
            qend = ((t * tq + qi) // CHUNK + 1) * CHUNK
            kpos = ks + lax.broadcasted_iota(jnp.int32, (kb, rows), 0)
            bias = jnp.where(kpos < qend, 0.0, NEG)
        for c in range(2):
            s = s_scr[slot, c]
            if masked:
                s = s + bias
            m = m_scr[c]
            m_new = jnp.maximum(m, jnp.max(s, axis=0, keepdims=True))
            alpha = jnp.exp2(m - m_new)
            p = jnp.exp2(s - m_new)
            l_scr[c] = alpha * l_scr[c] + jnp.sum(p, axis=0, keepdims=True)
            m_scr[c] = m_new
            acc_scr[c] = alpha * acc_scr[c] + jnp.dot(vtb, p.astype(BF16), preferred_element_type=F32)

    scores(0, 0)

    def pair(j, carry):
        i = 2 * j
        scores(i + 1, 1)
        accumulate(i, 0, False)
        scores(i + 2, 0)
        accumulate(i + 1, 1, False)
        return carry

    lax.fori_loop(0, n_free // 2, pair, 0)
    i0 = (n_free // 2) * 2

    @pl.when(n_free % 2 == 1)
    def _():
        scores(i0 + 1, 1)
        accumulate(i0, 0, False)
        accumulate(i0 + 1, 1, True)

    @pl.when(n_free % 2 == 0)
    def _():
        accumulate(i0, 0, True)

    o = acc_scr[0] / l_scr[0] - lam * (acc_scr[1] / l_scr[1])
    y = o * lax.rsqrt(jnp.mean(o * o, axis=0, keepdims=True) + EPS) * (1.0 - lam_init)
    for g in range(group):
        o_ref[:, g * e:(g + 1) * e] = (y[:, g * tq:(g + 1) * tq].T * g_ref[...]).astype(o_ref.dtype)


def _lam_specs(n_grid):
    zero = (lambda *_: (0, 0))
    return [pl.BlockSpec((1, LANES), zero)] * 4


def _diff_prompt(q, k, vt, lams, subln_g, *, batch, seq, kvh, group, dh, lam_init, tq=LANES, kb=512):
    kb = _pick(seq, kb, LANES)
    nt = seq // tq
    rows = group * tq
    e = 2 * dh
    return pl.pallas_call(
        functools.partial(_diff_prompt_kernel, tq=tq, kb=kb, group=group, dh=dh, lam_init=lam_init),
        grid=(batch, kvh, nt),
        in_specs=[pl.BlockSpec((tq, group * e), lambda b, h, t: (b * nt + t, h)),
                  pl.BlockSpec((seq, e), lambda b, h, t: (b, h)),
                  pl.BlockSpec((1, e, seq), lambda b, h, t: (b, h, 0))]
                 + _lam_specs(3) + [pl.BlockSpec((1, e), lambda b, h, t: (0, 0))],
        out_specs=pl.BlockSpec((tq, group * e), lambda b, h, t: (b * nt + t, h)),
        out_shape=jax.ShapeDtypeStruct((batch * seq, kvh * group * e), BF16),
        scratch_shapes=[pltpu.VMEM((2, e, rows), F32), pltpu.VMEM((2, 2, kb, rows), F32),
                        pltpu.VMEM((2, 1, rows), F32), pltpu.VMEM((2, 1, rows), F32)],
        compiler_params=_cparams(3),
        name="diff_prompt",
    )(q, k, vt, *lams, subln_g.reshape(1, e))


def _diff_sample_kernel(q_ref, kc_ref, vc_ref, kn_ref, vn_ref, lq1_ref, lk1_ref, lq2_ref, lk2_ref, g_ref, o_ref,
                        *, ts, group, dh, lam_init):
    lam = _diff_lambda(lq1_ref, lk1_ref, lq2_ref, lk2_ref, lam_init)
    kc = kc_ref[0].astype(BF16)
    vc = vc_ref[0].astype(BF16)
    kn = kn_ref[...]
    a_c, a_n = [], []
    for c in range(2):
        qc = _stack_groups(q_ref, c, group, dh)
        sc = lax.dot_general(qc, kc[:, c * dh:(c + 1) * dh], _DN_T, preferred_element_type=F32)
        sn = lax.dot_general(qc, kn[:, c * dh:(c + 1) * dh], _DN_T, preferred_element_type=F32)
        m = jnp.maximum(jnp.max(sc, axis=1, keepdims=True), jnp.max(sn, axis=1, keepdims=True))
        pc = jnp.exp2(sc - m)
        pn = jnp.exp2(sn - m)
        l = jnp.sum(pc, axis=1, keepdims=True) + jnp.sum(pn, axis=1, keepdims=True)
        a_c.append(pc / l)
        a_n.append(pn / l)
    ac = (a_c[0] - lam * a_c[1]).astype(BF16)
    an = (a_n[0] - lam * a_n[1]).astype(BF16)
    o = (jnp.dot(ac, vc, preferred_element_type=F32) + jnp.dot(an, vn_ref[...], preferred_element_type=F32))
    _diff_finish(o, g_ref, o_ref, group=group, tq=ts, out_scale=1.0 - lam_init)


def _diff_sample(q, k, v, cache_k, cache_v, lams, subln_g, *, batch, ts, kvh, group, dh, lam_init, row0):
    past = cache_k.shape[1]
    rb0 = row0 // ts
    e = 2 * dh
    return pl.pallas_call(
        functools.partial(_diff_sample_kernel, ts=ts, group=group, dh=dh, lam_init=lam_init),
        grid=(batch, kvh),
        in_specs=[pl.BlockSpec((ts, group * e), lambda b, h: (rb0 + b, h)),
                  pl.BlockSpec((1, past, e), lambda b, h: (b, 0, h)),
                  pl.BlockSpec((1, past, e), lambda b, h: (b, 0, h)),
                  pl.BlockSpec((ts, e), lambda b, h: (rb0 + b, h)),
                  pl.BlockSpec((ts, e), lambda b, h: (rb0 + b, h))]
                 + _lam_specs(2) + [pl.BlockSpec((1, e), lambda b, h: (0, 0))],
        out_specs=pl.BlockSpec((ts, group * e), lambda b, h: (b, h)),
        out_shape=jax.ShapeDtypeStruct((batch * ts, kvh * group * e), BF16),
        compiler_params=_cparams(2),
        name="diff_sample",
    )(q, cache_k, cache_v, k, v, *lams, subln_g)


def _dsa_kernel(iq_ref, ik_ref, iwt_ref, q_ref, k_ref, vt_ref, o_ref, key_scr, acc_scr, cut_scr,
                *, tq, kb, causal, s_real, n_sel, n_idx, kvh, group, dh, iscale):
    t = pl.program_id(1)
    s_pad = ik_ref.shape[0]
    lane = lax.broadcasted_iota(jnp.int32, (1, tq), 1)
    if causal:
        qend = ((t * tq + lane) // CHUNK + 1) * CHUNK
        n_blocks = ((t + 1) * tq + kb - 1) // kb
    else:
        qend = jnp.full((1, tq), s_real, jnp.int32)
        n_blocks = s_pad // kb

    w = iwt_ref[...] * iscale

    def score_body(i, carry):
        ks = pl.multiple_of(i * kb, kb)
        ikb = ik_ref[pl.ds(ks, kb), :]
        sc = jnp.zeros((kb, tq), F32)
        for n in range(n_idx):
            logit = lax.dot_general(ikb, iq_ref[:, n * dh:(n + 1) * dh], _DN_T, preferred_element_type=F32)
            sc = sc + w[n:n + 1, :] * jnp.maximum(logit, 0.0)
        bits = lax.bitcast_convert_type(sc, jnp.int32)
        key = jnp.where(bits < 0, bits ^ 0x7FFFFFFF, bits)
        kpos = ks + lax.broadcasted_iota(jnp.int32, (kb, tq), 0)
        key_scr[pl.ds(ks, kb), :] = jnp.where(kpos < qend, key, INT_MIN)
        return carry

    lax.fori_loop(0, n_blocks, score_body, 0)

    def count(pred):
        def body(i, acc):
            ks = pl.multiple_of(i * kb, kb)
            hit = pred(key_scr[pl.ds(ks, kb), :], ks)
            return acc + jnp.sum(hit.astype(jnp.int32), axis=0, keepdims=True)
        return lax.fori_loop(0, n_blocks, body, jnp.zeros((1, tq), jnp.int32))

    def count_ge(cand):
        return count(lambda key, ks: key >= cand)

    zero = jnp.zeros((1, tq), jnp.int32)
    n_zero = count_ge(zero)
    thr = jnp.where(n_zero >= n_sel, zero, jnp.full((1, tq), INT_MIN, jnp.int32))

    def bit_body(i, st):
        thr, n_ge = st
        cand = thr | lax.shift_left(jnp.int32(1), 30 - i)
        n_cand = count_ge(cand)
        ok = n_cand >= n_sel
        return jnp.where(ok, cand, thr), jnp.where(ok, n_cand, n_ge)

    thr, n_ge = lax.fori_loop(0, 31, bit_body, (thr, n_zero))

    n_gt = count(lambda key, ks: key > thr)
    need = n_sel - n_gt
    surplus = (n_ge - n_gt > need) & (thr > INT_MIN)
    cut_scr[...] = jnp.where(thr > INT_MIN, s_pad, 0) + jnp.zeros(cut_scr.shape, jnp.int32)

    @pl.when(jnp.max(surplus.astype(jnp.int32)) > 0)
    def _():
        def tied_before(cand):
            def pred(key, ks):
                idx = ks + lax.broadcasted_iota(jnp.int32, (kb, tq), 0)
                return (key == thr) & (idx < cand)
            return count(pred)

        def idx_body(i, c):
            cand = c | lax.shift_left(jnp.int32(1), (s_pad - 1).bit_length() - 1 - i)
            return jnp.where(tied_before(cand) < need, cand, c)

        c = lax.fori_loop(0, (s_pad - 1).bit_length(), idx_body, zero)
        cut_scr[...] = jnp.where(surplus, c + 1, cut_scr[0:1, :]) + jnp.zeros(cut_scr.shape, jnp.int32)

    cut = cut_scr[0:1, :]

    def bias_body(i, carry):
        ks = pl.multiple_of(i * kb, kb)
        key = key_scr[pl.ds(ks, kb), :]
        idx = ks + lax.broadcasted_iota(jnp.int32, (kb, tq), 0)
        sel = (key > thr) | ((key == thr) & (idx < cut))
        key_scr[pl.ds(ks, kb), :] = lax.bitcast_convert_type(jnp.where(sel, 0.0, NEG), jnp.int32)
        return carry

    lax.fori_loop(0, n_blocks, bias_body, 0)

    cols = group * tq
    for h in range(kvh):
        acc_scr[...] = jnp.zeros(acc_scr.shape, F32)

        def kv_body(i, st, h=h):
            m, l = st
            ks = pl.multiple_of(i * kb, kb)
            kblk = k_ref[pl.ds(ks, kb), h * dh:(h + 1) * dh]
            qh = jnp.concatenate([q_ref[:, (h * group + g) * dh:(h * group + g + 1) * dh] for g in range(group)],
                                 axis=0)
            bias = lax.bitcast_convert_type(key_scr[pl.ds(ks, kb), :], F32)
            s = lax.dot_general(kblk, qh, _DN_T, preferred_element_type=F32)
            s = s + jnp.concatenate([bias] * group, axis=1)
            m_new = jnp.maximum(m, jnp.max(s, axis=0, keepdims=True))
            alpha = jnp.exp2(m - m_new)
            p = jnp.exp2(s - m_new)
            l = alpha * l + jnp.sum(p, axis=0, keepdims=True)
            vtb = vt_ref[0, h * dh:(h + 1) * dh, pl.ds(ks, kb)]
            acc_scr[...] = alpha * acc_scr[...] + jnp.dot(vtb, p.astype(BF16), preferred_element_type=F32)
            return m_new, l

        _, l = lax.fori_loop(0, n_blocks, kv_body, (jnp.full((1, cols), NEG, F32), jnp.zeros((1, cols), F32)))
        out = acc_scr[...] / l
        for g in range(group):
            c0 = (h * group + g) * dh
            o_ref[:, c0:c0 + dh] = out[:, g * tq:(g + 1) * tq].T.astype(o_ref.dtype)


def _dsa(iq, ik, iwt, q, k, vt, *, batch, tq_total, tq, kb, causal, s_real, n_sel):
    dh = ik.shape[1]
    s_pad = vt.shape[2]
    kvh = k.shape[1] // dh
    heads = q.shape[1] // dh
    n_idx = iq.shape[1] // dh
    nt = tq_total // tq
    cols = heads // kvh * tq
    return pl.pallas_call(
        functools.partial(_dsa_kernel, tq=tq, kb=kb, causal=causal, s_real=s_real, n_sel=n_sel, n_idx=n_idx,
                          kvh=kvh, group=heads // kvh, dh=dh, iscale=(n_idx * dh) ** -0.5),
        grid=(batch, nt),
        in_specs=[pl.BlockSpec((tq, n_idx * dh), lambda b, t: (b * nt + t, 0)),
                  pl.BlockSpec((s_pad, dh), lambda b, t: (b, 0)),
                  pl.BlockSpec((n_idx, tq), lambda b, t: (0, b * nt + t)),
                  pl.BlockSpec((tq, heads * dh), lambda b, t: (b * nt + t, 0)),
                  pl.BlockSpec((s_pad, kvh * dh), lambda b, t: (b, 0)),
                  pl.BlockSpec((1, kvh * dh, s_pad), lambda b, t: (b, 0, 0))],
        out_specs=pl.BlockSpec((tq, heads * dh), lambda b, t: (b * nt + t, 0)),
        out_shape=jax.ShapeDtypeStruct((batch * tq_total, heads * dh), BF16),
        scratch_shapes=[pltpu.VMEM((s_pad, tq), jnp.int32), pltpu.VMEM((dh, cols), F32),
                        pltpu.VMEM((8, tq), jnp.int32)],
        compiler_params=_cparams(2),
        name="dsa",
    )(iq, ik, iwt, q, k, vt)


def _mixer_a(xn, tabs, w_in, lams, subln_g, cache_k, cache_v, lam_init, dims):
    bp, tp, bs, ts, d, past = dims
    mp = bp * tp
    dh = d // (2 * A_HEADS)
    a_q = A_HEADS * 2 * dh
    a_kv = A_KV_HEADS * 2 * dh
    group = A_HEADS // A_KV_HEADS
    (q,) = _proj(xn, w_in[:, :a_q].astype(BF16), (BF16,), rope=tabs, out_scale=dh ** -0.5 * LOG2E)
    k32, k16 = _proj(xn, w_in[:, a_q:a_q + a_kv].astype(BF16), (F32, BF16), rope=tabs)
    v32, v16 = _proj(xn, w_in[:, a_q + a_kv:].astype(BF16), (F32, BF16))
    lams = tuple(x.reshape(1, dh).astype(F32) for x in lams)
    subln_g = subln_g.astype(F32)
    vt_p = jnp.swapaxes(v16[:mp].reshape(bp, tp, a_kv), 1, 2)
    o_p = _diff_prompt(q, k16, vt_p, lams, subln_g, batch=bp, seq=tp, kvh=A_KV_HEADS, group=group, dh=dh,
                       lam_init=lam_init)
    o_s = _diff_sample(q, k16, v16, cache_k.reshape(bs, past, a_kv), cache_v.reshape(bs, past, a_kv), lams,
                       subln_g.reshape(1, 2 * dh), batch=bs, ts=ts, kvh=A_KV_HEADS, group=group, dh=dh,
                       lam_init=lam_init, row0=mp)
    return (o_p, o_s), k32, v32


def _mixer_b(xn, tabs, w_in, cache_k, cache_v, cache_ik, dims):
    bp, tp, bs, ts, d, past = dims
    mp = bp * tp
    dh = d // B_HEADS
    o1 = B_HEADS * dh
    o2 = o1 + B_KV_HEADS * dh
    o3 = o2 + B_KV_HEADS * dh
    o4 = o3 + B_IDX_HEADS * B_IDX_DIM
    o5 = o4 + B_IDX_DIM
    kvw = B_KV_HEADS * dh
    (q,) = _proj(xn, w_in[:, :o1].astype(BF16), (BF16,), rope=tabs, out_scale=dh ** -0.5 * LOG2E)
    k32, k16 = _proj(xn, w_in[:, o1:o2].astype(BF16), (F32, BF16), rope=tabs)
    v32, v16 = _proj(xn, w_in[:, o2:o3].astype(BF16), (F32, BF16))
    (iq,) = _proj(xn, w_in[:, o3:o4].astype(BF16), (BF16,), rope=tabs)
    ik32, ik16 = _proj(xn, w_in[:, o4:o5].astype(BF16), (F32, BF16), rope=tabs)
    w_iw = jnp.pad(w_in[:, o5:], ((0, 0), (0, LANES - B_IDX_HEADS))).astype(BF16)
    (iw,) = _proj(xn, w_iw, (F32,))
    iwt = iw[:, :B_IDX_HEADS].T

    vt_p = jnp.swapaxes(v16[:mp].reshape(bp, tp, kvw), 1, 2)
    tq_p = _pick(tp, 256, LANES)
    o_p = _dsa(iq, ik16, iwt, q, k16, vt_p, batch=bp, tq_total=tp, tq=tq_p, kb=_pick(tp, 512, LANES),
               causal=True, s_real=tp, n_sel=min(B_TOPK_MAX, tp // 4))

    past = cache_k.shape[1]
    s_real = past + ts
    s_pad = -(-s_real // LANES) * LANES
    tq_s = -(-ts // LANES) * LANES

    def keys(cache, new):
        new = new[mp:].reshape(bs, ts, -1)
        kk = jnp.concatenate([cache.reshape(bs, past, -1).astype(BF16), new,
                              jnp.zeros((bs, s_pad - s_real, new.shape[-1]), BF16)], axis=1)
        return kk

    def queries(a):
        a = jnp.pad(a[mp:].reshape(bs, ts, -1), ((0, 0), (0, tq_s - ts), (0, 0)))
        return a.reshape(bs * tq_s, -1)

    kk = keys(cache_k, k16).reshape(bs * s_pad, kvw)
    ikk = keys(cache_ik, ik16).reshape(bs * s_pad, B_IDX_DIM)
    vt_s = jnp.swapaxes(keys(cache_v, v16), 1, 2)
    o_s = _dsa(queries(iq), ikk, queries(iw[:, :B_IDX_HEADS]).T, queries(q), kk, vt_s, batch=bs, tq_total=tq_s,
               tq=tq_s, kb=_pick(s_pad, 512, LANES), causal=False, s_real=s_real,
               n_sel=min(B_TOPK_MAX, s_real // 4))
    o_s = o_s.reshape(bs, tq_s, o1)[:, :ts].reshape(bs * ts, o1)
    return (o_p, o_s), k32, v32, ik32


def _mixer_c(xn, w_in, rel_bias, cache_k, cache_v, dims):
    bp, tp, bs, ts, d, past = dims
    mp = bp * tp
    dh = d // C_HEADS
    hw = C_HEADS * dh
    win = C_LEFT_CHUNKS * CHUNK
    (q,) = _proj(xn, w_in[:, :hw].astype(BF16), (BF16,), out_scale=dh ** -0.5 * LOG2E)
    kv32, kv16 = _proj(xn, w_in[:, hw:].astype(BF16), (F32, BF16))
    rel_bias = rel_bias.astype(F32) * LOG2E

    chunk_bias = _band_bias(rel_bias, win, CHUNK, 0, win + CHUNK)
    o_p = _band_prompt(q, kv16, chunk_bias, batch=bp, seq=tp, heads=C_HEADS, dh=dh, row0=0)

    w_c = cache_k.shape[1]
    o_s = _band_sample(q, kv16, cache_k.reshape(bs, w_c, hw), cache_v.reshape(bs, w_c, hw),
                       _band_bias(rel_bias, past, ts, past - w_c, w_c + ts),
                       batch=bs, ts=ts, heads=C_HEADS, dh=dh, row0=mp)
    return (o_p, o_s), kv32[:, :hw], kv32[:, hw:]


def kernel(x_prompt, x_sample, cache_a_k, cache_a_v, cache_b_k, cache_b_v, cache_b_idx_k, cache_c_k, cache_c_v,
           norm_mix_g, norm_ffn_g, norm_out_g, a_w_in, a_w_out, a_lam_q1, a_lam_k1, a_lam_q2, a_lam_k2,
           a_subln_g, b_w_in, b_w_out, c_w_in, c_w_out, c_rel_bias, ffn_w_gu, ffn_w_down):
    bp, tp, d = x_prompt.shape
    bs, ts, _ = x_sample.shape
    past = cache_a_k.shape[2]
    depth = norm_mix_g.shape[0]
    assert ts <= CHUNK and past % CHUNK == 0 and tp % CHUNK == 0
    mp, ms = bp * tp, bs * ts
    dims = (bp, tp, bs, ts, d, past)
    i32 = jnp.int32

    h, xn = _join_norm(x_prompt.reshape(mp, d), x_sample.reshape(ms, d), norm_mix_g[0])
    pos =jnp.concatenate([jnp.tile(jnp.arange(tp, dtype=i32), bp), jnp.tile(past + jnp.arange(ts, dtype=i32), bs)])
    tabs = _rope_tables(pos)
    w_gu16 = ffn_w_gu.astype(BF16)
    w_down16 = ffn_w_down.astype(BF16)

    def split(a, tail):
        return a[:mp].reshape((bp, tp) + tail), a[mp:].reshape((bs, ts) + tail)

    st = {name: [] for name in ("a_k", "a_v", "b_k", "b_v", "b_ik", "c_k", "c_v")}
    for i in range(depth):
        j = i // N_MIXERS
        kind = i % N_MIXERS
        if kind == 0:
            lam_init = 0.8 - 0.6 * math.exp(-0.3 * i)
            o, k, v = _mixer_a(xn, tabs, a_w_in[j], (a_lam_q1[j], a_lam_k1[j], a_lam_q2[j], a_lam_k2[j]),
                               a_subln_g[j], cache_a_k[j], cache_a_v[j], lam_init, dims)
            tail = cache_a_k.shape[3:]
            st["a_k"].append(split(k, tail))
            st["a_v"].append(split(v, tail))
            w_out = a_w_out[j]
        elif kind == 1:
            o, k, v, ik = _mixer_b(xn, tabs, b_w_in[j], cache_b_k[j], cache_b_v[j], cache_b_idx_k[j], dims)
            tail = cache_b_k.shape[3:]
            st["b_k"].append(split(k, tail))
            st["b_v"].append(split(v, tail))
            st["b_ik"].append(split(ik, cache_b_idx_k.shape[3:]))
            w_out = b_w_out[j]
        else:
            o, k, v = _mixer_c(xn, c_w_in[j], c_rel_bias[j], cache_c_k[j], cache_c_v[j], dims)
            tail = cache_c_k.shape[3:]
            keep = min(C_LEFT_CHUNKS * CHUNK, tp)
            for name, a, cache in (("c_k", k, cache_c_k[j]), ("c_v", v, cache_c_v[j])):
                a_p, a_s = split(a, tail)
                kk_s = jnp.concatenate([cache, a_s], axis=1)
                st[name].append((a_p[:, tp - keep:], kk_s[:, kk_s.shape[1] - cache.shape[1]:]))
            w_out = c_w_out[j]
        h = _out_proj(o[0], o[1], w_out.astype(BF16), h)
        if i + 1 < depth:
            h, xn = _ffn(h, norm_ffn_g[i], w_gu16, w_down16, i, g_next=norm_mix_g[i + 1])
        else:
            h = _ffn(h, norm_ffn_g[i], w_gu16, w_down16, i)

    y_p = _rmsnorm(h, norm_out_g, F32, 0, mp).reshape(bp, tp, d)
    y_s = _rmsnorm(h, norm_out_g, F32, mp, ms).reshape(bs, ts, d)
    stack = lambda name, g: jnp.stack([pair[g] for pair in st[name]], axis=0)
    names = ("a_k", "a_v", "b_k", "b_v", "b_ik", "c_k", "c_v")
    return (y_p, y_s) + tuple(stack(n, 0) for n in names) + tuple(stack(n, 1) for n in names)
```

```python
import functools
import math

import jax
import jax.numpy as jnp
import numpy as np
from jax import lax
from jax.experimental import pallas as pl
from jax.experimental.pallas import tpu as pltpu

F32 = jnp.float32
BF16 = jnp.bfloat16

CHUNK = 64
N_MIXERS = 3
EPS = 1e-6
ROPE_THETA = 500000.0
NEG = -1e30
A_HEADS = 16
A_KV_HEADS = 4
B_HEADS = 32
B_KV_HEADS = 4
B_IDX_HEADS = 32
B_IDX_DIM = 128
B_TOPK_MAX = 256
C_HEADS = 32
C_LEFT_CHUNKS = 8
C_REL_CLIP = 256

LANES = 128
VMEM_LIMIT = 56 * 1024 * 1024
FFN_VMEM_LIMIT = 58 * 1024 * 1024
INT_MIN = -2 ** 31
LOG2E = math.log2(math.e)


def _cparams(n_grid, vmem_limit=VMEM_LIMIT):
    return pltpu.CompilerParams(dimension_semantics=("arbitrary",) * n_grid,
                                vmem_limit_bytes=vmem_limit)


def _pick(n, target, mult=16):
    for t in range(min(n, target), 0, -1):
        if n % t == 0 and t % mult == 0:
            return t
    return n


def _rmsnorm_kernel(x_ref, g_ref, o_ref):
    x = x_ref[...]
    y = x * lax.rsqrt(jnp.mean(x * x, axis=-1, keepdims=True) + EPS)
    o_ref[...] = (y * g_ref[...]).astype(o_ref.dtype)


def _rmsnorm(x, g, out_dtype, row0=0, rows=None):
    d = x.shape[1]
    m = x.shape[0] if rows is None else rows
    tm = _pick(math.gcd(m, row0) if row0 else m, 512)
    b0 = row0 // tm
    return pl.pallas_call(
        _rmsnorm_kernel,
        grid=(m // tm,),
        in_specs=[pl.BlockSpec((tm, d), lambda i: (b0 + i, 0)),
                  pl.BlockSpec((1, d), lambda i: (0, 0))],
        out_specs=pl.BlockSpec((tm, d), lambda i: (i, 0)),
        out_shape=jax.ShapeDtypeStruct((m, d), out_dtype),
        compiler_params=_cparams(1),
        name="rmsnorm",
    )(x, g.reshape(1, d).astype(F32))


def _join_norm_kernel(xp_ref, xs_ref, g_ref, h_ref, xn_ref, *, n_p):
    i = pl.program_id(0)

    def emit(x_ref):
        x = x_ref[...]
        h_ref[...] = x
        y = x * lax.rsqrt(jnp.mean(x * x, axis=-1, keepdims=True) + EPS)
        xn_ref[...] = (y * g_ref[...]).astype(xn_ref.dtype)

    @pl.when(i < n_p)
    def _():
        emit(xp_ref)

    @pl.when(i >= n_p)
    def _():
        emit(xs_ref)


def _join_norm(x_p, x_s, g):
    (m_p, d), m_s = x_p.shape, x_s.shape[0]
    tm = _pick(math.gcd(m_p, m_s), 256)
    n_p = m_p // tm
    row = pl.BlockSpec((tm, d), lambda i: (i, 0))
    return pl.pallas_call(
        functools.partial(_join_norm_kernel, n_p=n_p),
        grid=((m_p + m_s) // tm,),
        in_specs=[pl.BlockSpec((tm, d), lambda i: (jnp.minimum(i, n_p - 1), 0)),
                  pl.BlockSpec((tm, d), lambda i: (jnp.maximum(i - n_p, 0), 0)),
                  pl.BlockSpec((1, d), lambda i: (0, 0))],
        out_specs=[row, row],
        out_shape=[jax.ShapeDtypeStruct((m_p + m_s, d), F32), jax.ShapeDtypeStruct((m_p + m_s, d), BF16)],
        compiler_params=_cparams(1),
        name="join_norm",
    )(x_p, x_s, g.reshape(1, d).astype(F32))


def _rope_tables(pos, dh=LANES):
    r = dh // 4
    half = r // 2
    inv = ROPE_THETA ** (-2.0 * jnp.arange(half, dtype=F32) / r)
    ang = pos.astype(F32)[:, None] * inv[None, :]
    cos, sin = jnp.cos(ang), jnp.sin(ang)
    rows = pos.shape[0]
    zeros = jnp.zeros((rows, dh - r), F32)
    zh = jnp.zeros((rows, half), F32)
    c = jnp.concatenate([cos, cos, jnp.ones((rows, dh - r), F32)], axis=1)
    sa = jnp.concatenate([zh, sin, zeros], axis=1)
    sb = jnp.concatenate([-sin, zh, zeros], axis=1)
    return c, sa, sb


def _rope_slab(x, c, sa, sb):
    return x * c + pltpu.roll(x, 16, 1) * sa + pltpu.roll(x, LANES - 16, 1) * sb


def _proj_kernel(*refs, rope, out_scale):
    x_ref, w_ref = refs[0], refs[1]
    outs = refs[2:]
    if rope:
        c_ref, sa_ref, sb_ref = refs[2:5]
        outs = refs[5:]
    acc = jnp.dot(x_ref[...], w_ref[...], preferred_element_type=F32)
    if rope:
        c, sa, sb = c_ref[...], sa_ref[...], sb_ref[...]
        tn = acc.shape[1]
        acc = jnp.concatenate(
            [_rope_slab(acc[:, s * LANES:(s + 1) * LANES], c, sa, sb) for s in range(tn // LANES)], axis=1)
    if out_scale != 1.0:
        acc = acc * out_scale
    for o in outs:
        o[...] = acc.astype(o.dtype)


def _proj(x, w, out_dtypes, rope=None, out_scale=1.0, tm_target=1024):
    m, k = x.shape
    n = w.shape[1]
    tn_target = 1024 if all(dt == BF16 for dt in out_dtypes) else 512
    tm = _pick(m, tm_target)
    tn = _pick(n, tn_target, LANES)
    in_specs = [pl.BlockSpec((tm, k), lambda i, j: (i, 0)),
                pl.BlockSpec((k, tn), lambda i, j: (0, j))]
    args = [x, w]
    if rope is not None:
        in_specs += [pl.BlockSpec((tm, LANES), lambda i, j: (i, 0))] * 3
        args += list(rope)
    outs = pl.pallas_call(
        functools.partial(_proj_kernel, rope=rope is not None, out_scale=out_scale),
        grid=(m // tm, n // tn),
        in_specs=in_specs,
        out_specs=[pl.BlockSpec((tm, tn), lambda i, j: (i, j)) for _ in out_dtypes],
        out_shape=[jax.ShapeDtypeStruct((m, n), dt) for dt in out_dtypes],
        compiler_params=_cparams(2),
        name="proj",
    )(*args)
    return outs


def _out_proj_kernel(xp_ref, xs_ref, w_ref, r_ref, o_ref, *, n_p):
    i = pl.program_id(0)

    @pl.when(i < n_p)
    def _():
        o_ref[...] = r_ref[...] + jnp.dot(xp_ref[...], w_ref[...], preferred_element_type=F32)

    @pl.when(i >= n_p)
    def _():
        o_ref[...] = r_ref[...] + jnp.dot(xs_ref[...], w_ref[...], preferred_element_type=F32)


def _out_proj(x_p, x_s, w, residual, tm_target=1024, tn_target=512):
    (m_p, k), m_s = x_p.shape, x_s.shape[0]
    n = w.shape[1]
    tm = _pick(math.gcd(m_p, m_s), tm_target)
    tn = _pick(n, tn_target, LANES)
    n_p = m_p // tm
    return pl.pallas_call(
        functools.partial(_out_proj_kernel, n_p=n_p),
        grid=((m_p + m_s) // tm, n // tn),
        in_specs=[pl.BlockSpec((tm, k), lambda i, j: (jnp.minimum(i, n_p - 1), 0)),
                  pl.BlockSpec((tm, k), lambda i, j: (jnp.maximum(i - n_p, 0), 0)),
                  pl.BlockSpec((k, tn), lambda i, j: (0, j)),
                  pl.BlockSpec((tm, tn), lambda i, j: (i, j))],
        out_specs=pl.BlockSpec((tm, tn), lambda i, j: (i, j)),
        out_shape=jax.ShapeDtypeStruct((m_p + m_s, n), F32),
        compiler_params=_cparams(2),
        name="out_proj",
    )(x_p, x_s, w, residual)


FFN_DOWN_CHUNKS = 4


def _ffn_kernel(h_ref, g_ref, wg_ref, wu_ref, wd_ref, o_ref, xn_ref):
    j = pl.program_id(1)

    @pl.when(j == 0)
    def _():
        x = h_ref[...]
        y = x * lax.rsqrt(jnp.mean(x * x, axis=-1, keepdims=True) + EPS)
        xn_ref[...] = (y * g_ref[...]).astype(BF16)
        o_ref[...] = x

    xn = xn_ref[...]
    gate = jnp.dot(xn, wg_ref[...], preferred_element_type=F32)
    up = jnp.dot(xn, wu_ref[...], preferred_element_type=F32)
    act = (gate * (1.0 / (1.0 + jnp.exp(-gate))) * up).astype(BF16)
    cw = o_ref.shape[1] // FFN_DOWN_CHUNKS
    for c in range(FFN_DOWN_CHUNKS):
        cs = slice(c * cw, (c + 1) * cw)
        o_ref[:, cs] += jnp.dot(act, wd_ref[:, cs], preferred_element_type=F32)


def _ffn(h, g, w_gu, w_down, layer, tm_target=512, tf=256):
    m, d = h.shape
    f = w_down.shape[1]
    tm = _pick(m, tm_target)
    nf = f // tf
    row = pl.BlockSpec((tm, d), lambda i, j: (i, 0))
    return pl.pallas_call(
        _ffn_kernel,
        grid=(m // tm, nf),
        in_specs=[row,
                  pl.BlockSpec((1, d), lambda i, j: (0, 0)),
                  pl.BlockSpec((None, d, tf), lambda i, j: (layer, 0, j)),
                  pl.BlockSpec((None, d, tf), lambda i, j: (layer, 0, j + nf)),
                  pl.BlockSpec((None, tf, d), lambda i, j: (layer, j, 0))],
        out_specs=row,
        out_shape=jax.ShapeDtypeStruct((m, d), F32),
        scratch_shapes=[pltpu.VMEM((tm, d), BF16)],
        compiler_params=_cparams(2, FFN_VMEM_LIMIT),
        name="ffn",
    )(h, g.reshape(1, d).astype(F32), w_gu, w_gu, w_down)


_DN_T = (((1,), (1,)), ((), ()))


BAND_HEADS_PER_STEP = 4


def _band_prompt_kernel(q_ref, kp_ref, kc_ref, vp_ref, vc_ref, bias_ref, o_ref, bias_scr, *, tq, dh):
    t = pl.program_id(2)
    hp = bias_ref.shape[0]

    @pl.when((pl.program_id(1) == 0) & (t == 0))
    def _():
        bias_scr[...] = jnp.full(bias_scr.shape, NEG, F32)
        width = bias_ref.shape[3]
        for u in range(hp):
            for cc in range(tq // CHUNK):
                c0 = (cc // 2) * LANES
                bias_scr[u, cc * CHUNK:(cc + 1) * CHUNK, c0:c0 + width] = bias_ref[u, cc % 2]

    col = lax.broadcasted_iota(jnp.int32, (1, 2 * tq), 1)
    before_start = jnp.where((col < tq) & (t == 0), NEG, 0.0)
    for u in range(hp):
        hs = slice(u * dh, (u + 1) * dh)
        k = jnp.concatenate([kp_ref[:, hs], kc_ref[:, hs]], axis=0)
        v = jnp.concatenate([vp_ref[:, hs], vc_ref[:, hs]], axis=0)
        s = lax.dot_general(q_ref[:, hs], k, _DN_T, preferred_element_type=F32) + bias_scr[u] + before_start
        m = jnp.max(s, axis=1, keepdims=True)
        p = jnp.exp2(s - m)
        l = jnp.sum(p, axis=1, keepdims=True)
        o = jnp.dot(p.astype(BF16), v, preferred_element_type=F32) / l
        o_ref[:, hs] = o.astype(o_ref.dtype)


def _band_prompt(q, kv, chunk_bias, *, batch, seq, heads, dh, row0):
    assert 2 * CHUNK == LANES
    hp = BAND_HEADS_PER_STEP
    tq = chunk_bias.shape[2] - CHUNK
    nt = seq // tq
    rb0 = row0 // tq
    nh = heads // hp
    pad = jnp.full((heads, CHUNK, CHUNK), NEG, F32)
    bias2 = jnp.stack([jnp.concatenate([chunk_bias, pad], axis=2), jnp.concatenate([pad, chunk_bias], axis=2)],
                      axis=1)
    cur = lambda h, b, t: (rb0 + b * nt + t, h)
    prev = lambda h, b, t: (rb0 + b * nt + jnp.maximum(t - 1, 0), h)
    cur_v = lambda h, b, t: (rb0 + b * nt + t, nh + h)
    prev_v = lambda h, b, t: (rb0 + b * nt + jnp.maximum(t - 1, 0), nh + h)
    blk = (tq, hp * dh)
    return pl.pallas_call(
        functools.partial(_band_prompt_kernel, tq=tq, dh=dh),
        grid=(nh, batch, nt),
        in_specs=[pl.BlockSpec(blk, cur),
                  pl.BlockSpec(blk, prev), pl.BlockSpec(blk, cur),
                  pl.BlockSpec(blk, prev_v), pl.BlockSpec(blk, cur_v),
                  pl.BlockSpec((hp, 2, CHUNK, tq + 2 * CHUNK), lambda h, b, t: (h, 0, 0, 0))],
        out_specs=pl.BlockSpec(blk, lambda h, b, t: (b * nt + t, h)),
        out_shape=jax.ShapeDtypeStruct((batch * seq, heads * dh), BF16),
        scratch_shapes=[pltpu.VMEM((hp, tq, 2 * tq), F32)],
        compiler_params=_cparams(3),
        name="band_prompt",
    )(q, kv, kv, kv, kv, bias2)


def _band_sample_kernel(q_ref, kc_ref, vc_ref, kn_ref, vn_ref, bias_ref, o_ref, *, w, dh):
    for u in range(bias_ref.shape[0]):
        hs = slice(u * dh, (u + 1) * dh)
        q = q_ref[:, hs]
        kc = kc_ref[0, :, hs].astype(BF16)
        vc = vc_ref[0, :, hs].astype(BF16)
        bias = bias_ref[u]
        sc = lax.dot_general(q, kc, _DN_T, preferred_element_type=F32) + bias[:, :w]
        sn = lax.dot_general(q, kn_ref[:, hs], _DN_T, preferred_element_type=F32) + bias[:, w:]
        m = jnp.maximum(jnp.max(sc, axis=1, keepdims=True), jnp.max(sn, axis=1, keepdims=True))
        pc = jnp.exp2(sc - m)
        pn = jnp.exp2(sn - m)
        l = jnp.sum(pc, axis=1, keepdims=True) + jnp.sum(pn, axis=1, keepdims=True)
        o = (jnp.dot(pc.astype(BF16), vc, preferred_element_type=F32)
             + jnp.dot(pn.astype(BF16), vn_ref[:, hs], preferred_element_type=F32)) / l
        o_ref[:, hs] = o.astype(o_ref.dtype)


def _band_sample(q, kv, cache_k, cache_v, bias, *, batch, ts, heads, dh, row0):
    hp = BAND_HEADS_PER_STEP
    w = cache_k.shape[1]
    rb0 = row0 // ts
    nh = heads // hp
    return pl.pallas_call(
        functools.partial(_band_sample_kernel, w=w, dh=dh),
        grid=(nh, batch),
        in_specs=[pl.BlockSpec((ts, hp * dh), lambda h, b: (rb0 + b, h)),
                  pl.BlockSpec((1, w, hp * dh), lambda h, b: (b, 0, h)),
                  pl.BlockSpec((1, w, hp * dh), lambda h, b: (b, 0, h)),
                  pl.BlockSpec((ts, hp * dh), lambda h, b: (rb0 + b, h)),
                  pl.BlockSpec((ts, hp * dh), lambda h, b: (rb0 + b, nh + h)),
                  pl.BlockSpec((hp, ts, w + ts), lambda h, b: (h, 0, 0))],
        out_specs=pl.BlockSpec((ts, hp * dh), lambda h, b: (b, h)),
        out_shape=jax.ShapeDtypeStruct((batch * ts, heads * dh), BF16),
        compiler_params=_cparams(2),
        name="band_sample",
    )(q, cache_k, cache_v, kv, kv, bias)


def _band_bias(rel_bias, q0, nq, k0, nk):
    diag = np.arange(nq + nk - 1) - (nq - 1)
    idx = np.clip(q0 - k0 - diag, -C_REL_CLIP, C_REL_CLIP) + C_REL_CLIP
    r = rel_bias[idx].T.astype(F32)
    bias = jnp.stack([r[:, nq - 1 - i:nq - 1 - i + nk] for i in range(nq)], axis=1)
    qpos = q0 + np.arange(nq)[:, None]
    kpos = k0 + np.arange(nk)[None, :]
    qch, kch = qpos // CHUNK, kpos // CHUNK
    ok = (kpos >= 0) & (kch <= qch) & (qch - kch <= C_LEFT_CHUNKS)
    return jnp.where(ok[None], bias, NEG)


def _diff_lambda(lq1_ref, lk1_ref, lq2_ref, lk2_ref, lam_init):
    s1 = jnp.sum(lq1_ref[...] * lk1_ref[...], axis=1, keepdims=True)
    s2 = jnp.sum(lq2_ref[...] * lk2_ref[...], axis=1, keepdims=True)
    return jnp.exp(s1) - jnp.exp(s2) + lam_init


def _stack_groups(q_ref, c, group, dh):
    return jnp.concatenate([q_ref[:, (g * 2 + c) * dh:(g * 2 + c + 1) * dh] for g in range(group)], axis=0)


def _diff_finish(o, g_ref, o_ref, *, group, tq, out_scale):
    y = o * lax.rsqrt(jnp.mean(o * o, axis=-1, keepdims=True) + EPS) * g_ref[...] * out_scale
    y = y.astype(o_ref.dtype)
    e = y.shape[1]
    for g in range(group):
        o_ref[:, g * e:(g + 1) * e] = y[g * tq:(g + 1) * tq]


def _diff_prompt_kernel(q_ref, k_ref, vt_ref, lq1_ref, lk1_ref, lq2_ref, lk2_ref, g_ref, o_ref,
                        acc_scr, s_scr, m_scr, l_scr, *, tq, kb, group, dh, lam_init):
    t = pl.program_id(2)
    rows = group * tq
    e = 2 * dh
    lam = _diff_lambda(lq1_ref, lk1_ref, lq2_ref, lk2_ref, lam_init)
    acc_scr[...] = jnp.zeros(acc_scr.shape, F32)
    m_scr[...] = jnp.full(m_scr.shape, NEG, F32)
    l_scr[...] = jnp.zeros(l_scr.shape, F32)
    n_blocks = ((t + 1) * tq + kb - 1) // kb
    n_free = n_blocks - 1

    def scores(i, slot):
        ks = pl.multiple_of(i * kb, kb)
        kblk = k_ref[pl.ds(ks, kb), :]
        for c in range(2):
            qc = _stack_groups(q_ref, c, group, dh)
            s_scr[slot, c] = lax.dot_general(kblk[:, c * dh:(c + 1) * dh], qc, _DN_T,
                                             preferred_element_type=F32)

    def accumulate(i, slot, masked):
        ks = pl.multiple_of(i * kb, kb)
        vtb = vt_ref[0, :, pl.ds(ks, kb)]
        if masked:
            qi = lax.broadcasted_iota(jnp.int32, (kb, rows), 1) % tq
            qend = ((t * tq + qi) // CHUNK + 1) * CHUNK
            kpos = ks + lax.broadcasted_iota(jnp.int32, (kb, rows), 0)
            bias = jnp.where(kpos < qend, 0.0, NEG)
        for c in range(2):
            s = s_scr[slot, c]
            if masked:
                s = s + bias
            m = m_scr[c]
            m_new = jnp.maximum(m, jnp.max(s, axis=0, keepdims=True))
            alpha = jnp.exp2(m - m_new)
            p = jnp.exp2(s - m_new)
            l_scr[c] = alpha * l_scr[c] + jnp.sum(p, axis=0, keepdims=True)
            m_scr[c] = m_new
            acc_scr[c] = alpha * acc_scr[c] + jnp.dot(vtb, p.astype(BF16), preferred_element_type=F32)

    scores(0, 0)

    def pair(j, carry):
        i = 2 * j
        scores(i + 1, 1)
        accumulate(i, 0, False)
        scores(i + 2, 0)
        accumulate(i + 1, 1, False)
        return carry

    lax.fori_loop(0, n_free // 2, pair, 0)
    i0 = (n_free // 2) * 2

    @pl.when(n_free % 2 == 1)
    def _():
        scores(i0 + 1, 1)
        accumulate(i0, 0, False)
        accumulate(i0 + 1, 1, True)

    @pl.when(n_free % 2 == 0)
    def _():
        accumulate(i0, 0, True)

    o = acc_scr[0] / l_scr[0] - lam * (acc_scr[1] / l_scr[1])
    y = o * lax.rsqrt(jnp.mean(o * o, axis=0, keepdims=True) + EPS) * (1.0 - lam_init)
    for g in range(group):
        o_ref[:, g * e:(g + 1) * e] = (y[:, g * tq:(g + 1) * tq].T * g_ref[...]).astype(o_ref.dtype)


def _lam_specs(n_grid):
    zero = (lambda *_: (0, 0))
    return [pl.BlockSpec((1, LANES), zero)] * 4


def _diff_prompt(q, k, vt, lams, subln_g, *, batch, seq, kvh, group, dh, lam_init, tq=LANES, kb=512):
    kb = _pick(seq, kb, LANES)
    nt = seq // tq
    rows = group * tq
    e = 2 * dh
    return pl.pallas_call(
        functools.partial(_diff_prompt_kernel, tq=tq, kb=kb, group=group, dh=dh, lam_init=lam_init),
        grid=(batch, kvh, nt),
        in_specs=[pl.BlockSpec((tq, group * e), lambda b, h, t: (b * nt + t, h)),
                  pl.BlockSpec((seq, e), lambda b, h, t: (b, h)),
                  pl.BlockSpec((1, e, seq), lambda b, h, t: (b, h, 0))]
                 + _lam_specs(3) + [pl.BlockSpec((1, e), lambda b, h, t: (0, 0))],
        out_specs=pl.BlockSpec((tq, group * e), lambda b, h, t: (b * nt + t, h)),
        out_shape=jax.ShapeDtypeStruct((batch * seq, kvh * group * e), BF16),
        scratch_shapes=[pltpu.VMEM((2, e, rows), F32), pltpu.VMEM((2, 2, kb, rows), F32),
                        pltpu.VMEM((2, 1, rows), F32), pltpu.VMEM((2, 1, rows), F32)],
        compiler_params=_cparams(3),
        name="diff_prompt",
    )(q, k, vt, *lams, subln_g.reshape(1, e))


def _diff_sample_kernel(q_ref, kc_ref, vc_ref, kn_ref, vn_ref, lq1_ref, lk1_ref, lq2_ref, lk2_ref, g_ref, o_ref,
                        *, ts, group, dh, lam_init):
    lam = _diff_lambda(lq1_ref, lk1_ref, lq2_ref, lk2_ref, lam_init)
    kc = kc_ref[0].astype(BF16)
    vc = vc_ref[0].astype(BF16)
    kn = kn_ref[...]
    a_c, a_n = [], []
    for c in range(2):
        qc = _stack_groups(q_ref, c, group, dh)
        sc = lax.dot_general(qc, kc[:, c * dh:(c + 1) * dh], _DN_T, preferred_element_type=F32)
        sn = lax.dot_general(qc, kn[:, c * dh:(c + 1) * dh], _DN_T, preferred_element_type=F32)
        m = jnp.maximum(jnp.max(sc, axis=1, keepdims=True), jnp.max(sn, axis=1, keepdims=True))
        pc = jnp.exp2(sc - m)
        pn = jnp.exp2(sn - m)
        l = jnp.sum(pc, axis=1, keepdims=True) + jnp.sum(pn, axis=1, keepdims=True)
        a_c.append(pc / l)
        a_n.append(pn / l)
    ac = (a_c[0] - lam * a_c[1]).astype(BF16)
    an = (a_n[0] - lam * a_n[1]).astype(BF16)
    o = (jnp.dot(ac, vc, preferred_element_type=F32) + jnp.dot(an, vn_ref[...], preferred_element_type=F32))
    _diff_finish(o, g_ref, o_ref, group=group, tq=ts, out_scale=1.0 - lam_init)


def _diff_sample(q, k, v, cache_k, cache_v, lams, subln_g, *, batch, ts, kvh, group, dh, lam_init, row0):
    past = cache_k.shape[1]
    rb0 = row0 // ts
    e = 2 * dh
    return pl.pallas_call(
        functools.partial(_diff_sample_kernel, ts=ts, group=group, dh=dh, lam_init=lam_init),
        grid=(batch, kvh),
        in_specs=[pl.BlockSpec((ts, group * e), lambda b, h: (rb0 + b, h)),
                  pl.BlockSpec((1, past, e), lambda b, h: (b, 0, h)),
                  pl.BlockSpec((1, past, e), lambda b, h: (b, 0, h)),
                  pl.BlockSpec((ts, e), lambda b, h: (rb0 + b, h)),
                  pl.BlockSpec((ts, e), lambda b, h: (rb0 + b, h))]
                 + _lam_specs(2) + [pl.BlockSpec((1, e), lambda b, h: (0, 0))],
        out_specs=pl.BlockSpec((ts, group * e), lambda b, h: (b, h)),
        out_shape=jax.ShapeDtypeStruct((batch * ts, kvh * group * e), BF16),
        compiler_params=_cparams(2),
        name="diff_sample",
    )(q, cache_k, cache_v, k, v, *lams, subln_g)


def _dsa_kernel(iq_ref, ik_ref, iwt_ref, q_ref, k_ref, vt_ref, o_ref, key_scr, acc_scr, cut_scr,
                *, tq, kb, causal, s_real, n_sel, n_idx, kvh, group, dh, iscale):
    t = pl.program_id(1)
    s_pad = ik_ref.shape[0]
    lane = lax.broadcasted_iota(jnp.int32, (1, tq), 1)
    if causal:
        qend = ((t * tq + lane) // CHUNK + 1) * CHUNK
        n_blocks = ((t + 1) * tq + kb - 1) // kb
    else:
        qend = jnp.full((1, tq), s_real, jnp.int32)
        n_blocks = s_pad // kb

    w = iwt_ref[...] * iscale

    def score_body(i, carry):
        ks = pl.multiple_of(i * kb, kb)
        ikb = ik_ref[pl.ds(ks, kb), :]
        sc = jnp.zeros((kb, tq), F32)
        for n in range(n_idx):
            logit = lax.dot_general(ikb, iq_ref[:, n * dh:(n + 1) * dh], _DN_T, preferred_element_type=F32)
            sc = sc + w[n:n + 1, :] * jnp.maximum(logit, 0.0)
        bits = lax.bitcast_convert_type(sc, jnp.int32)
        key = jnp.where(bits < 0, bits ^ 0x7FFFFFFF, bits)
        kpos = ks + lax.broadcasted_iota(jnp.int32, (kb, tq), 0)
        key_scr[pl.ds(ks, kb), :] = jnp.where(kpos < qend, key, INT_MIN)
        return carry

    lax.fori_loop(0, n_blocks, score_body, 0)

    def count(pred):
        def body(i, acc):
            ks = pl.multiple_of(i * kb, kb)
            hit = pred(key_scr[pl.ds(ks, kb), :], ks)
            return acc + jnp.sum(hit.astype(jnp.int32), axis=0, keepdims=True)
        return lax.fori_loop(0, n_blocks, body, jnp.zeros((1, tq), jnp.int32))

    def count_ge(cand):
        return count(lambda key, ks: key >= cand)

    zero = jnp.zeros((1, tq), jnp.int32)
    n_zero = count_ge(zero)
    thr = jnp.where(n_zero >= n_sel, zero, jnp.full((1, tq), INT_MIN, jnp.int32))

    def bit_body(i, st):
        thr, n_ge = st
        cand = thr | lax.shift_left(jnp.int32(1), 30 - i)
        n_cand = count_ge(cand)
        ok = n_cand >= n_sel
        return jnp.where(ok, cand, thr), jnp.where(ok, n_cand, n_ge)

    thr, n_ge = lax.fori_loop(0, 31, bit_body, (thr, n_zero))

    n_gt = count(lambda key, ks: key > thr)
    need = n_sel - n_gt
    surplus = (n_ge - n_gt > need) & (thr > INT_MIN)
    cut_scr[...] = jnp.where(thr > INT_MIN, s_pad, 0) + jnp.zeros(cut_scr.shape, jnp.int32)

    @pl.when(jnp.max(surplus.astype(jnp.int32)) > 0)
    def _():
        def tied_before(cand):
            def pred(key, ks):
                idx = ks + lax.broadcasted_iota(jnp.int32, (kb, tq), 0)
                return (key == thr) & (idx < cand)
            return count(pred)

        def idx_body(i, c):
            cand = c | lax.shift_left(jnp.int32(1), (s_pad - 1).bit_length() - 1 - i)
            return jnp.where(tied_before(cand) < need, cand, c)

        c = lax.fori_loop(0, (s_pad - 1).bit_length(), idx_body, zero)
        cut_scr[...] = jnp.where(surplus, c + 1, cut_scr[0:1, :]) + jnp.zeros(cut_scr.shape, jnp.int32)

    cut = cut_scr[0:1, :]

    def bias_body(i, carry):
        ks = pl.multiple_of(i * kb, kb)
        key = key_scr[pl.ds(ks, kb), :]
        idx = ks + lax.broadcasted_iota(jnp.int32, (kb, tq), 0)
        sel = (key > thr) | ((key == thr) & (idx < cut))
        key_scr[pl.ds(ks, kb), :] = lax.bitcast_convert_type(jnp.where(sel, 0.0, NEG), jnp.int32)
        return carry

    lax.fori_loop(0, n_blocks, bias_body, 0)

    cols = group * tq
    for h in range(kvh):
        acc_scr[...] = jnp.zeros(acc_scr.shape, F32)

        def kv_body(i, st, h=h):
            m, l = st
            ks = pl.multiple_of(i * kb, kb)
            kblk = k_ref[pl.ds(ks, kb), h * dh:(h + 1) * dh]
            qh = jnp.concatenate([q_ref[:, (h * group + g) * dh:(h * group + g + 1) * dh] for g in range(group)],
                                 axis=0)
            bias = lax.bitcast_convert_type(key_scr[pl.ds(ks, kb), :], F32)
            s = lax.dot_general(kblk, qh, _DN_T, preferred_element_type=F32)
            s = s + jnp.concatenate([bias] * group, axis=1)
            m_new = jnp.maximum(m, jnp.max(s, axis=0, keepdims=True))
            alpha = jnp.exp2(m - m_new)
            p = jnp.exp2(s - m_new)
            l = alpha * l + jnp.sum(p, axis=0, keepdims=True)
            vtb = vt_ref[0, h * dh:(h + 1) * dh, pl.ds(ks, kb)]
            acc_scr[...] = alpha * acc_scr[...] + jnp.dot(vtb, p.astype(BF16), preferred_element_type=F32)
            return m_new, l

        _, l = lax.fori_loop(0, n_blocks, kv_body, (jnp.full((1, cols), NEG, F32), jnp.zeros((1, cols), F32)))
        out = acc_scr[...] / l
        for g in range(group):
            c0 = (h * group + g) * dh
            o_ref[:, c0:c0 + dh] = out[:, g * tq:(g + 1) * tq].T.astype(o_ref.dtype)


def _dsa(iq, ik, iwt, q, k, vt, *, batch, tq_total, tq, kb, causal, s_real, n_sel):
    dh = ik.shape[1]
    s_pad = vt.shape[2]
    kvh = k.shape[1] // dh
    heads = q.shape[1] // dh
    n_idx = iq.shape[1] // dh
    nt = tq_total // tq
    cols = heads // kvh * tq
    return pl.pallas_call(
        functools.partial(_dsa_kernel, tq=tq, kb=kb, causal=causal, s_real=s_real, n_sel=n_sel, n_idx=n_idx,
                          kvh=kvh, group=heads // kvh, dh=dh, iscale=(n_idx * dh) ** -0.5),
        grid=(batch, nt),
        in_specs=[pl.BlockSpec((tq, n_idx * dh), lambda b, t: (b * nt + t, 0)),
                  pl.BlockSpec((s_pad, dh), lambda b, t: (b, 0)),
                  pl.BlockSpec((n_idx, tq), lambda b, t: (0, b * nt + t)),
                  pl.BlockSpec((tq, heads * dh), lambda b, t: (b * nt + t, 0)),
                  pl.BlockSpec((s_pad, kvh * dh), lambda b, t: (b, 0)),
                  pl.BlockSpec((1, kvh * dh, s_pad), lambda b, t: (b, 0, 0))],
        out_specs=pl.BlockSpec((tq, heads * dh), lambda b, t: (b * nt + t, 0)),
        out_shape=jax.ShapeDtypeStruct((batch * tq_total, heads * dh), BF16),
        scratch_shapes=[pltpu.VMEM((s_pad, tq), jnp.int32), pltpu.VMEM((dh, cols), F32),
                        pltpu.VMEM((8, tq), jnp.int32)],
        compiler_params=_cparams(2),
        name="dsa",
    )(iq, ik, iwt, q, k, vt)


def _mixer_a(xn, tabs, w_in, lams, subln_g, cache_k, cache_v, lam_init, dims):
    bp, tp, bs, ts, d, past = dims
    mp = bp * tp
    dh = d // (2 * A_HEADS)
    a_q = A_HEADS * 2 * dh
    a_kv = A_KV_HEADS * 2 * dh
    group = A_HEADS // A_KV_HEADS
    (q,) = _proj(xn, w_in[:, :a_q].astype(BF16), (BF16,), rope=tabs, out_scale=dh ** -0.5 * LOG2E)
    k32, k16 = _proj(xn, w_in[:, a_q:a_q + a_kv].astype(BF16), (F32, BF16), rope=tabs)
    v32, v16 = _proj(xn, w_in[:, a_q + a_kv:].astype(BF16), (F32, BF16))
    lams = tuple(x.reshape(1, dh).astype(F32) for x in lams)
    subln_g = subln_g.astype(F32)
    vt_p = jnp.swapaxes(v16[:mp].reshape(bp, tp, a_kv), 1, 2)
    o_p = _diff_prompt(q, k16, vt_p, lams, subln_g, batch=bp, seq=tp, kvh=A_KV_HEADS, group=group, dh=dh,
                       lam_init=lam_init)
    o_s = _diff_sample(q, k16, v16, cache_k.reshape(bs, past, a_kv), cache_v.reshape(bs, past, a_kv), lams,
                       subln_g.reshape(1, 2 * dh), batch=bs, ts=ts, kvh=A_KV_HEADS, group=group, dh=dh,
                       lam_init=lam_init, row0=mp)
    return (o_p, o_s), k32, v32


def _mixer_b(xn, tabs, w_in, cache_k, cache_v, cache_ik, dims):
    bp, tp, bs, ts, d, past = dims
    mp = bp * tp
    dh = d // B_HEADS
    o1 = B_HEADS * dh
    o2 = o1 + B_KV_HEADS * dh
    o3 = o2 + B_KV_HEADS * dh
    o4 = o3 + B_IDX_HEADS * B_IDX_DIM
    o5 = o4 + B_IDX_DIM
    kvw = B_KV_HEADS * dh
    (q,) = _proj(xn, w_in[:, :o1].astype(BF16), (BF16,), rope=tabs, out_scale=dh ** -0.5 * LOG2E)
    k32, k16 = _proj(xn, w_in[:, o1:o2].astype(BF16), (F32, BF16), rope=tabs)
    v32, v16 = _proj(xn, w_in[:, o2:o3].astype(BF16), (F32, BF16))
    (iq,) = _proj(xn, w_in[:, o3:o4].astype(BF16), (BF16,), rope=tabs)
    ik32, ik16 = _proj(xn, w_in[:, o4:o5].astype(BF16), (F32, BF16), rope=tabs)
    w_iw = jnp.pad(w_in[:, o5:], ((0, 0), (0, LANES - B_IDX_HEADS))).astype(BF16)
    (iw,) = _proj(xn, w_iw, (F32,))
    iwt = iw[:, :B_IDX_HEADS].T

    vt_p = jnp.swapaxes(v16[:mp].reshape(bp, tp, kvw), 1, 2)
    tq_p = _pick(tp, 256, LANES)
    o_p = _dsa(iq, ik16, iwt, q, k16, vt_p, batch=bp, tq_total=tp, tq=tq_p, kb=_pick(tp, 512, LANES),
               causal=True, s_real=tp, n_sel=min(B_TOPK_MAX, tp // 4))

    past = cache_k.shape[1]
    s_real = past + ts
    s_pad = -(-s_real // LANES) * LANES
    tq_s = -(-ts // LANES) * LANES

    def keys(cache, new):
        new = new[mp:].reshape(bs, ts, -1)
        kk = jnp.concatenate([cache.reshape(bs, past, -1).astype(BF16), new,
                              jnp.zeros((bs, s_pad - s_real, new.shape[-1]), BF16)], axis=1)
        return kk

    def queries(a):
        a = jnp.pad(a[mp:].reshape(bs, ts, -1), ((0, 0), (0, tq_s - ts), (0, 0)))
        return a.reshape(bs * tq_s, -1)

    kk = keys(cache_k, k16).reshape(bs * s_pad, kvw)
    ikk = keys(cache_ik, ik16).reshape(bs * s_pad, B_IDX_DIM)
    vt_s = jnp.swapaxes(keys(cache_v, v16), 1, 2)
    o_s = _dsa(queries(iq), ikk, queries(iw[:, :B_IDX_HEADS]).T, queries(q), kk, vt_s, batch=bs, tq_total=tq_s,
               tq=tq_s, kb=_pick(s_pad, 512, LANES), causal=False, s_real=s_real,
               n_sel=min(B_TOPK_MAX, s_real // 4))
    o_s = o_s.reshape(bs, tq_s, o1)[:, :ts].reshape(bs * ts, o1)
    return (o_p, o_s), k32, v32, ik32


def _mixer_c(xn, w_in, rel_bias, cache_k, cache_v, dims):
    bp, tp, bs, ts, d, past = dims
    mp = bp * tp
    dh = d // C_HEADS
    hw = C_HEADS * dh
    win = C_LEFT_CHUNKS * CHUNK
    (q,) = _proj(xn, w_in[:, :hw].astype(BF16), (BF16,), out_scale=dh ** -0.5 * LOG2E)
    kv32, kv16 = _proj(xn, w_in[:, hw:].astype(BF16), (F32, BF16))
    rel_bias = rel_bias.astype(F32) * LOG2E

    chunk_bias = _band_bias(rel_bias, win, CHUNK, 0, win + CHUNK)
    o_p = _band_prompt(q, kv16, chunk_bias, batch=bp, seq=tp, heads=C_HEADS, dh=dh, row0=0)

    w_c = cache_k.shape[1]
    o_s = _band_sample(q, kv16, cache_k.reshape(bs, w_c, hw), cache_v.reshape(bs, w_c, hw),
                       _band_bias(rel_bias, past, ts, past - w_c, w_c + ts),
                       batch=bs, ts=ts, heads=C_HEADS, dh=dh, row0=mp)
    return (o_p, o_s), kv32[:, :hw], kv32[:, hw:]


def kernel(x_prompt, x_sample, cache_a_k, cache_a_v, cache_b_k, cache_b_v, cache_b_idx_k, cache_c_k, cache_c_v,
           norm_mix_g, norm_ffn_g, norm_out_g, a_w_in, a_w_out, a_lam_q1, a_lam_k1, a_lam_q2, a_lam_k2,
           a_subln_g, b_w_in, b_w_out, c_w_in, c_w_out, c_rel_bias, ffn_w_gu, ffn_w_down):
    bp, tp, d = x_prompt.shape
    bs, ts, _ = x_sample.shape
    past = cache_a_k.shape[2]
    depth = norm_mix_g.shape[0]
    assert ts <= CHUNK and past % CHUNK == 0 and tp % CHUNK == 0
    mp, ms = bp * tp, bs * ts
    dims = (bp, tp, bs, ts, d, past)
    i32 = jnp.int32

    h, xn = _join_norm(x_prompt.reshape(mp, d), x_sample.reshape(ms, d), norm_mix_g[0])
    pos =jnp.concatenate([jnp.tile(jnp.arange(tp, dtype=i32), bp), jnp.tile(past + jnp.arange(ts, dtype=i32), bs)])
    tabs = _rope_tables(pos)
    w_gu16 = ffn_w_gu.astype(BF16)
    w_down16 = ffn_w_down.astype(BF16)

    def split(a, tail):
        return a[:mp].reshape((bp, tp) + tail), a[mp:].reshape((bs, ts) + tail)

    st = {name: [] for name in ("a_k", "a_v", "b_k", "b_v", "b_ik", "c_k", "c_v")}
    for i in range(depth):
        j = i // N_MIXERS
        kind = i % N_MIXERS
        if i > 0:
            xn = _rmsnorm(h, norm_mix_g[i], BF16)
        if kind == 0:
            lam_init = 0.8 - 0.6 * math.exp(-0.3 * i)
            o, k, v = _mixer_a(xn, tabs, a_w_in[j], (a_lam_q1[j], a_lam_k1[j], a_lam_q2[j], a_lam_k2[j]),
                               a_subln_g[j], cache_a_k[j], cache_a_v[j], lam_init, dims)
            tail = cache_a_k.shape[3:]
            st["a_k"].append(split(k, tail))
            st["a_v"].append(split(v, tail))
            w_out = a_w_out[j]
        elif kind == 1:
            o, k, v, ik = _mixer_b(xn, tabs, b_w_in[j], cache_b_k[j], cache_b_v[j], cache_b_idx_k[j], dims)
            tail = cache_b_k.shape[3:]
            st["b_k"].append(split(k, tail))
            st["b_v"].append(split(v, tail))
            st["b_ik"].append(split(ik, cache_b_idx_k.shape[3:]))
            w_out = b_w_out[j]
        else:
            o, k, v = _mixer_c(xn, c_w_in[j], c_rel_bias[j], cache_c_k[j], cache_c_v[j], dims)
            tail = cache_c_k.shape[3:]
            keep = min(C_LEFT_CHUNKS * CHUNK, tp)
            for name, a, cache in (("c_k", k, cache_c_k[j]), ("c_v", v, cache_c_v[j])):
                a_p, a_s = split(a, tail)
                kk_s = jnp.concatenate([cache, a_s], axis=1)
                st[name].append((a_p[:, tp - keep:], kk_s[:, kk_s.shape[1] - cache.shape[1]:]))
            w_out = c_w_out[j]
        h = _out_proj(o[0], o[1], w_out.astype(BF16), h)
        h = _ffn(h, norm_ffn_g[i], w_gu16, w_down16, i)

    y_p = _rmsnorm(h, norm_out_g, F32, 0, mp).reshape(bp, tp, d)
    y_s = _rmsnorm(h, norm_out_g, F32, mp, ms).reshape(bs, ts, d)
    stack = lambda name, g: jnp.stack([pair[g] for pair in st[name]], axis=0)
    names = ("a_k", "a_v", "b_k", "b_v", "b_ik", "c_k", "c_v")
    return (y_p, y_s) + tuple(stack(n, 0) for n in names) + tuple(stack(n, 1) for n in names)
```
